```python
import math
import jax, jax.numpy as jnp
from jax import lax
import numpy as np

D_MODEL = 1024
BATCH = 8
SEQ = 2048
DEPTH = 1

MEM_TOKENS = 256
MLSTM_WIDTH = D_MODEL
MLSTM_HEADS = 4
MLSTM_HEAD_DIM = MLSTM_WIDTH // MLSTM_HEADS
MLSTM_CONV = 4
MLSTM_QKV_BLOCK = 4
MLSTM_CHUNK = 64
MOBA_WIDTH = D_MODEL // 2
MOBA_HEADS = 4
MOBA_HEAD_DIM = MOBA_WIDTH // MOBA_HEADS
MOBA_BLOCK = 256
MOBA_TOP_K = 3
MOBA_QUERY_GROUP = 32
MEM_WIDTH = D_MODEL // 2
MEM_HEADS = 4
MEM_HEAD_DIM = MEM_WIDTH // MEM_HEADS

MIX_WIDTH = MLSTM_WIDTH + MOBA_WIDTH + MEM_WIDTH
IN_SPLITS = (MLSTM_WIDTH, MLSTM_WIDTH, MOBA_WIDTH, MOBA_WIDTH, MOBA_WIDTH, MOBA_WIDTH, MEM_WIDTH, MEM_WIDTH)
IN_WIDTH = 2 * MLSTM_WIDTH + 4 * MOBA_WIDTH + 2 * MEM_WIDTH

ROPE_THETA = 10000.0
DEEPNORM_ALPHA = (2 * DEPTH) ** 0.25
DEEPNORM_BETA = (8 * DEPTH) ** -0.25
LN_EPS = 1e-5

kernel_name = 'hymba_mlstm_moba_memxattn_deepnorm'


def _split_cols(a, sizes):
    outs, off = [], 0
    for s in sizes:
        outs.append(a[..., off:off + s])
        off += s
    return outs


def _layer_norm(x, g, b):
    xf = x.astype(jnp.float32)
    mu = jnp.mean(xf, axis=-1, keepdims=True)
    var = jnp.mean(jnp.square(xf - mu), axis=-1, keepdims=True)
    return (xf - mu) * lax.rsqrt(var + LN_EPS) * g.astype(jnp.float32) + b.astype(jnp.float32)


def _rotary(x, positions):
    d = x.shape[-1]
    half = d // 2
    inv_freq = ROPE_THETA ** (-jnp.arange(half, dtype=jnp.float32) * 2.0 / d)
    ang = positions.astype(jnp.float32)[..., None] * inv_freq
    cos = jnp.cos(ang)[:, :, None, :]
    sin = jnp.sin(ang)[:, :, None, :]
    xf = x.astype(jnp.float32)
    x1, x2 = xf[..., :half], xf[..., half:]
    return jnp.concatenate([x1 * cos - x2 * sin, x2 * cos + x1 * sin], axis=-1).astype(x.dtype)


def _causal_conv(x, w, b):
    k_w = w.shape[0]
    s = x.shape[1]
    xp = jnp.pad(x, ((0, 0), (k_w - 1, 0), (0, 0)))
    out = b
    for j in range(k_w):
        out = out + xp[:, j:j + s] * w[j]
    return out


def _blockdiag(x, w):
    g = w.shape[0]
    xs = x.reshape(*x.shape[:-1], g, -1)
    return jnp.einsum('bsgi,gio->bsgo', xs, w).reshape(x.shape)


def _mlstm_chunkwise(q, k, v, i_pre, log_f):
    bsz, nh, s, d = q.shape
    L = MLSTM_CHUNK
    nc = s // L

    def to_chunks(a):
        return jnp.moveaxis(a.reshape(bsz, nh, nc, L, *a.shape[3:]), 2, 0)

    causal = jnp.tril(jnp.ones((L, L), dtype=bool))

    def step(carry, inp):
        c_prev, n_prev, m_prev = carry
        qc, kc, vc, ic, fc = inp
        b = jnp.cumsum(fc, axis=-1)
        log_d = jnp.where(causal, b[..., :, None] - b[..., None, :] + ic[..., None, :], -jnp.inf)
        m_inter = b + m_prev[..., None]
        m_row = jnp.maximum(m_inter, jnp.max(log_d, axis=-1))
        w_inter = jnp.exp(m_inter - m_row)
        s_qk = jnp.einsum('bhld,bhsd->bhls', qc, kc) * jnp.exp(log_d - m_row[..., None])
        num = (w_inter[..., None] * jnp.einsum('bhld,bhde->bhle', qc, c_prev)
               + jnp.einsum('bhls,bhse->bhle', s_qk, vc))
        den = w_inter * jnp.einsum('bhld,bhd->bhl', qc, n_prev) + jnp.sum(s_qk, axis=-1)
        h = num / jnp.maximum(jnp.abs(den), jnp.exp(-m_row))[..., None]
        b_end = b[..., -1]
        log_w = b_end[..., None] - b + ic
        m_new = jnp.maximum(b_end + m_prev, jnp.max(log_w, axis=-1))
        decay = jnp.exp(b_end + m_prev - m_new)
        w_in = jnp.exp(log_w - m_new[..., None])
        c_new = decay[..., None, None] * c_prev + jnp.einsum('bhl,bhld,bhle->bhde', w_in, kc, vc)
        n_new = decay[..., None] * n_prev + jnp.einsum('bhl,bhld->bhd', w_in, kc)
        return (c_new, n_new, m_new), h

    init = (jnp.zeros((bsz, nh, d, d), q.dtype), jnp.zeros((bsz, nh, d), q.dtype),
            jnp.zeros((bsz, nh), q.dtype))
    _, h = lax.scan(step, init, (to_chunks(q), to_chunks(k), to_chunks(v),
                                 to_chunks(i_pre), to_chunks(log_f)))
    return jnp.moveaxis(h, 0, 2).reshape(bsz, nh, s, d)


def _moba_attention(q, k, v):
    bsz, s, nh, d = q.shape
    bs = MOBA_BLOCK
    qg = MOBA_QUERY_GROUP
    nb = -(-s // bs)
    sp = nb * bs
    pad = sp - s
    padw = ((0, 0), (0, 0), (0, pad), (0, 0))
    qh = jnp.pad(q.transpose(0, 2, 1, 3), padw).astype(jnp.float32)
    kh = jnp.pad(k.transpose(0, 2, 1, 3), padw).astype(jnp.float32)
    vh = jnp.pad(v.transpose(0, 2, 1, 3), padw).astype(jnp.float32)
    k_blocks = kh.reshape(bsz, nh, nb, bs, d)
    v_blocks = vh.reshape(bsz, nh, nb, bs, d)
    scale = d ** -0.5
    n_sel = min(MOBA_TOP_K, nb - 1)
    qblk_all = jnp.arange(sp) // bs

    if n_sel > 0:
        k_mean = jnp.mean(k_blocks, axis=3)
        gate = jnp.einsum('bhtd,bhnd->bhtn', qh, k_mean)
        fully_past = jnp.arange(nb)[None, :] < qblk_all[:, None]
        gate = jnp.where(fully_past, gate, -jnp.inf)
        _, sel = lax.top_k(gate, n_sel)
        sel_valid = sel < qblk_all[:, None]
    b_ix = jnp.arange(bsz)[:, None, None, None]
    h_ix = jnp.arange(nh)[None, :, None, None]

    def group(c):
        start = c * qg
        blk = start // bs
        q_c = lax.dynamic_slice_in_dim(qh, start, qg, axis=2)
        k_own = lax.dynamic_index_in_dim(k_blocks, blk, axis=2, keepdims=False)
        v_own = lax.dynamic_index_in_dim(v_blocks, blk, axis=2, keepdims=False)
        qpos = start + jnp.arange(qg)
        kpos = blk * bs + jnp.arange(bs)
        s_own = jnp.einsum('bhqd,bhkd->bhqk', q_c, k_own) * scale
        s_own = jnp.where(kpos[None, :] <= qpos[:, None], s_own, -jnp.inf)
        if n_sel > 0:
            idx = lax.dynamic_slice_in_dim(sel, start, qg, axis=2)
            ok = lax.dynamic_slice_in_dim(sel_valid, start, qg, axis=2)
            k_sel = k_blocks[b_ix, h_ix, idx]
            v_sel = v_blocks[b_ix, h_ix, idx]
            s_past = jnp.einsum('bhqd,bhqnkd->bhqnk', q_c, k_sel) * scale
            s_past = jnp.where(ok[..., None], s_past, -jnp.inf).reshape(bsz, nh, qg, n_sel * bs)
            p = jax.nn.softmax(jnp.concatenate([s_past, s_own], axis=-1), axis=-1)
            p_past = p[..., :n_sel * bs].reshape(bsz, nh, qg, n_sel, bs)
            p_own = p[..., n_sel * bs:]
            out = (jnp.einsum('bhqnk,bhqnkd->bhqd', p_past, v_sel)
                   + jnp.einsum('bhqk,bhkd->bhqd', p_own, v_own))
        else:
            p_own = jax.nn.softmax(s_own, axis=-1)
            out = jnp.einsum('bhqk,bhkd->bhqd', p_own, v_own)
        return out

    out = lax.map(group, jnp.arange(sp // qg))
    out = jnp.moveaxis(out, 0, 2).reshape(bsz, nh, sp, d)[:, :, :s]
    return out.transpose(0, 2, 1, 3).astype(q.dtype)


def _memory_attention(q, mem, w_mem_kv):
    bsz, m, _ = mem.shape
    kv = mem @ w_mem_kv
    k_m, v_m = _split_cols(kv, (MEM_WIDTH, MEM_WIDTH))
    k_m = k_m.reshape(bsz, m, MEM_HEADS, MEM_HEAD_DIM).astype(jnp.float32)
    v_m = v_m.reshape(bsz, m, MEM_HEADS, MEM_HEAD_DIM).astype(jnp.float32)
    s = jnp.einsum('bshd,bmhd->bhsm', q.astype(jnp.float32), k_m) * (MEM_HEAD_DIM ** -0.5)
    p = jax.nn.softmax(s, axis=-1)
    return jnp.einsum('bhsm,bmhd->bshd', p, v_m).astype(q.dtype)


def setup_inputs(seed: int = 0) -> dict:
    key = jax.random.key(seed)
    ks = jax.random.split(key, 20)
    f32 = jnp.float32
    nrm = lambda k, shp: jax.random.normal(k, shp, dtype=f32)
    x = nrm(ks[0], (BATCH, SEQ, D_MODEL))
    mem = nrm(ks[1], (BATCH, MEM_TOKENS, D_MODEL))
    positions = jnp.broadcast_to(jnp.arange(SEQ, dtype=jnp.int32)[None, :], (BATCH, SEQ))
    w_in = nrm(ks[2], (D_MODEL, IN_WIDTH)) * D_MODEL ** -0.5
    mlstm_conv_w = nrm(ks[3], (MLSTM_CONV, MLSTM_WIDTH)) * MLSTM_CONV ** -0.5
    mlstm_conv_b = 0.01 * nrm(ks[4], (MLSTM_WIDTH,))
    nblk = MLSTM_WIDTH // MLSTM_QKV_BLOCK
    bshape = (nblk, MLSTM_QKV_BLOCK, MLSTM_QKV_BLOCK)
    mlstm_wq = nrm(ks[5], bshape) * MLSTM_QKV_BLOCK ** -0.5
    mlstm_wk = nrm(ks[6], bshape) * MLSTM_QKV_BLOCK ** -0.5
    mlstm_wv = nrm(ks[7], bshape) * MLSTM_QKV_BLOCK ** -0.5
    mlstm_w_gates = nrm(ks[8], (3 * MLSTM_WIDTH, 2 * MLSTM_HEADS)) * (3 * MLSTM_WIDTH) ** -0.5
    b_i = 0.1 * nrm(ks[9], (MLSTM_HEADS,))
    b_f = jnp.linspace(3.0, 6.0, MLSTM_HEADS, dtype=f32) + 0.01 * nrm(ks[10], (MLSTM_HEADS,))
    mlstm_b_gates = jnp.concatenate([b_i, b_f])
    mlstm_norm_g = 1.0 + 0.02 * nrm(ks[11], (MLSTM_WIDTH,))
    mlstm_skip = 1.0 + 0.02 * nrm(ks[12], (MLSTM_WIDTH,))
    w_mem_kv = nrm(ks[13], (D_MODEL, 2 * MEM_WIDTH)) * D_MODEL ** -0.5
    w_out = nrm(ks[14], (MIX_WIDTH, D_MODEL)) * MIX_WIDTH ** -0.5 * DEEPNORM_BETA
    ln_g = 1.0 + 0.02 * nrm(ks[15], (D_MODEL,))
    ln_b = 0.02 * nrm(ks[16], (D_MODEL,))
    return {'x': x, 'mem': mem, 'positions': positions, 'w_in': w_in,
            'mlstm_conv_w': mlstm_conv_w, 'mlstm_conv_b': mlstm_conv_b,
            'mlstm_wq': mlstm_wq, 'mlstm_wk': mlstm_wk, 'mlstm_wv': mlstm_wv,
            'mlstm_w_gates': mlstm_w_gates, 'mlstm_b_gates': mlstm_b_gates,
            'mlstm_norm_g': mlstm_norm_g, 'mlstm_skip': mlstm_skip,
            'w_mem_kv': w_mem_kv, 'w_out': w_out, 'ln_g': ln_g, 'ln_b': ln_b}


def reference(x, mem, positions, w_in, mlstm_conv_w, mlstm_conv_b, mlstm_wq, mlstm_wk, mlstm_wv,
              mlstm_w_gates, mlstm_b_gates, mlstm_norm_g, mlstm_skip, w_mem_kv, w_out, ln_g, ln_b):
    bsz, s, _ = x.shape
    for _layer in range(DEPTH):
        proj = x @ w_in
        x_m, z_m, q_a, k_a, v_a, z_a, q_c, z_c = _split_cols(proj, IN_SPLITS)

        x_conv = jax.nn.silu(_causal_conv(x_m, mlstm_conv_w, mlstm_conv_b))
        q_m = _blockdiag(x_conv, mlstm_wq)
        k_m = _blockdiag(x_conv, mlstm_wk)
        v_m = _blockdiag(x_m, mlstm_wv)
        gates = jnp.concatenate([q_m, k_m, v_m], axis=-1) @ mlstm_w_gates + mlstm_b_gates
        gates = gates.astype(jnp.float32).transpose(0, 2, 1)
        i_pre, f_pre = gates[:, :MLSTM_HEADS], gates[:, MLSTM_HEADS:]
        heads = lambda a: a.reshape(bsz, s, MLSTM_HEADS, MLSTM_HEAD_DIM).transpose(0, 2, 1, 3).astype(jnp.float32)
        h = _mlstm_chunkwise(heads(q_m), heads(k_m) * (MLSTM_HEAD_DIM ** -0.5), heads(v_m),
                             i_pre, jax.nn.log_sigmoid(f_pre))
        mu = jnp.mean(h, axis=-1, keepdims=True)
        var = jnp.mean(jnp.square(h - mu), axis=-1, keepdims=True)
        h = ((h - mu) * lax.rsqrt(var + LN_EPS)).transpose(0, 2, 1, 3).reshape(bsz, s, MLSTM_WIDTH)
        h = (h * mlstm_norm_g.astype(jnp.float32)).astype(x.dtype)
        out_m = (h + mlstm_skip * x_conv) * jax.nn.silu(z_m)

        qa = _rotary(q_a.reshape(bsz, s, MOBA_HEADS, MOBA_HEAD_DIM), positions)
        ka = _rotary(k_a.reshape(bsz, s, MOBA_HEADS, MOBA_HEAD_DIM), positions)
        va = v_a.reshape(bsz, s, MOBA_HEADS, MOBA_HEAD_DIM)
        out_a = _moba_attention(qa, ka, va).reshape(bsz, s, MOBA_WIDTH) * jax.nn.silu(z_a)

        qc = q_c.reshape(bsz, s, MEM_HEADS, MEM_HEAD_DIM)
        out_c = _memory_attention(qc, mem, w_mem_kv).reshape(bsz, s, MEM_WIDTH) * jax.nn.silu(z_c)

        mixed = jnp.concatenate([out_m, out_a, out_c], axis=-1) @ w_out
        x = _layer_norm(DEEPNORM_ALPHA * x + mixed, ln_g, ln_b).astype(x.dtype)
    return x
```

```python
import functools
import math

import jax
import jax.numpy as jnp
from jax import lax
from jax.experimental import pallas as pl
from jax.experimental.pallas import tpu as pltpu

MOBA_HEADS = 4
MOBA_BLOCK = 256
MOBA_TOP_K = 3
MEM_HEADS = 4
ROPE_THETA = 10000.0
DEPTH = 1
DEEPNORM_ALPHA = (2 * DEPTH) ** 0.25
LN_EPS = 1e-5

V7X_LANES = 128
V7X_SUBLANES = 8
V7X_VMEM_LIMIT_BYTES = 56 * 1024 * 1024

TOKEN_TILE = MOBA_BLOCK
MASK_VALUE = -1e30

F32 = jnp.float32
BF16 = jnp.bfloat16


def _dot(a, b):
    return jnp.dot(a, b, preferred_element_type=F32)


def _dot_nt(a, b):
    return lax.dot_general(a, b, (((1,), (1,)), ((), ())), preferred_element_type=F32)


def _split3(v):
    hi = v.astype(BF16)
    r1 = v - hi.astype(F32)
    mid = r1.astype(BF16)
    lo = (r1 - mid.astype(F32)).astype(BF16)
    return hi, mid, lo


def _silu(v):
    return v * jax.nn.sigmoid(v)


def _log_sigmoid(v):
    return jnp.minimum(v, 0.0) - jnp.log1p(jnp.exp(-jnp.abs(v)))


def _proj_kernel(x_ref, pos_ref, win_ref, convw_ref, convb_ref, wq_ref, wk_ref, wkt_ref, wv_ref,
                 wg_ref, bg_ref, normg_ref, skip_ref, invf_ref, sgn_ref,
                 qm_ref, kmt_ref, vm_ref, add_ref, mul_ref, gcol_ref, grow_ref,
                 qa_ref, ka_ref, va_ref, sza_ref, selb_ref, qc_ref, szc_ref,
                 xpad_scr, kmean_scr,
                 *, ml_width, ml_heads, moba_width, mem_width, n_sel, gate_slots):
    t = pl.program_id(1)
    tm = x_ref.shape[0]
    hd = ml_width // ml_heads
    moba_hd = moba_width // MOBA_HEADS
    mem_hd = mem_width // MEM_HEADS
    c_zm = ml_width
    c_qa = 2 * ml_width
    c_ka = c_qa + moba_width
    c_va = c_ka + moba_width
    c_za = c_va + moba_width
    c_qc = c_za + moba_width
    c_zc = c_qc + mem_width

    xb = x_ref[...].astype(BF16)

    xm = _dot(xb, win_ref[:, 0:ml_width])
    pad = xpad_scr.shape[0] - tm

    @pl.when(t == 0)
    def _():
        xpad_scr[0:pad, :] = jnp.zeros((pad, ml_width), F32)

    @pl.when(t > 0)
    def _():
        xpad_scr[0:pad, :] = xpad_scr[tm:tm + pad, :]

    xpad_scr[pad:pad + tm, :] = xm
    k_w = convw_ref.shape[0]
    conv = convb_ref[...] + xm * convw_ref[k_w - 1:k_w, :]
    for j in range(k_w - 1):
        conv = conv + xpad_scr[pl.ds(pad - (k_w - 1) + j, tm), :] * convw_ref[j:j + 1, :]
    xc = _silu(conv)

    sz = _silu(_dot(xb, win_ref[:, c_zm:c_zm + ml_width]))
    add_ref[...] = (skip_ref[...] * xc * sz).astype(BF16)
    mul_ref[...] = (normg_ref[...] * sz).astype(BF16)

    g = jnp.zeros((tm, V7X_LANES), F32) + bg_ref[...]
    for h in range(ml_heads):
        hs = slice(h * hd, (h + 1) * hd)
        xc_h = xc[:, hs].astype(BF16)
        xm_h = xm[:, hs].astype(BF16)
        q_h = _dot(xc_h, wq_ref[h]).astype(BF16)
        k_h = _dot(xc_h, wk_ref[h]).astype(BF16)
        v_h = _dot(xm_h, wv_ref[h]).astype(BF16)
        qm_ref[:, hs] = q_h
        vm_ref[:, hs] = v_h
        kmt_ref[hs, :] = (_dot_nt(wkt_ref[h], xc_h) * (hd ** -0.5)).astype(BF16)
        g = g + _dot(q_h, wg_ref[hs, :])
        g = g + _dot(k_h, wg_ref[ml_width + h * hd:ml_width + (h + 1) * hd, :])
        g = g + _dot(v_h, wg_ref[2 * ml_width + h * hd:2 * ml_width + (h + 1) * hd, :])

    lane = lax.broadcasted_iota(jnp.int32, (tm, V7X_LANES), 1)
    logf = jnp.where(lane >= ml_heads, _log_sigmoid(g), 0.0)
    row_i = lax.broadcasted_iota(jnp.int32, (tm, tm), 0)
    col_i = lax.broadcasted_iota(jnp.int32, (tm, tm), 1)
    tril = (row_i >= col_i).astype(BF16)
    f_hi, f_mid, f_lo = _split3(logf)
    bcum = _dot(tril, f_hi) + _dot(tril, f_mid) + _dot(tril, f_lo)
    gcol = jnp.where(lane < ml_heads, g, bcum)
    gcol_ref[...] = gcol
    grow_ref[...] = jnp.transpose(gcol)[0:grow_ref.shape[0], :]

    ang = pos_ref[...] * invf_ref[...]
    cosf = jnp.cos(ang)
    sins = jnp.sin(ang) * sgn_ref[...]
    qa = _dot(xb, win_ref[:, c_qa:c_qa + moba_width])
    ka = _dot(xb, win_ref[:, c_ka:c_ka + moba_width])

    @pl.when(t == 0)
    def _():
        kmean_scr[...] = jnp.zeros(kmean_scr.shape, F32)

    nb_lanes = gate_slots
    gate = jnp.zeros((tm, V7X_LANES), F32)
    kmean_rows = []
    for h in range(MOBA_HEADS):
        hs = slice(h * moba_hd, (h + 1) * moba_hd)
        q_h = qa[:, hs]
        k_h = ka[:, hs]
        q_rot = q_h * cosf + pltpu.roll(q_h, moba_hd // 2, 1) * sins
        k_rot = k_h * cosf + pltpu.roll(k_h, moba_hd // 2, 1) * sins
        qa_ref[:, hs] = (q_rot * (moba_hd ** -0.5)).astype(BF16)
        ka_ref[:, hs] = k_rot.astype(BF16)
        kmean_rows.append(jnp.mean(k_rot, axis=0, keepdims=True))
        km = kmean_scr[:, hs]
        q_hi = q_rot.astype(BF16)
        q_lo = (q_rot - q_hi.astype(F32)).astype(BF16)
        k_hi = km.astype(BF16)
        k_lo = (km - k_hi.astype(F32)).astype(BF16)
        gate = gate + _dot_nt(q_hi, k_hi) + _dot_nt(q_lo, k_hi) + _dot_nt(q_hi, k_lo)
    km_row = lax.broadcasted_iota(jnp.int32, kmean_scr.shape, 0)
    km_head = lax.broadcasted_iota(jnp.int32, kmean_scr.shape, 1) // moba_hd
    kmean_scr[...] = jnp.where(km_row == km_head * nb_lanes + t,
                               jnp.concatenate(kmean_rows, axis=1), kmean_scr[...])

    n_l = lane & (nb_lanes - 1)
    valid = (n_l < t) & (lane < MOBA_HEADS * nb_lanes)
    gate = jnp.where(valid, gate, -jnp.inf)
    cnt = jnp.zeros((tm, V7X_LANES), jnp.int32)
    for r in range(1, nb_lanes):
        up = pltpu.roll(gate, r, 1)
        cnt = cnt + ((n_l >= r) & (up >= gate)).astype(jnp.int32)
        dn = pltpu.roll(gate, V7X_LANES - r, 1)
        cnt = cnt + ((n_l < nb_lanes - r) & (dn > gate)).astype(jnp.int32)
    selb = jnp.where(valid & (cnt < n_sel), 0.0, MASK_VALUE)
    for h in range(MOBA_HEADS):
        shift = (V7X_LANES - h * nb_lanes) % V7X_LANES
        sel_h = selb if shift == 0 else pltpu.roll(selb, shift, 1)
        selb_ref[:, h * V7X_LANES:(h + 1) * V7X_LANES] = sel_h

    va_ref[...] = _dot(xb, win_ref[:, c_va:c_va + moba_width]).astype(BF16)
    sza_ref[...] = _silu(_dot(xb, win_ref[:, c_za:c_za + moba_width])).astype(BF16)

    qc_ref[...] = (_dot(xb, win_ref[:, c_qc:c_qc + mem_width]) * (mem_hd ** -0.5)).astype(BF16)
    szc_ref[...] = _silu(_dot(xb, win_ref[:, c_zc:c_zc + mem_width])).astype(BF16)


def _resident(shape):
    nd = len(shape)
    return pl.BlockSpec(shape, lambda b, t, _nd=nd: (0,) * _nd)


def _proj_call(x, pos, win, convw, convb, wq, wk, wkt, wv, wg, bg, normg, skip, invf, sgn,
               *, ml_width, ml_heads, moba_width, mem_width):
    bsz, s, d = x.shape
    tm = TOKEN_TILE
    nt = s // tm
    n_sel = min(MOBA_TOP_K, nt - 1)
    gate_slots = max(V7X_SUBLANES, pl.next_power_of_2(nt))
    assert MOBA_HEADS * gate_slots <= V7X_LANES and s % tm == 0
    tok = lambda w: pl.BlockSpec((None, tm, w), lambda b, t: (b, t, 0))
    out_shapes = (
        jax.ShapeDtypeStruct((bsz, s, ml_width), BF16),
        jax.ShapeDtypeStruct((bsz, ml_width, s), BF16),
        jax.ShapeDtypeStruct((bsz, s, ml_width), BF16),
        jax.ShapeDtypeStruct((bsz, s, ml_width), BF16),
        jax.ShapeDtypeStruct((bsz, s, ml_width), BF16),
        jax.ShapeDtypeStruct((bsz, s, V7X_LANES), F32),
        jax.ShapeDtypeStruct((bsz, V7X_SUBLANES, s), F32),
        jax.ShapeDtypeStruct((bsz, s, moba_width), BF16),
        jax.ShapeDtypeStruct((bsz, s, moba_width), BF16),
        jax.ShapeDtypeStruct((bsz, s, moba_width), BF16),
        jax.ShapeDtypeStruct((bsz, s, moba_width), BF16),
        jax.ShapeDtypeStruct((bsz, s, MOBA_HEADS * V7X_LANES), F32),
        jax.ShapeDtypeStruct((bsz, s, mem_width), BF16),
        jax.ShapeDtypeStruct((bsz, s, mem_width), BF16),
    )
    out_specs = (
        tok(ml_width),
        pl.BlockSpec((None, ml_width, tm), lambda b, t: (b, 0, t)),
        tok(ml_width), tok(ml_width), tok(ml_width),
        tok(V7X_LANES),
        pl.BlockSpec((None, V7X_SUBLANES, tm), lambda b, t: (b, 0, t)),
        tok(moba_width), tok(moba_width), tok(moba_width), tok(moba_width),
        tok(MOBA_HEADS * V7X_LANES),
        tok(mem_width), tok(mem_width),
    )
    in_specs = [tok(d), tok(1)] + [_resident(a.shape) for a in
                                   (win, convw, convb, wq, wk, wkt, wv, wg, bg, normg, skip, invf, sgn)]
    kern = functools.partial(_proj_kernel, ml_width=ml_width, ml_heads=ml_heads,
                             moba_width=moba_width, mem_width=mem_width, n_sel=n_sel,
                             gate_slots=gate_slots)
    return pl.pallas_call(
        kern,
        grid=(bsz, nt),
        in_specs=in_specs,
        out_specs=out_specs,
        out_shape=out_shapes,
        scratch_shapes=[pltpu.VMEM((tm + V7X_SUBLANES, ml_width), F32),
                        pltpu.VMEM((V7X_LANES, moba_width), F32)],
        compiler_params=pltpu.CompilerParams(dimension_semantics=("arbitrary", "arbitrary"),
                                             vmem_limit_bytes=V7X_VMEM_LIMIT_BYTES),
        name="proj",
    )(x, pos, win, convw, convb, wq, wk, wkt, wv, wg, bg, normg, skip, invf, sgn)


def _mlstm_kernel(qm_ref, kmt_ref, vm_ref, gcol_ref, grow_ref, add_ref, mul_ref, out_ref,
                  c_scr, n_scr, m_scr, *, heads):
    c = pl.program_id(1)
    L = qm_ref.shape[0]
    hd = qm_ref.shape[1] // heads

    @pl.when(c == 0)
    def _():
        c_scr[...] = jnp.zeros(c_scr.shape, F32)
        n_scr[...] = jnp.zeros(n_scr.shape, F32)
        m_scr[...] = jnp.zeros(m_scr.shape, F32)

    row_i = lax.broadcasted_iota(jnp.int32, (L, L), 0)
    col_i = lax.broadcasted_iota(jnp.int32, (L, L), 1)
    causal = row_i >= col_i
    gcol = gcol_ref[...]
    grow = grow_ref[...]
    for h in range(heads):
        hs = slice(h * hd, (h + 1) * hd)
        q = qm_ref[:, hs]
        kt = kmt_ref[hs, :]
        v = vm_ref[:, hs]
        i_row = grow[h:h + 1, :]
        b_row = grow[heads + h:heads + h + 1, :]
        b_col = gcol[:, heads + h:heads + h + 1]
        m_prev = m_scr[h][0:1, 0:1]
        c_prev = c_scr[h]
        n_prev = n_scr[h][0:1, :]

        log_d = jnp.where(causal, b_col - b_row + i_row, -jnp.inf)
        m_inter = b_col + m_prev
        m_row = jnp.maximum(m_inter, jnp.max(log_d, axis=1, keepdims=True))
        w_inter = jnp.exp(m_inter - m_row)
        s_qk = _dot(q, kt) * jnp.exp(log_d - m_row)
        num = w_inter * _dot(q, c_prev.astype(BF16)) + _dot(s_qk.astype(BF16), v)
        den = (w_inter * jnp.sum(q.astype(F32) * n_prev, axis=1, keepdims=True)
               + jnp.sum(s_qk, axis=1, keepdims=True))
        hh = num * (1.0 / jnp.maximum(jnp.abs(den), jnp.exp(-m_row)))
        mu = jnp.mean(hh, axis=1, keepdims=True)
        hc = hh - mu
        var = jnp.mean(hc * hc, axis=1, keepdims=True)
        hn = hc * lax.rsqrt(var + LN_EPS)
        out_ref[:, hs] = (hn * mul_ref[:, hs].astype(F32) + add_ref[:, hs].astype(F32)).astype(BF16)

        b_end = b_row[:, L - 1:L]
        log_w = b_end - b_row + i_row
        m_new = jnp.maximum(b_end + m_prev, jnp.max(log_w, axis=1, keepdims=True))
        decay = jnp.exp(b_end + m_prev - m_new)
        w_in = jnp.exp(log_w - m_new)
        kw = (kt.astype(F32) * w_in).astype(BF16)
        c_scr[h] = decay * c_prev + _dot(kw, v)
        w8 = jnp.broadcast_to(w_in, (V7X_SUBLANES, L)).astype(BF16)
        n_scr[h] = decay * n_scr[h] + _dot_nt(w8, kt)
        m_scr[h] = jnp.broadcast_to(m_new, m_scr.shape[1:])


def _mlstm_call(qm, kmt, vm, gcol, grow, add, mul, *, heads):
    bsz, s, w = qm.shape
    L = TOKEN_TILE
    hd = w // heads
    tok = lambda width: pl.BlockSpec((None, L, width), lambda b, c: (b, c, 0))
    tmaj = lambda rows: pl.BlockSpec((None, rows, L), lambda b, c: (b, 0, c))
    return pl.pallas_call(
        functools.partial(_mlstm_kernel, heads=heads),
        grid=(bsz, s // L),
        in_specs=[tok(w), tmaj(w), tok(w), tok(V7X_LANES), tmaj(V7X_SUBLANES), tok(w), tok(w)],
        out_specs=tok(w),
        out_shape=jax.ShapeDtypeStruct((bsz, s, w), BF16),
        scratch_shapes=[pltpu.VMEM((heads, hd, hd), F32),
                        pltpu.VMEM((heads, V7X_SUBLANES, hd), F32),
                        pltpu.VMEM((heads, V7X_SUBLANES, V7X_LANES), F32)],
        compiler_params=pltpu.CompilerParams(dimension_semantics=("arbitrary", "arbitrary"),
                                             vmem_limit_bytes=V7X_VMEM_LIMIT_BYTES),
        name="mlstm",
    )(qm, kmt, vm, gcol, grow, add, mul)


def _moba_kernel(q_ref, k_ref, v_ref, selb_ref, sz_ref, out_ref):
    j = pl.program_id(2)
    tq, hd = q_ref.shape
    bs = MOBA_BLOCK
    q = q_ref[...]
    selb = selb_ref[...]
    lane = lax.broadcasted_iota(jnp.int32, selb.shape, 1)

    own = pl.multiple_of(j * bs, bs)
    row_i = lax.broadcasted_iota(jnp.int32, (tq, bs), 0)
    col_i = lax.broadcasted_iota(jnp.int32, (tq, bs), 1)
    s0 = jnp.where(row_i >= col_i, _dot_nt(q, k_ref[pl.ds(own, bs), :]), MASK_VALUE)
    m0 = jnp.max(s0, axis=1, keepdims=True)
    p0 = jnp.exp(s0 - m0)
    l0 = jnp.sum(p0, axis=1, keepdims=True)
    acc0 = _dot(p0.astype(BF16), v_ref[pl.ds(own, bs), :])

    def body(n, carry):
        m, l, acc = carry
        start = pl.multiple_of(n * bs, bs)
        bias = jnp.sum(jnp.where(lane == n, selb, 0.0), axis=1, keepdims=True)
        s = _dot_nt(q, k_ref[pl.ds(start, bs), :]) + bias
        m_new = jnp.maximum(m, jnp.max(s, axis=1, keepdims=True))
        alpha = jnp.exp(m - m_new)
        p = jnp.exp(s - m_new)
        l = alpha * l + jnp.sum(p, axis=1, keepdims=True)
        acc = alpha * acc + _dot(p.astype(BF16), v_ref[pl.ds(start, bs), :])
        return m_new, l, acc

    m, l, acc = lax.fori_loop(0, j, body, (m0, l0, acc0))
    out_ref[...] = (acc * (1.0 / l) * sz_ref[...].astype(F32)).astype(BF16)


def _moba_call(qa, ka, va, selb, sza):
    bsz, s, w = qa.shape
    hd = w // MOBA_HEADS
    tq = TOKEN_TILE
    qspec = pl.BlockSpec((None, tq, hd), lambda b, h, j: (b, j, h))
    kvspec = pl.BlockSpec((None, s, hd), lambda b, h, j: (b, 0, h))
    return pl.pallas_call(
        _moba_kernel,
        grid=(bsz, MOBA_HEADS, s // tq),
        in_specs=[qspec, kvspec, kvspec,
                  pl.BlockSpec((None, tq, V7X_LANES), lambda b, h, j: (b, j, h)), qspec],
        out_specs=qspec,
        out_shape=jax.ShapeDtypeStruct((bsz, s, w), BF16),
        compiler_params=pltpu.CompilerParams(dimension_semantics=("arbitrary", "arbitrary", "arbitrary"),
                                             vmem_limit_bytes=V7X_VMEM_LIMIT_BYTES),
        name="moba",
    )(qa, ka, va, selb, sza)


def _memattn_kernel(q_ref, mem_ref, wkv_ref, sz_ref, out_ref, kv_scr):
    j = pl.program_id(1)
    width = q_ref.shape[1]
    hd = width // MEM_HEADS

    @pl.when(j == 0)
    def _():
        kv_scr[...] = _dot(mem_ref[...].astype(BF16), wkv_ref[...]).astype(BF16)

    for h in range(MEM_HEADS):
        hs = slice(h * hd, (h + 1) * hd)
        s = _dot_nt(q_ref[:, hs], kv_scr[:, hs])
        p = jnp.exp(s - jnp.max(s, axis=1, keepdims=True))
        l = jnp.sum(p, axis=1, keepdims=True)
        o = _dot(p.astype(BF16), kv_scr[:, width + h * hd:width + (h + 1) * hd])
        out_ref[:, hs] = (o * (1.0 / l) * sz_ref[:, hs].astype(F32)).astype(BF16)


def _memattn_call(qc, mem, wkv, szc):
    bsz, s, w = qc.shape
    _, m_tok, d = mem.shape
    tq = TOKEN_TILE
    tok = pl.BlockSpec((None, tq, w), lambda b, j: (b, j, 0))
    return pl.pallas_call(
        _memattn_kernel,
        grid=(bsz, s // tq),
        in_specs=[tok, pl.BlockSpec((None, m_tok, d), lambda b, j: (b, 0, 0)),
                  pl.BlockSpec(wkv.shape, lambda b, j: (0, 0)), tok],
        out_specs=tok,
        out_shape=jax.ShapeDtypeStruct((bsz, s, w), BF16),
        scratch_shapes=[pltpu.VMEM((m_tok, 2 * w), BF16)],
        compiler_params=pltpu.CompilerParams(dimension_semantics=("arbitrary", "arbitrary"),
                                             vmem_limit_bytes=V7X_VMEM_LIMIT_BYTES),
        name="memattn",
    )(qc, mem, wkv, szc)


def _outproj_kernel(x_ref, om_ref, oa_ref, oc_ref, wo_ref, lng_ref, lnb_ref, y_ref):
    w_m = om_ref.shape[1]
    w_a = oa_ref.shape[1]
    mixed = (_dot(om_ref[...], wo_ref[0:w_m, :])
             + _dot(oa_ref[...], wo_ref[w_m:w_m + w_a, :])
             + _dot(oc_ref[...], wo_ref[w_m + w_a:, :]))
    y = DEEPNORM_ALPHA * x_ref[...] + mixed
    mu = jnp.mean(y, axis=1, keepdims=True)
    yc = y - mu
    var = jnp.mean(yc * yc, axis=1, keepdims=True)
    y_ref[...] = yc * lax.rsqrt(var + LN_EPS) * lng_ref[...] + lnb_ref[...]


def _outproj_call(x, om, oa, oc, wo, lng, lnb):
    bsz, s, d = x.shape
    tm = 2 * TOKEN_TILE
    tok = lambda w: pl.BlockSpec((None, tm, w), lambda b, t: (b, t, 0))
    res = lambda a: pl.BlockSpec(a.shape, lambda b, t: (0, 0))
    return pl.pallas_call(
        _outproj_kernel,
        grid=(bsz, s // tm),
        in_specs=[tok(d), tok(om.shape[2]), tok(oa.shape[2]), tok(oc.shape[2]), res(wo), res(lng), res(lnb)],
        out_specs=tok(d),
        out_shape=jax.ShapeDtypeStruct((bsz, s, d), x.dtype),
        compiler_params=pltpu.CompilerParams(dimension_semantics=("arbitrary", "arbitrary"),
                                             vmem_limit_bytes=V7X_VMEM_LIMIT_BYTES),
        name="outproj",
    )(x, om, oa, oc, wo, lng, lnb)


def _diag_tiles(w, heads):
    groups, blk, _ = w.shape
    gph = groups // heads
    eye = jnp.eye(gph, dtype=w.dtype)
    tiles = jnp.einsum('hgio,gk->hgiko', w.reshape(heads, gph, blk, blk), eye)
    return tiles.reshape(heads, gph * blk, gph * blk)


def kernel(x, mem, positions, w_in, mlstm_conv_w, mlstm_conv_b, mlstm_wq, mlstm_wk, mlstm_wv, mlstm_w_gates, mlstm_b_gates, mlstm_norm_g, mlstm_skip, w_mem_kv, w_out, ln_g, ln_b):
    bsz, s, d = x.shape
    ml_width = mlstm_conv_w.shape[1]
    ml_heads = mlstm_b_gates.shape[0] // 2
    mem_width = w_mem_kv.shape[1] // 2
    moba_width = (w_in.shape[1] - 2 * ml_width - 2 * mem_width) // 4
    moba_hd = moba_width // MOBA_HEADS
    assert moba_hd == V7X_LANES and mem_width // MEM_HEADS == V7X_LANES
    assert s % (2 * TOKEN_TILE) == 0

    row = lambda a: a.reshape(1, -1).astype(F32)
    wq = _diag_tiles(mlstm_wq, ml_heads).astype(BF16)
    wk = _diag_tiles(mlstm_wk, ml_heads)
    wkt = jnp.swapaxes(wk, 1, 2).astype(BF16)
    wk = wk.astype(BF16)
    wv = _diag_tiles(mlstm_wv, ml_heads).astype(BF16)
    n_gates = mlstm_w_gates.shape[1]
    wg = jnp.pad(mlstm_w_gates, ((0, 0), (0, V7X_LANES - n_gates))).astype(BF16)
    bg = jnp.pad(mlstm_b_gates, (0, V7X_LANES - n_gates)).reshape(1, -1).astype(F32)
    half = moba_hd // 2
    inv_freq = ROPE_THETA ** (-jnp.arange(half, dtype=F32) * 2.0 / moba_hd)
    invf = jnp.concatenate([inv_freq, inv_freq]).reshape(1, -1)
    sgn = jnp.concatenate([-jnp.ones((half,), F32), jnp.ones((half,), F32)]).reshape(1, -1)
    pos = positions.astype(F32).reshape(bsz, s, 1)

    (qm, kmt, vm, add, mul, gcol, grow, qa, ka, va, sza, selb, qc, szc) = _proj_call(
        x, pos, w_in.astype(BF16), mlstm_conv_w.astype(F32), row(mlstm_conv_b), wq, wk, wkt, wv, wg, bg,
        row(mlstm_norm_g), row(mlstm_skip), invf, sgn,
        ml_width=ml_width, ml_heads=ml_heads, moba_width=moba_width, mem_width=mem_width)
    out_m = _mlstm_call(qm, kmt, vm, gcol, grow, add, mul, heads=ml_heads)
    out_a = _moba_call(qa, ka, va, selb, sza)
    out_c = _memattn_call(qc, mem, w_mem_kv.astype(BF16), szc)
    return _outproj_call(x, out_m, out_a, out_c, w_out.astype(BF16), row(ln_g), row(ln_b))
```

```python
import functools
import math

import jax
import jax.numpy as jnp
from jax import lax
from jax.experimental import pallas as pl
from jax.experimental.pallas import tpu as pltpu

MOBA_HEADS = 4
MOBA_BLOCK = 256
MOBA_TOP_K = 3
MEM_HEADS = 4
ROPE_THETA = 10000.0
DEPTH = 1
DEEPNORM_ALPHA = (2 * DEPTH) ** 0.25
LN_EPS = 1e-5

V7X_LANES = 128
V7X_SUBLANES = 8
V7X_VMEM_LIMIT_BYTES = 56 * 1024 * 1024

TOKEN_TILE = MOBA_BLOCK
MASK_VALUE = -1e30

F32 = jnp.float32
BF16 = jnp.bfloat16


def _dot(a, b):
    return jnp.dot(a, b, preferred_element_type=F32)


def _dot_nt(a, b):
    return lax.dot_general(a, b, (((1,), (1,)), ((), ())), preferred_element_type=F32)


def _split3(v):
    hi = v.astype(BF16)
    r1 = v - hi.astype(F32)
    mid = r1.astype(BF16)
    lo = (r1 - mid.astype(F32)).astype(BF16)
    return hi, mid, lo


def _silu(v):
    return v * jax.nn.sigmoid(v)


def _log_sigmoid(v):
    return jnp.minimum(v, 0.0) - jnp.log1p(jnp.exp(-jnp.abs(v)))


def _proj_kernel(x_ref, pos_ref, win_ref, convw_ref, convb_ref, wq_ref, wk_ref, wkt_ref, wv_ref,
                 wg_ref, bg_ref, normg_ref, skip_ref, invf_ref, sgn_ref,
                 qm_ref, kmt_ref, vm_ref, add_ref, mul_ref, gcol_ref, grow_ref,
                 qa_ref, ka_ref, va_ref, sza_ref, qc_ref, szc_ref,
                 xpad_scr, kmean_scr,
                 *, ml_width, ml_heads, moba_width, mem_width, n_sel, gate_slots):
    t = pl.program_id(1)
    tm = x_ref.shape[0]
    hd = ml_width // ml_heads
    moba_hd = moba_width // MOBA_HEADS
    mem_hd = mem_width // MEM_HEADS
    c_zm = ml_width
    c_qa = 2 * ml_width
    c_ka = c_qa + moba_width
    c_va = c_ka + moba_width
    c_za = c_va + moba_width
    c_qc = c_za + moba_width
    c_zc = c_qc + mem_width

    xb = x_ref[...].astype(BF16)

    xm = _dot(xb, win_ref[:, 0:ml_width])
    pad = xpad_scr.shape[0] - tm

    @pl.when(t == 0)
    def _():
        xpad_scr[0:pad, :] = jnp.zeros((pad, ml_width), F32)

    @pl.when(t > 0)
    def _():
        xpad_scr[0:pad, :] = xpad_scr[tm:tm + pad, :]

    xpad_scr[pad:pad + tm, :] = xm
    k_w = convw_ref.shape[0]
    conv = convb_ref[...] + xm * convw_ref[k_w - 1:k_w, :]
    for j in range(k_w - 1):
        conv = conv + xpad_scr[pl.ds(pad - (k_w - 1) + j, tm), :] * convw_ref[j:j + 1, :]
    xc = _silu(conv)

    sz = _silu(_dot(xb, win_ref[:, c_zm:c_zm + ml_width]))
    add_ref[...] = (skip_ref[...] * xc * sz).astype(BF16)
    mul_ref[...] = (normg_ref[...] * sz).astype(BF16)

    g = jnp.zeros((tm, V7X_LANES), F32) + bg_ref[...]
    for h in range(ml_heads):
        hs = slice(h * hd, (h + 1) * hd)
        xc_h = xc[:, hs].astype(BF16)
        xm_h = xm[:, hs].astype(BF16)
        q_h = _dot(xc_h, wq_ref[h]).astype(BF16)
        k_h = _dot(xc_h, wk_ref[h]).astype(BF16)
        v_h = _dot(xm_h, wv_ref[h]).astype(BF16)
        qm_ref[:, hs] = q_h
        vm_ref[:, hs] = v_h
        kmt_ref[hs, :] = (_dot_nt(wkt_ref[h], xc_h) * (hd ** -0.5)).astype(BF16)
        g = g + _dot(q_h, wg_ref[hs, :])
        g = g + _dot(k_h, wg_ref[ml_width + h * hd:ml_width + (h + 1) * hd, :])
        g = g + _dot(v_h, wg_ref[2 * ml_width + h * hd:2 * ml_width + (h + 1) * hd, :])

    lane = lax.broadcasted_iota(jnp.int32, (tm, V7X_LANES), 1)
    logf = jnp.where(lane >= ml_heads, _log_sigmoid(g), 0.0)
    row_i = lax.broadcasted_iota(jnp.int32, (tm, tm), 0)
    col_i = lax.broadcasted_iota(jnp.int32, (tm, tm), 1)
    tril = (row_i >= col_i).astype(BF16)
    f_hi, f_mid, f_lo = _split3(logf)
    bcum = _dot(tril, f_hi) + _dot(tril, f_mid) + _dot(tril, f_lo)
    gcol = jnp.where(lane < ml_heads, g, bcum)
    gcol_ref[...] = gcol
    grow_ref[...] = jnp.transpose(gcol)[0:grow_ref.shape[0], :]

    ang = pos_ref[...] * invf_ref[...]
    cosf = jnp.cos(ang)
    sins = jnp.sin(ang) * sgn_ref[...]
    qa = _dot(xb, win_ref[:, c_qa:c_qa + moba_width])
    ka = _dot(xb, win_ref[:, c_ka:c_ka + moba_width])

    @pl.when(t == 0)
    def _():
        kmean_scr[...] = jnp.zeros(kmean_scr.shape, F32)

    nb_lanes = gate_slots
    gate = jnp.zeros((tm, V7X_LANES), F32)
    kmean_rows = []
    for h in range(MOBA_HEADS):
        hs = slice(h * moba_hd, (h + 1) * moba_hd)
        q_h = qa[:, hs]
        k_h = ka[:, hs]
        q_rot = q_h * cosf + pltpu.roll(q_h, moba_hd // 2, 1) * sins
        k_rot = k_h * cosf + pltpu.roll(k_h, moba_hd // 2, 1) * sins
        qa_ref[:, 2 * h * moba_hd:(2 * h + 1) * moba_hd] = (q_rot * (moba_hd ** -0.5)).astype(BF16)
        ka_ref[:, 2 * h * moba_hd:(2 * h + 1) * moba_hd] = k_rot.astype(BF16)
        kmean_rows.append(jnp.mean(k_rot, axis=0, keepdims=True))
        km = kmean_scr[:, hs]
        q_hi = q_rot.astype(BF16)
        q_lo = (q_rot - q_hi.astype(F32)).astype(BF16)
        k_hi = km.astype(BF16)
        k_lo = (km - k_hi.astype(F32)).astype(BF16)
        gate = gate + _dot_nt(q_hi, k_hi) + _dot_nt(q_lo, k_hi) + _dot_nt(q_hi, k_lo)
    km_row = lax.broadcasted_iota(jnp.int32, kmean_scr.shape, 0)
    km_head = lax.broadcasted_iota(jnp.int32, kmean_scr.shape, 1) // moba_hd
    kmean_scr[...] = jnp.where(km_row == km_head * nb_lanes + t,
                               jnp.concatenate(kmean_rows, axis=1), kmean_scr[...])

    n_l = lane & (nb_lanes - 1)
    valid = (n_l < t) & (lane < MOBA_HEADS * nb_lanes)
    gate = jnp.where(valid, gate, -jnp.inf)
    cnt = jnp.zeros((tm, V7X_LANES), jnp.int32)
    for r in range(1, nb_lanes):
        up = pltpu.roll(gate, r, 1)
        cnt = cnt + ((n_l >= r) & (up >= gate)).astype(jnp.int32)
        dn = pltpu.roll(gate, V7X_LANES - r, 1)
        cnt = cnt + ((n_l < nb_lanes - r) & (dn > gate)).astype(jnp.int32)
    keep = (valid & (cnt < n_sel)) | ((n_l == t) & (lane < MOBA_HEADS * nb_lanes))
    selb = jnp.where(keep, 0.0, MASK_VALUE)
    va = _dot(xb, win_ref[:, c_va:c_va + moba_width]).astype(BF16)
    k_tag = (lane == t).astype(BF16)
    v_tag = (lane == 0).astype(BF16)
    for h in range(MOBA_HEADS):
        shift = (V7X_LANES - h * nb_lanes) % V7X_LANES
        sel_h = selb if shift == 0 else pltpu.roll(selb, shift, 1)
        aug = slice((2 * h + 1) * moba_hd, (2 * h + 2) * moba_hd)
        qa_ref[:, aug] = sel_h.astype(BF16)
        ka_ref[:, aug] = k_tag
        va_ref[:, 2 * h * moba_hd:(2 * h + 1) * moba_hd] = va[:, h * moba_hd:(h + 1) * moba_hd]
        va_ref[:, aug] = v_tag
    sza_ref[...] = _silu(_dot(xb, win_ref[:, c_za:c_za + moba_width])).astype(BF16)

    qc_ref[...] = (_dot(xb, win_ref[:, c_qc:c_qc + mem_width]) * (mem_hd ** -0.5)).astype(BF16)
    szc_ref[...] = _silu(_dot(xb, win_ref[:, c_zc:c_zc + mem_width])).astype(BF16)


def _resident(shape):
    nd = len(shape)
    return pl.BlockSpec(shape, lambda b, t, _nd=nd: (0,) * _nd)


def _proj_call(x, pos, win, convw, convb, wq, wk, wkt, wv, wg, bg, normg, skip, invf, sgn,
               *, ml_width, ml_heads, moba_width, mem_width):
    bsz, s, d = x.shape
    tm = TOKEN_TILE
    nt = s // tm
    n_sel = min(MOBA_TOP_K, nt - 1)
    gate_slots = max(V7X_SUBLANES, pl.next_power_of_2(nt))
    assert MOBA_HEADS * gate_slots <= V7X_LANES and s % tm == 0
    tok = lambda w: pl.BlockSpec((None, tm, w), lambda b, t: (b, t, 0))
    out_shapes = (
        jax.ShapeDtypeStruct((bsz, s, ml_width), BF16),
        jax.ShapeDtypeStruct((bsz, ml_width, s), BF16),
        jax.ShapeDtypeStruct((bsz, s, ml_width), BF16),
        jax.ShapeDtypeStruct((bsz, s, ml_width), BF16),
        jax.ShapeDtypeStruct((bsz, s, ml_width), BF16),
        jax.ShapeDtypeStruct((bsz, s, V7X_LANES), F32),
        jax.ShapeDtypeStruct((bsz, V7X_SUBLANES, s), F32),
        jax.ShapeDtypeStruct((bsz, s, 2 * moba_width), BF16),
        jax.ShapeDtypeStruct((bsz, s, 2 * moba_width), BF16),
        jax.ShapeDtypeStruct((bsz, s, 2 * moba_width), BF16),
        jax.ShapeDtypeStruct((bsz, s, moba_width), BF16),
        jax.ShapeDtypeStruct((bsz, s, mem_width), BF16),
        jax.ShapeDtypeStruct((bsz, s, mem_width), BF16),
    )
    out_specs = (
        tok(ml_width),
        pl.BlockSpec((None, ml_width, tm), lambda b, t: (b, 0, t)),
        tok(ml_width), tok(ml_width), tok(ml_width),
        tok(V7X_LANES),
        pl.BlockSpec((None, V7X_SUBLANES, tm), lambda b, t: (b, 0, t)),
        tok(2 * moba_width), tok(2 * moba_width), tok(2 * moba_width), tok(moba_width),
        tok(mem_width), tok(mem_width),
    )
    in_specs = [tok(d), tok(1)] + [_resident(a.shape) for a in
                                   (win, convw, convb, wq, wk, wkt, wv, wg, bg, normg, skip, invf, sgn)]
    kern = functools.partial(_proj_kernel, ml_width=ml_width, ml_heads=ml_heads,
                             moba_width=moba_width, mem_width=mem_width, n_sel=n_sel,
                             gate_slots=gate_slots)
    return pl.pallas_call(
        kern,
        grid=(bsz, nt),
        in_specs=in_specs,
        out_specs=out_specs,
        out_shape=out_shapes,
        scratch_shapes=[pltpu.VMEM((tm + V7X_SUBLANES, ml_width), F32),
                        pltpu.VMEM((V7X_LANES, moba_width), F32)],
        compiler_params=pltpu.CompilerParams(dimension_semantics=("arbitrary", "arbitrary"),
                                             vmem_limit_bytes=V7X_VMEM_LIMIT_BYTES),
        name="proj",
    )(x, pos, win, convw, convb, wq, wk, wkt, wv, wg, bg, normg, skip, invf, sgn)


def _mlstm_kernel(qm_ref, kmt_ref, vm_ref, gcol_ref, grow_ref, add_ref, mul_ref, out_ref,
                  c_scr, n_scr, m_scr, *, heads):
    c = pl.program_id(1)
    L = qm_ref.shape[0]
    hd = qm_ref.shape[1] // heads

    @pl.when(c == 0)
    def _():
        c_scr[...] = jnp.zeros(c_scr.shape, F32)
        n_scr[...] = jnp.zeros(n_scr.shape, F32)
        m_scr[...] = jnp.zeros(m_scr.shape, F32)

    row_i = lax.broadcasted_iota(jnp.int32, (L, L), 0)
    col_i = lax.broadcasted_iota(jnp.int32, (L, L), 1)
    causal = row_i >= col_i
    gcol = gcol_ref[...]
    grow = grow_ref[...]
    for h in range(heads):
        hs = slice(h * hd, (h + 1) * hd)
        q = qm_ref[:, hs]
        kt = kmt_ref[hs, :]
        v = vm_ref[:, hs]
        i_row = grow[h:h + 1, :]
        b_row = grow[heads + h:heads + h + 1, :]
        b_col = gcol[:, heads + h:heads + h + 1]
        m_prev = m_scr[h][0:1, 0:1]
        c_prev = c_scr[h]
        n_prev = n_scr[h][0:1, :]

        log_d = jnp.where(causal, b_col - b_row + i_row, -jnp.inf)
        m_inter = b_col + m_prev
        m_row = jnp.maximum(m_inter, jnp.max(log_d, axis=1, keepdims=True))
        w_inter = jnp.exp(m_inter - m_row)
        s_qk = _dot(q, kt) * jnp.exp(log_d - m_row)
        num = w_inter * _dot(q, c_prev.astype(BF16)) + _dot(s_qk.astype(BF16), v)
        den = (w_inter * jnp.sum(q.astype(F32) * n_prev, axis=1, keepdims=True)
               + jnp.sum(s_qk, axis=1, keepdims=True))
        hh = num * (1.0 / jnp.maximum(jnp.abs(den), jnp.exp(-m_row)))
        mu = jnp.mean(hh, axis=1, keepdims=True)
        hc = hh - mu
        var = jnp.mean(hc * hc, axis=1, keepdims=True)
        hn = hc * lax.rsqrt(var + LN_EPS)
        out_ref[:, hs] = (hn * mul_ref[:, hs].astype(F32) + add_ref[:, hs].astype(F32)).astype(BF16)

        b_end = b_row[:, L - 1:L]
        log_w = b_end - b_row + i_row
        m_new = jnp.maximum(b_end + m_prev, jnp.max(log_w, axis=1, keepdims=True))
        decay = jnp.exp(b_end + m_prev - m_new)
        w_in = jnp.exp(log_w - m_new)
        kw = (kt.astype(F32) * w_in).astype(BF16)
        c_scr[h] = decay * c_prev + _dot(kw, v)
        w8 = jnp.broadcast_to(w_in, (V7X_SUBLANES, L)).astype(BF16)
        n_scr[h] = decay * n_scr[h] + _dot_nt(w8, kt)
        m_scr[h] = jnp.broadcast_to(m_new, m_scr.shape[1:])


def _mlstm_call(qm, kmt, vm, gcol, grow, add, mul, *, heads):
    bsz, s, w = qm.shape
    L = TOKEN_TILE
    hd = w // heads
    tok = lambda width: pl.BlockSpec((None, L, width), lambda b, c: (b, c, 0))
    tmaj = lambda rows: pl.BlockSpec((None, rows, L), lambda b, c: (b, 0, c))
    return pl.pallas_call(
        functools.partial(_mlstm_kernel, heads=heads),
        grid=(bsz, s // L),
        in_specs=[tok(w), tmaj(w), tok(w), tok(V7X_LANES), tmaj(V7X_SUBLANES), tok(w), tok(w)],
        out_specs=tok(w),
        out_shape=jax.ShapeDtypeStruct((bsz, s, w), BF16),
        scratch_shapes=[pltpu.VMEM((heads, hd, hd), F32),
                        pltpu.VMEM((heads, V7X_SUBLANES, hd), F32),
                        pltpu.VMEM((heads, V7X_SUBLANES, V7X_LANES), F32)],
        compiler_params=pltpu.CompilerParams(dimension_semantics=("arbitrary", "arbitrary"),
                                             vmem_limit_bytes=V7X_VMEM_LIMIT_BYTES),
        name="mlstm",
    )(qm, kmt, vm, gcol, grow, add, mul)


def _moba_kernel(q_ref, k_ref, v_ref, sz_ref, out_ref, s_scr, m_scr, acc_scr):
    j = pl.program_id(1)
    tq = q_ref.shape[0]
    bs = MOBA_BLOCK
    aw = q_ref.shape[1] // MOBA_HEADS
    hd = aw // 2
    half = bs // 2
    causal = (lax.broadcasted_iota(jnp.int32, (tq, bs), 0) >= lax.broadcasted_iota(jnp.int32, (tq, bs), 1))

    own = pl.multiple_of(j * bs, bs)
    for h in range(MOBA_HEADS):
        hs = slice(h * aw, (h + 1) * aw)
        s = jnp.where(causal, _dot_nt(q_ref[:, hs], k_ref[pl.ds(own, bs), hs]), MASK_VALUE)
        s_scr[h, j] = s
        m_scr[h] = jnp.maximum(s[:, :half], s[:, half:])

    def scores(n, carry):
        start = pl.multiple_of(n * bs, bs)
        for h in range(MOBA_HEADS):
            hs = slice(h * aw, (h + 1) * aw)
            s = _dot_nt(q_ref[:, hs], k_ref[pl.ds(start, bs), hs])
            s_scr[h, n] = s
            m_scr[h] = jnp.maximum(m_scr[h], jnp.maximum(s[:, :half], s[:, half:]))
        return carry

    lax.fori_loop(0, j, scores, 0)

    for h in range(MOBA_HEADS):
        m_scr[h] = jnp.broadcast_to(jnp.max(m_scr[h], axis=1, keepdims=True), m_scr.shape[1:])
        acc_scr[h] = jnp.zeros(acc_scr.shape[1:], F32)

    def weighted(n, carry):
        start = pl.multiple_of(n * bs, bs)
        for h in range(MOBA_HEADS):
            hs = slice(h * aw, (h + 1) * aw)
            m = m_scr[h]
            s = s_scr[h, n]
            p = jnp.concatenate([jnp.exp(s[:, :half] - m), jnp.exp(s[:, half:] - m)], axis=1)
            acc_scr[h] += _dot(p.astype(BF16), v_ref[pl.ds(start, bs), hs])
        return carry

    lax.fori_loop(0, j + 1, weighted, 0)

    for h in range(MOBA_HEADS):
        acc = acc_scr[h]
        o = acc[:, :hd] * (1.0 / acc[:, hd:hd + 1])
        out_ref[:, h * hd:(h + 1) * hd] = (o * sz_ref[:, h * hd:(h + 1) * hd].astype(F32)).astype(BF16)


def _moba_call(qa, ka, va, sza):
    bsz, s, aw_all = qa.shape
    w = sza.shape[2]
    tq = TOKEN_TILE
    assert tq == MOBA_BLOCK and aw_all // MOBA_HEADS == MOBA_BLOCK
    nblk = s // MOBA_BLOCK
    tok = lambda width: pl.BlockSpec((None, tq, width), lambda b, j: (b, j, 0))
    kvspec = pl.BlockSpec((None, s, aw_all), lambda b, j: (b, 0, 0))
    return pl.pallas_call(
        _moba_kernel,
        grid=(bsz, s // tq),
        in_specs=[tok(aw_all), kvspec, kvspec, tok(w)],
        out_specs=tok(w),
        out_shape=jax.ShapeDtypeStruct((bsz, s, w), BF16),
        scratch_shapes=[pltpu.VMEM((MOBA_HEADS, nblk, tq, MOBA_BLOCK), F32),
                        pltpu.VMEM((MOBA_HEADS, tq, MOBA_BLOCK // 2), F32),
                        pltpu.VMEM((MOBA_HEADS, tq, aw_all // MOBA_HEADS), F32)],
        compiler_params=pltpu.CompilerParams(dimension_semantics=("arbitrary", "arbitrary"),
                                             vmem_limit_bytes=V7X_VMEM_LIMIT_BYTES),
        name="moba",
    )(qa, ka, va, sza)


def _memattn_kernel(q_ref, mem_ref, wkv_ref, sz_ref, out_ref, kv_scr):
    j = pl.program_id(1)
    width = q_ref.shape[1]
    hd = width // MEM_HEADS

    @pl.when(j == 0)
    def _():
        kv_scr[...] = _dot(mem_ref[...].astype(BF16), wkv_ref[...]).astype(BF16)

    for h in range(MEM_HEADS):
        hs = slice(h * hd, (h + 1) * hd)
        s = _dot_nt(q_ref[:, hs], kv_scr[:, hs])
        p = jnp.exp(s - jnp.max(s, axis=1, keepdims=True))
        l = jnp.sum(p, axis=1, keepdims=True)
        o = _dot(p.astype(BF16), kv_scr[:, width + h * hd:width + (h + 1) * hd])
        out_ref[:, hs] = (o * (1.0 / l) * sz_ref[:, hs].astype(F32)).astype(BF16)


def _memattn_call(qc, mem, wkv, szc):
    bsz, s, w = qc.shape
    _, m_tok, d = mem.shape
    tq = TOKEN_TILE
    tok = pl.BlockSpec((None, tq, w), lambda b, j: (b, j, 0))
    return pl.pallas_call(
        _memattn_kernel,
        grid=(bsz, s // tq),
        in_specs=[tok, pl.BlockSpec((None, m_tok, d), lambda b, j: (b, 0, 0)),
                  pl.BlockSpec(wkv.shape, lambda b, j: (0, 0)), tok],
        out_specs=tok,
        out_shape=jax.ShapeDtypeStruct((bsz, s, w), BF16),
        scratch_shapes=[pltpu.VMEM((m_tok, 2 * w), BF16)],
        compiler_params=pltpu.CompilerParams(dimension_semantics=("arbitrary", "arbitrary"),
                                             vmem_limit_bytes=V7X_VMEM_LIMIT_BYTES),
        name="memattn",
    )(qc, mem, wkv, szc)


def _outproj_kernel(x_ref, om_ref, oa_ref, oc_ref, wo_ref, lng_ref, lnb_ref, y_ref):
    w_m = om_ref.shape[1]
    w_a = oa_ref.shape[1]
    mixed = (_dot(om_ref[...], wo_ref[0:w_m, :])
             + _dot(oa_ref[...], wo_ref[w_m:w_m + w_a, :])
             + _dot(oc_ref[...], wo_ref[w_m + w_a:, :]))
    y = DEEPNORM_ALPHA * x_ref[...] + mixed
    mu = jnp.mean(y, axis=1, keepdims=True)
    yc = y - mu
    var = jnp.mean(yc * yc, axis=1, keepdims=True)
    y_ref[...] = yc * lax.rsqrt(var + LN_EPS) * lng_ref[...] + lnb_ref[...]


def _outproj_call(x, om, oa, oc, wo, lng, lnb):
    bsz, s, d = x.shape
    tm = 2 * TOKEN_TILE
    tok = lambda w: pl.BlockSpec((None, tm, w), lambda b, t: (b, t, 0))
    res = lambda a: pl.BlockSpec(a.shape, lambda b, t: (0, 0))
    return pl.pallas_call(
        _outproj_kernel,
        grid=(bsz, s // tm),
        in_specs=[tok(d), tok(om.shape[2]), tok(oa.shape[2]), tok(oc.shape[2]), res(wo), res(lng), res(lnb)],
        out_specs=tok(d),
        out_shape=jax.ShapeDtypeStruct((bsz, s, d), x.dtype),
        compiler_params=pltpu.CompilerParams(dimension_semantics=("arbitrary", "arbitrary"),
                                             vmem_limit_bytes=V7X_VMEM_LIMIT_BYTES),
        name="outproj",
    )(x, om, oa, oc, wo, lng, lnb)


def _diag_tiles(w, heads):
    groups, blk, _ = w.shape
    hd = groups // heads * blk
    rows = w.reshape(heads, hd, blk)
    idx = jnp.arange(hd)
    same_group = (idx[:, None] // blk) == (idx[None, :] // blk)
    return jnp.where(same_group, jnp.tile(rows, (1, 1, hd // blk)), 0.0)


def kernel(x, mem, positions, w_in, mlstm_conv_w, mlstm_conv_b, mlstm_wq, mlstm_wk, mlstm_wv, mlstm_w_gates, mlstm_b_gates, mlstm_norm_g, mlstm_skip, w_mem_kv, w_out, ln_g, ln_b):
    bsz, s, d = x.shape
    ml_width = mlstm_conv_w.shape[1]
    ml_heads = mlstm_b_gates.shape[0] // 2
    mem_width = w_mem_kv.shape[1] // 2
    moba_width = (w_in.shape[1] - 2 * ml_width - 2 * mem_width) // 4
    moba_hd = moba_width // MOBA_HEADS
    assert moba_hd == V7X_LANES and mem_width // MEM_HEADS == V7X_LANES
    assert s % (2 * TOKEN_TILE) == 0

    row = lambda a: a.reshape(1, -1).astype(F32)
    wq = _diag_tiles(mlstm_wq, ml_heads).astype(BF16)
    wk = _diag_tiles(mlstm_wk, ml_heads)
    wkt = jnp.swapaxes(wk, 1, 2).astype(BF16)
    wk = wk.astype(BF16)
    wv = _diag_tiles(mlstm_wv, ml_heads).astype(BF16)
    n_gates = mlstm_w_gates.shape[1]
    wg = jnp.pad(mlstm_w_gates, ((0, 0), (0, V7X_LANES - n_gates))).astype(BF16)
    bg = jnp.pad(mlstm_b_gates, (0, V7X_LANES - n_gates)).reshape(1, -1).astype(F32)
    half = moba_hd // 2
    inv_freq = ROPE_THETA ** (-jnp.arange(half, dtype=F32) * 2.0 / moba_hd)
    invf = jnp.concatenate([inv_freq, inv_freq]).reshape(1, -1)
    sgn = jnp.concatenate([-jnp.ones((half,), F32), jnp.ones((half,), F32)]).reshape(1, -1)
    pos = positions.astype(F32).reshape(bsz, s, 1)

    (qm, kmt, vm, add, mul, gcol, grow, qa, ka, va, sza, qc, szc) = _proj_call(
        x, pos, w_in.astype(BF16), mlstm_conv_w.astype(F32), row(mlstm_conv_b), wq, wk, wkt, wv, wg, bg,
        row(mlstm_norm_g), row(mlstm_skip), invf, sgn,
        ml_width=ml_width, ml_heads=ml_heads, moba_width=moba_width, mem_width=mem_width)
    out_m = _mlstm_call(qm, kmt, vm, gcol, grow, add, mul, heads=ml_heads)
    out_a = _moba_call(qa, ka, va, sza)
    out_c = _memattn_call(qc, mem, w_mem_kv.astype(BF16), szc)
    return _outproj_call(x, out_m, out_a, out_c, w_out.astype(BF16), row(ln_g), row(ln_b))
```

```python
import functools
import math

import jax
import jax.numpy as jnp
from jax import lax
from jax.experimental import pallas as pl
from jax.experimental.pallas import tpu as pltpu

MOBA_HEADS = 4
MOBA_BLOCK = 256
MOBA_TOP_K = 3
MEM_HEADS = 4
ROPE_THETA = 10000.0
DEPTH = 1
DEEPNORM_ALPHA = (2 * DEPTH) ** 0.25
LN_EPS = 1e-5

V7X_LANES = 128
V7X_SUBLANES = 8
V7X_VMEM_LIMIT_BYTES = 56 * 1024 * 1024

TOKEN_TILE = MOBA_BLOCK
MASK_VALUE = -1e30
GATE_ROWS = 2 * V7X_SUBLANES

F32 = jnp.float32
BF16 = jnp.bfloat16


def _dot(a, b):
    return jnp.dot(a, b, preferred_element_type=F32)


def _dot_nt(a, b):
    return lax.dot_general(a, b, (((1,), (1,)), ((), ())), preferred_element_type=F32)


def _split3(v):
    hi = v.astype(BF16)
    r1 = v - hi.astype(F32)
    mid = r1.astype(BF16)
    lo = (r1 - mid.astype(F32)).astype(BF16)
    return hi, mid, lo


def _lane_scan(v, op, identity, lane_idx):
    shift = 1
    while shift < v.shape[-1]:
        v = op(v, jnp.where(lane_idx >= shift, pltpu.roll(v, shift, v.ndim - 1), identity))
        shift *= 2
    return v


def _silu(v):
    return v * jax.nn.sigmoid(v)


def _log_sigmoid(v):
    return jnp.minimum(v, 0.0) - jnp.log1p(jnp.exp(-jnp.abs(v)))


def _proj_kernel(x_ref, pos_ref, win_ref, convw_ref, convb_ref, wq_ref, wk_ref, wv_ref,
                 wg_ref, bg_ref, normg_ref, skip_ref, invf_ref, sgn_ref,
                 qm_ref, kmt_ref, vm_ref, add_ref, mul_ref, gcol_ref, grow_ref,
                 qa_ref, ka_ref, va_ref, sza_ref, qc_ref, szc_ref,
                 xpad_scr, kmean_scr,
                 *, ml_width, ml_heads, moba_width, mem_width, n_sel, gate_slots):
    t = pl.program_id(1)
    tm = x_ref.shape[0]
    hd = ml_width // ml_heads
    moba_hd = moba_width // MOBA_HEADS
    mem_hd = mem_width // MEM_HEADS
    c_zm = ml_width
    c_qa = 2 * ml_width
    c_ka = c_qa + moba_width
    c_va = c_ka + moba_width
    c_za = c_va + moba_width
    c_qc = c_za + moba_width
    c_zc = c_qc + mem_width

    xb = x_ref[...].astype(BF16)
    pad = xpad_scr.shape[0] - tm
    k_w = convw_ref.shape[0]
    nb_lanes = gate_slots

    @pl.when(t == 0)
    def _():
        xpad_scr[0:pad, :] = jnp.zeros((pad, ml_width), F32)
        kmean_scr[...] = jnp.zeros(kmean_scr.shape, F32)

    @pl.when(t > 0)
    def _():
        xpad_scr[0:pad, :] = xpad_scr[tm:tm + pad, :]

    def x_cols(c0, width):
        return _dot(xb, win_ref[:, c0:c0 + width])

    def rotary_pair(p, qp, kp):
        rots, means = [], []
        for i in range(2):
            h = 2 * p + i
            ls = slice(i * moba_hd, (i + 1) * moba_hd)
            q_rot = qp[:, ls] * cosf + pltpu.roll(qp[:, ls], moba_hd // 2, 1) * sins
            k_rot = kp[:, ls] * cosf + pltpu.roll(kp[:, ls], moba_hd // 2, 1) * sins
            qa_ref[:, 2 * h * moba_hd:(2 * h + 1) * moba_hd] = (q_rot * (moba_hd ** -0.5)).astype(BF16)
            ka_ref[:, 2 * h * moba_hd:(2 * h + 1) * moba_hd] = k_rot.astype(BF16)
            rots.append(q_rot)
            means.append(jnp.mean(k_rot, axis=0, keepdims=True))
        return rots, means

    def conv_gate_head(h, xm_h, zm_h):
        hs = slice(h * hd, (h + 1) * hd)
        xpad_scr[pad:pad + tm, hs] = xm_h
        conv = convb_ref[:, hs] + xm_h * convw_ref[k_w - 1:k_w, hs]
        for j in range(k_w - 1):
            conv = conv + xpad_scr[pl.ds(pad - (k_w - 1) + j, tm), hs] * convw_ref[j:j + 1, hs]
        xc_h = _silu(conv)
        sz_h = _silu(zm_h)
        add_ref[:, hs] = (skip_ref[:, hs] * xc_h * sz_h).astype(BF16)
        mul_ref[:, hs] = (normg_ref[:, hs] * sz_h).astype(BF16)
        return xc_h.astype(BF16), xm_h.astype(BF16)

    def blockdiag_head(h, xc_b, xm_b):
        hs = slice(h * hd, (h + 1) * hd)
        q_h = _dot(xc_b, wq_ref[h]).astype(BF16)
        k_f = _dot(xc_b, wk_ref[h])
        v_h = _dot(xm_b, wv_ref[h]).astype(BF16)
        qm_ref[:, hs] = q_h
        vm_ref[:, hs] = v_h
        kmt_ref[hs, :] = (jnp.transpose(k_f) * (hd ** -0.5)).astype(BF16)
        return q_h, k_f.astype(BF16), v_h

    pw = 2 * moba_hd
    qa0, ka0 = x_cols(c_qa, pw), x_cols(c_ka, pw)
    ang = pos_ref[...] * invf_ref[...]
    cosf = jnp.cos(ang)
    sins = jnp.sin(ang) * sgn_ref[...]
    qa1, ka1 = x_cols(c_qa + pw, pw), x_cols(c_ka + pw, pw)
    rots0, means0 = rotary_pair(0, qa0, ka0)
    xm0, zm0 = x_cols(0, hd), x_cols(c_zm, hd)
    rots1, means1 = rotary_pair(1, qa1, ka1)
    q_rots, kmean_rows = rots0 + rots1, means0 + means1
    q_all = jnp.concatenate(q_rots, axis=1)
    km = kmean_scr[...]
    q_hi = q_all.astype(BF16)
    q_lo = (q_all - q_hi.astype(F32)).astype(BF16)
    k_hi = km.astype(BF16)
    k_lo = (km - k_hi.astype(F32)).astype(BF16)
    xz = {0: (xm0, zm0), 1: (x_cols(hd, hd), x_cols(c_zm + hd, hd))}
    gate = _dot_nt(q_hi, k_hi) + _dot_nt(q_lo, k_hi) + _dot_nt(q_hi, k_lo)
    pending = [("xz", h) for h in range(2, ml_heads)] + [("va", None), ("za", None)]
    def gate_terms(h, q_h, k_h, v_h):
        hs = slice(h * hd, (h + 1) * hd)
        return (_dot(q_h, wg_ref[hs, :])
                + _dot(k_h, wg_ref[ml_width + h * hd:ml_width + (h + 1) * hd, :])
                + _dot(v_h, wg_ref[2 * ml_width + h * hd:2 * ml_width + (h + 1) * hd, :]))

    def select_blocks(gate):
        lane = lax.broadcasted_iota(jnp.int32, (tm, V7X_LANES), 1)
        n_l = lane & (nb_lanes - 1)
        valid = (n_l < t) & (lane < MOBA_HEADS * nb_lanes)
        gate = jnp.where(valid, gate, -jnp.inf)
        cnt = jnp.zeros((tm, V7X_LANES), jnp.int32)
        for r in range(1, nb_lanes):
            up = pltpu.roll(gate, r, 1)
            cnt = cnt + ((n_l >= r) & (up >= gate)).astype(jnp.int32)
            dn = pltpu.roll(gate, V7X_LANES - r, 1)
            cnt = cnt + ((n_l < nb_lanes - r) & (dn > gate)).astype(jnp.int32)
        keep = (valid & (cnt < n_sel)) | ((n_l == t) & (lane < MOBA_HEADS * nb_lanes))
        selb = jnp.where(keep, 0.0, MASK_VALUE)
        k_tag = (lane == t).astype(BF16)
        v_tag = (lane == 0).astype(BF16)
        for h in range(MOBA_HEADS):
            shift = (V7X_LANES - h * nb_lanes) % V7X_LANES
            sel_h = selb if shift == 0 else pltpu.roll(selb, shift, 1)
            aug = slice((2 * h + 1) * moba_hd, (2 * h + 2) * moba_hd)
            qa_ref[:, aug] = sel_h.astype(BF16)
            ka_ref[:, aug] = k_tag
            va_ref[:, aug] = v_tag

    qkv = []
    va = za = None
    g = jnp.zeros((tm, V7X_LANES), F32) + bg_ref[...]
    for h in range(ml_heads):
        xc_b, xm_b = conv_gate_head(h, *xz[h])
        if pending:
            kind, arg = pending.pop(0)
            if kind == "xz":
                xz[arg] = (x_cols(arg * hd, hd), x_cols(c_zm + arg * hd, hd))
            elif kind == "va":
                va = x_cols(c_va, moba_width).astype(BF16)
            else:
                za = x_cols(c_za, moba_width)
        qkv.append(blockdiag_head(h, xc_b, xm_b))
    if va is None:
        va = x_cols(c_va, moba_width).astype(BF16)
    if za is None:
        za = x_cols(c_za, moba_width)
    for h in range(ml_heads):
        g = g + gate_terms(h, *qkv[h])
    sza_ref[...] = _silu(za).astype(BF16)
    qc = x_cols(c_qc, mem_width)
    zc = x_cols(c_zc, mem_width)

    nrow = grow_ref.shape[0]
    gt = jnp.transpose(g)[0:V7X_SUBLANES, :]
    sub = lax.broadcasted_iota(jnp.int32, gt.shape, 0)
    tok_i = lax.broadcasted_iota(jnp.int32, gt.shape, 1)
    b = _lane_scan(jnp.where(sub >= ml_heads, _log_sigmoid(gt), 0.0), jnp.add, 0.0, tok_i)
    a = _lane_scan(jnp.where(sub >= ml_heads, pltpu.roll(gt, ml_heads, 0) - b, -jnp.inf),
                   jnp.maximum, -jnp.inf, tok_i)
    rows =jnp.concatenate([jnp.where(sub < ml_heads, gt, b), pltpu.roll(a, ml_heads, 0)], axis=0)
    grow_ref[...] = rows
    gcol_ref[...] = jnp.transpose(
        jnp.concatenate([rows, jnp.zeros((V7X_LANES - nrow, tm), F32)], axis=0))

    km_row = lax.broadcasted_iota(jnp.int32, kmean_scr.shape, 0)
    km_head = lax.broadcasted_iota(jnp.int32, kmean_scr.shape, 1) // moba_hd
    kmean_scr[...] = jnp.where(km_row == km_head * nb_lanes + t,
                               jnp.concatenate(kmean_rows, axis=1), kmean_scr[...])
    select_blocks(gate)
    for h in range(MOBA_HEADS):
        va_ref[:, 2 * h * moba_hd:(2 * h + 1) * moba_hd] = va[:, h * moba_hd:(h + 1) * moba_hd]
    qc_ref[...] = (qc * (mem_hd ** -0.5)).astype(BF16)
    szc_ref[...] = _silu(zc).astype(BF16)


def _resident(shape):
    nd = len(shape)
    return pl.BlockSpec(shape, lambda b, t, _nd=nd: (0,) * _nd)


def _proj_call(x, pos, win, convw, convb, wq, wk, wv, wg, bg, normg, skip, invf, sgn,
               *, ml_width, ml_heads, moba_width, mem_width):
    bsz, s, d = x.shape
    tm = TOKEN_TILE
    nt = s // tm
    n_sel = min(MOBA_TOP_K, nt - 1)
    gate_slots = max(V7X_SUBLANES, pl.next_power_of_2(nt))
    assert MOBA_HEADS * gate_slots <= V7X_LANES and s % tm == 0
    tok = lambda w: pl.BlockSpec((None, tm, w), lambda b, t: (b, t, 0))
    out_shapes = (
        jax.ShapeDtypeStruct((bsz, s, ml_width), BF16),
        jax.ShapeDtypeStruct((bsz, ml_width, s), BF16),
        jax.ShapeDtypeStruct((bsz, s, ml_width), BF16),
        jax.ShapeDtypeStruct((bsz, s, ml_width), BF16),
        jax.ShapeDtypeStruct((bsz, s, ml_width), BF16),
        jax.ShapeDtypeStruct((bsz, s, V7X_LANES), F32),
        jax.ShapeDtypeStruct((bsz, GATE_ROWS, s), F32),
        jax.ShapeDtypeStruct((bsz, s, 2 * moba_width), BF16),
        jax.ShapeDtypeStruct((bsz, s, 2 * moba_width), BF16),
        jax.ShapeDtypeStruct((bsz, s, 2 * moba_width), BF16),
        jax.ShapeDtypeStruct((bsz, s, moba_width), BF16),
        jax.ShapeDtypeStruct((bsz, s, mem_width), BF16),
        jax.ShapeDtypeStruct((bsz, s, mem_width), BF16),
    )
    out_specs = (
        tok(ml_width),
        pl.BlockSpec((None, ml_width, tm), lambda b, t: (b, 0, t)),
        tok(ml_width), tok(ml_width), tok(ml_width),
        tok(V7X_LANES),
        pl.BlockSpec((None, GATE_ROWS, tm), lambda b, t: (b, 0, t)),
        tok(2 * moba_width), tok(2 * moba_width), tok(2 * moba_width), tok(moba_width),
        tok(mem_width), tok(mem_width),
    )
    in_specs = [tok(d), tok(1)] + [_resident(a.shape) for a in
                                   (win, convw, convb, wq, wk, wv, wg, bg, normg, skip, invf, sgn)]
    kern = functools.partial(_proj_kernel, ml_width=ml_width, ml_heads=ml_heads,
                             moba_width=moba_width, mem_width=mem_width, n_sel=n_sel,
                             gate_slots=gate_slots)
    return pl.pallas_call(
        kern,
        grid=(bsz, nt),
        in_specs=in_specs,
        out_specs=out_specs,
        out_shape=out_shapes,
        scratch_shapes=[pltpu.VMEM((tm + V7X_SUBLANES, ml_width), F32),
                        pltpu.VMEM((V7X_LANES, moba_width), F32)],
        compiler_params=pltpu.CompilerParams(dimension_semantics=("arbitrary", "arbitrary"),
                                             vmem_limit_bytes=V7X_VMEM_LIMIT_BYTES),
        name="proj",
    )(x, pos, win, convw, convb, wq, wk, wv, wg, bg, normg, skip, invf, sgn)


def _mlstm_kernel(qm_ref, kmt_ref, vm_ref, gcol_ref, grow_ref, add_ref, mul_ref, out_ref,
                  c_scr, n_scr, m_scr, *, heads):
    c = pl.program_id(1)
    L = qm_ref.shape[0]
    hd = qm_ref.shape[1] // heads

    @pl.when(c == 0)
    def _():
        c_scr[...] = jnp.zeros(c_scr.shape, F32)
        n_scr[...] = jnp.zeros(n_scr.shape, F32)
        m_scr[...] = jnp.zeros(m_scr.shape, F32)

    row_i = lax.broadcasted_iota(jnp.int32, (L, L), 0)
    col_i = lax.broadcasted_iota(jnp.int32, (L, L), 1)
    causal = row_i >= col_i
    gcol = gcol_ref[...]
    grow = grow_ref[...]
    ones = jnp.ones((L, V7X_LANES), BF16)
    head_slices = [slice(h * hd, (h + 1) * hd) for h in range(heads)]

    qk, inter, qn, mm_l, r_l, w_inter_l = [], [], [], [], [], []
    for h, hs in enumerate(head_slices):
        q = qm_ref[:, hs]
        kt = kmt_ref[hs, :]
        v = vm_ref[:, hs]
        i_row = grow[h:h + 1, :]
        b_row = grow[heads + h:heads + h + 1, :]
        a_col = gcol[:, 2 * heads + h:2 * heads + h + 1]
        m_prev = m_scr[h][0:1, 0:1]
        c_prev = c_scr[h]
        n_prev = n_scr[h]
        qk.append(_dot(q, kt))
        inter.append(_dot(q, c_prev.astype(BF16)))
        qn.append(_dot(q, n_prev.astype(BF16)))
        r_row = i_row - b_row
        mm = jnp.maximum(m_prev, a_col)
        r_l.append(r_row)
        mm_l.append(mm)
        w_inter_l.append(jnp.exp(m_prev - mm))
        b_end = b_row[:, L - 1:L]
        log_w = b_end + r_row
        m_new = jnp.maximum(b_end + m_prev, jnp.max(log_w, axis=1, keepdims=True))
        decay = jnp.exp(b_end + m_prev - m_new)
        kw = (kt.astype(F32) * jnp.exp(log_w - m_new)).astype(BF16)
        c_scr[h] = decay * c_prev + _dot(kw, v)
        n_scr[h] = decay * n_prev + _dot(kw, ones)
        m_scr[h] = jnp.broadcast_to(m_new, m_scr.shape[1:])

    s_qk = [(qk[h] * jnp.exp(jnp.where(causal, r_l[h] - mm_l[h], -jnp.inf))).astype(BF16)
            for h in range(heads)]
    pv = [_dot(s_qk[h], vm_ref[:, hs]) for h, hs in enumerate(head_slices)]
    rs = [_dot(s_qk[h], ones) for h in range(heads)]

    for h, hs in enumerate(head_slices):
        b_col = gcol[:, heads + h:heads + h + 1]
        num = w_inter_l[h] * inter[h] + pv[h]
        den = w_inter_l[h] * qn[h] + rs[h]
        rec = 1.0 / jnp.maximum(jnp.abs(den), jnp.exp(-(b_col + mm_l[h])))
        hh = num * jnp.concatenate([rec] * (hd // V7X_LANES), axis=1)
        mu = jnp.mean(hh, axis=1, keepdims=True)
        hc = hh - mu
        var = jnp.mean(hc * hc, axis=1, keepdims=True)
        hn = hc * lax.rsqrt(var + LN_EPS)
        out_ref[:, hs] = (hn * mul_ref[:, hs].astype(F32) + add_ref[:, hs].astype(F32)).astype(BF16)


def _mlstm_call(qm, kmt, vm, gcol, grow, add, mul, *, heads):
    bsz, s, w = qm.shape
    L = TOKEN_TILE
    hd = w // heads
    tok = lambda width: pl.BlockSpec((None, L, width), lambda b, c: (b, c, 0))
    tmaj = lambda rows: pl.BlockSpec((None, rows, L), lambda b, c: (b, 0, c))
    return pl.pallas_call(
        functools.partial(_mlstm_kernel, heads=heads),
        grid=(bsz, s // L),
        in_specs=[tok(w), tmaj(w), tok(w), tok(V7X_LANES), tmaj(GATE_ROWS), tok(w), tok(w)],
        out_specs=tok(w),
        out_shape=jax.ShapeDtypeStruct((bsz, s, w), BF16),
        scratch_shapes=[pltpu.VMEM((heads, hd, hd), F32),
                        pltpu.VMEM((heads, hd, V7X_LANES), F32),
                        pltpu.VMEM((heads, V7X_SUBLANES, V7X_LANES), F32)],
        compiler_params=pltpu.CompilerParams(dimension_semantics=("arbitrary", "arbitrary"),
                                             vmem_limit_bytes=V7X_VMEM_LIMIT_BYTES),
        name="mlstm",
    )(qm, kmt, vm, gcol, grow, add, mul)


def _moba_kernel(q_ref, k_ref, v_ref, sz_ref, out_ref, s_scr, m_scr, acc_scr):
    j = pl.program_id(1)
    tq = q_ref.shape[0]
    bs = MOBA_BLOCK
    aw = q_ref.shape[1] // MOBA_HEADS
    hd = aw // 2
    half = bs // 2
    causal = (lax.broadcasted_iota(jnp.int32, (tq, bs), 0) >= lax.broadcasted_iota(jnp.int32, (tq, bs), 1))

    own = pl.multiple_of(j * bs, bs)
    for h in range(MOBA_HEADS):
        hs = slice(h * aw, (h + 1) * aw)
        s = jnp.where(causal, _dot_nt(q_ref[:, hs], k_ref[pl.ds(own, bs), hs]), MASK_VALUE)
        s_scr[h, j] = s
        m_scr[h] = jnp.maximum(s[:, :half], s[:, half:])

    def scores(n, carry):
        start = pl.multiple_of(n * bs, bs)
        for h in range(MOBA_HEADS):
            hs = slice(h * aw, (h + 1) * aw)
            s = _dot_nt(q_ref[:, hs], k_ref[pl.ds(start, bs), hs])
            s_scr[h, n] = s
            m_scr[h] = jnp.maximum(m_scr[h], jnp.maximum(s[:, :half], s[:, half:]))
        return carry

    lax.fori_loop(0, j, scores, 0)

    for h in range(MOBA_HEADS):
        m_scr[h] = jnp.broadcast_to(jnp.max(m_scr[h], axis=1, keepdims=True), m_scr.shape[1:])
        acc_scr[h] = jnp.zeros(acc_scr.shape[1:], F32)

    def weighted(n, carry):
        start = pl.multiple_of(n * bs, bs)
        for h in range(MOBA_HEADS):
            hs = slice(h * aw, (h + 1) * aw)
            m = m_scr[h]
            s = s_scr[h, n]
            p = jnp.concatenate([jnp.exp(s[:, :half] - m), jnp.exp(s[:, half:] - m)], axis=1)
            acc_scr[h] += _dot(p.astype(BF16), v_ref[pl.ds(start, bs), hs])
        return carry

    lax.fori_loop(0, j + 1, weighted, 0)

    for h in range(MOBA_HEADS):
        acc = acc_scr[h]
        o = acc[:, :hd] * (1.0 / acc[:, hd:hd + 1])
        out_ref[:, h * hd:(h + 1) * hd] = (o * sz_ref[:, h * hd:(h + 1) * hd].astype(F32)).astype(BF16)


def _moba_call(qa, ka, va, sza):
    bsz, s, aw_all = qa.shape
    w = sza.shape[2]
    tq = TOKEN_TILE
    assert tq == MOBA_BLOCK and aw_all // MOBA_HEADS == MOBA_BLOCK
    nblk = s // MOBA_BLOCK
    tok = lambda width: pl.BlockSpec((None, tq, width), lambda b, j: (b, j, 0))
    kvspec = pl.BlockSpec((None, s, aw_all), lambda b, j: (b, 0, 0))
    return pl.pallas_call(
        _moba_kernel,
        grid=(bsz, s // tq),
        in_specs=[tok(aw_all), kvspec, kvspec, tok(w)],
        out_specs=tok(w),
        out_shape=jax.ShapeDtypeStruct((bsz, s, w), BF16),
        scratch_shapes=[pltpu.VMEM((MOBA_HEADS, nblk, tq, MOBA_BLOCK), F32),
                        pltpu.VMEM((MOBA_HEADS, tq, MOBA_BLOCK // 2), F32),
                        pltpu.VMEM((MOBA_HEADS, tq, aw_all // MOBA_HEADS), F32)],
        compiler_params=pltpu.CompilerParams(dimension_semantics=("arbitrary", "arbitrary"),
                                             vmem_limit_bytes=V7X_VMEM_LIMIT_BYTES),
        name="moba",
    )(qa, ka, va, sza)


def _memattn_kernel(q_ref, mem_ref, wkv_ref, sz_ref, out_ref, kv_scr):
    j = pl.program_id(1)
    width = q_ref.shape[1]
    hd = width // MEM_HEADS

    @pl.when(j == 0)
    def _():
        kv_scr[...] = _dot(mem_ref[...].astype(BF16), wkv_ref[...]).astype(BF16)

    head_slices = [slice(h * hd, (h + 1) * hd) for h in range(MEM_HEADS)]
    scores = [_dot_nt(q_ref[:, hs], kv_scr[:, hs]) for hs in head_slices]
    probs = [jnp.exp(s - jnp.max(s, axis=1, keepdims=True)) for s in scores]
    outs = [_dot(p.astype(BF16), kv_scr[:, width + h * hd:width + (h + 1) * hd])
            for h, p in enumerate(probs)]
    for h, hs in enumerate(head_slices):
        l = jnp.sum(probs[h], axis=1, keepdims=True)
        out_ref[:, hs] = (outs[h] * (1.0 / l) * sz_ref[:, hs].astype(F32)).astype(BF16)


def _memattn_call(qc, mem, wkv, szc):
    bsz, s, w = qc.shape
    _, m_tok, d = mem.shape
    tq = TOKEN_TILE
    tok = pl.BlockSpec((None, tq, w), lambda b, j: (b, j, 0))
    return pl.pallas_call(
        _memattn_kernel,
        grid=(bsz, s // tq),
        in_specs=[tok, pl.BlockSpec((None, m_tok, d), lambda b, j: (b, 0, 0)),
                  pl.BlockSpec(wkv.shape, lambda b, j: (0, 0)), tok],
        out_specs=tok,
        out_shape=jax.ShapeDtypeStruct((bsz, s, w), BF16),
        scratch_shapes=[pltpu.VMEM((m_tok, 2 * w), BF16)],
        compiler_params=pltpu.CompilerParams(dimension_semantics=("arbitrary", "arbitrary"),
                                             vmem_limit_bytes=V7X_VMEM_LIMIT_BYTES),
        name="memattn",
    )(qc, mem, wkv, szc)


def _outproj_kernel(x_ref, om_ref, oa_ref, oc_ref, wo_ref, lng_ref, lnb_ref, y_ref):
    w_m = om_ref.shape[1]
    w_a = oa_ref.shape[1]
    mixed = (_dot(om_ref[...], wo_ref[0:w_m, :])
             + _dot(oa_ref[...], wo_ref[w_m:w_m + w_a, :])
             + _dot(oc_ref[...], wo_ref[w_m + w_a:, :]))
    y = DEEPNORM_ALPHA * x_ref[...] + mixed
    mu = jnp.mean(y, axis=1, keepdims=True)
    yc = y - mu
    var = jnp.mean(yc * yc, axis=1, keepdims=True)
    y_ref[...] = yc * lax.rsqrt(var + LN_EPS) * lng_ref[...] + lnb_ref[...]


def _outproj_call(x, om, oa, oc, wo, lng, lnb):
    bsz, s, d = x.shape
    tm = 2 * TOKEN_TILE
    tok = lambda w: pl.BlockSpec((None, tm, w), lambda b, t: (b, t, 0))
    res = lambda a: pl.BlockSpec(a.shape, lambda b, t: (0, 0))
    return pl.pallas_call(
        _outproj_kernel,
        grid=(bsz, s // tm),
        in_specs=[tok(d), tok(om.shape[2]), tok(oa.shape[2]), tok(oc.shape[2]), res(wo), res(lng), res(lnb)],
        out_specs=tok(d),
        out_shape=jax.ShapeDtypeStruct((bsz, s, d), x.dtype),
        compiler_params=pltpu.CompilerParams(dimension_semantics=("arbitrary", "arbitrary"),
                                             vmem_limit_bytes=V7X_VMEM_LIMIT_BYTES),
        name="outproj",
    )(x, om, oa, oc, wo, lng, lnb)


def _diag_tiles(w, heads):
    groups, blk, _ = w.shape
    hd = groups // heads * blk
    rows = w.reshape(heads, hd, blk)
    idx = jnp.arange(hd)
    same_group = (idx[:, None] // blk) == (idx[None, :] // blk)
    return jnp.where(same_group, jnp.tile(rows, (1, 1, hd // blk)), 0.0)


def kernel(x, mem, positions, w_in, mlstm_conv_w, mlstm_conv_b, mlstm_wq, mlstm_wk, mlstm_wv, mlstm_w_gates, mlstm_b_gates, mlstm_norm_g, mlstm_skip, w_mem_kv, w_out, ln_g, ln_b):
    bsz, s, d = x.shape
    ml_width = mlstm_conv_w.shape[1]
    ml_heads = mlstm_b_gates.shape[0] // 2
    mem_width = w_mem_kv.shape[1] // 2
    moba_width = (w_in.shape[1] - 2 * ml_width - 2 * mem_width) // 4
    moba_hd = moba_width // MOBA_HEADS
    assert moba_hd == V7X_LANES and mem_width // MEM_HEADS == V7X_LANES
    assert s % (2 * TOKEN_TILE) == 0

    row = lambda a: a.reshape(1, -1).astype(F32)
    wq = _diag_tiles(mlstm_wq, ml_heads).astype(BF16)
    wk = _diag_tiles(mlstm_wk, ml_heads).astype(BF16)
    wv = _diag_tiles(mlstm_wv, ml_heads).astype(BF16)
    n_gates = mlstm_w_gates.shape[1]
    wg = jnp.pad(mlstm_w_gates, ((0, 0), (0, V7X_LANES - n_gates))).astype(BF16)
    bg = jnp.pad(mlstm_b_gates, (0, V7X_LANES - n_gates)).reshape(1, -1).astype(F32)
    half = moba_hd // 2
    inv_freq = ROPE_THETA ** (-jnp.arange(half, dtype=F32) * 2.0 / moba_hd)
    invf = jnp.concatenate([inv_freq, inv_freq]).reshape(1, -1)
    sgn = jnp.concatenate([-jnp.ones((half,), F32), jnp.ones((half,), F32)]).reshape(1, -1)
    pos = positions.astype(F32).reshape(bsz, s, 1)

    (qm, kmt, vm, add, mul, gcol, grow, qa, ka, va, sza, qc, szc) = _proj_call(
        x, pos, w_in.astype(BF16), mlstm_conv_w.astype(F32), row(mlstm_conv_b), wq, wk, wv, wg, bg,
        row(mlstm_norm_g), row(mlstm_skip), invf, sgn,
        ml_width=ml_width, ml_heads=ml_heads, moba_width=moba_width, mem_width=mem_width)
    out_m = _mlstm_call(qm, kmt, vm, gcol, grow, add, mul, heads=ml_heads)
    out_a = _moba_call(qa, ka, va, sza)
    out_c = _memattn_call(qc, mem, w_mem_kv.astype(BF16), szc)
    return _outproj_call(x, out_m, out_a, out_c, w_out.astype(BF16), row(ln_g), row(ln_b))
```

```python
import functools
import math

import jax
import jax.numpy as jnp
from jax import lax
from jax.experimental import pallas as pl
from jax.experimental.pallas import tpu as pltpu

MOBA_HEADS = 4
MOBA_BLOCK = 256
MOBA_TOP_K = 3
MEM_HEADS = 4
ROPE_THETA = 10000.0
DEPTH = 1
DEEPNORM_ALPHA = (2 * DEPTH) ** 0.25
LN_EPS = 1e-5

V7X_LANES = 128
V7X_SUBLANES = 8
V7X_VMEM_LIMIT_BYTES = 56 * 1024 * 1024

TOKEN_TILE = MOBA_BLOCK
MASK_VALUE = -1e30
GATE_ROWS = 2 * V7X_SUBLANES

F32 = jnp.float32
BF16 = jnp.bfloat16


def _dot(a, b):
    return jnp.dot(a, b, preferred_element_type=F32)


def _dot_nt(a, b):
    return lax.dot_general(a, b, (((1,), (1,)), ((), ())), preferred_element_type=F32)


def _split3(v):
    hi = v.astype(BF16)
    r1 = v - hi.astype(F32)
    mid = r1.astype(BF16)
    lo = (r1 - mid.astype(F32)).astype(BF16)
    return hi, mid, lo


def _lane_scan(v, op, identity, lane_idx):
    shift = 1
    while shift < v.shape[-1]:
        v = op(v, jnp.where(lane_idx >= shift, pltpu.roll(v, shift, v.ndim - 1), identity))
        shift *= 2
    return v


def _silu(v):
    return v * jax.nn.sigmoid(v)


def _log_sigmoid(v):
    return jnp.minimum(v, 0.0) - jnp.log1p(jnp.exp(-jnp.abs(v)))


def _proj_kernel(x_ref, pos_ref, win_ref, convw_ref, convb_ref, wq_ref, wk_ref, wv_ref,
                 wg_ref, bg_ref, normg_ref, skip_ref, invf_ref, sgn_ref,
                 qm_ref, kmt_ref, vm_ref, add_ref, mul_ref, gcol_ref, grow_ref,
                 qa_ref, ka_ref, va_ref, sza_ref, qc_ref, szc_ref,
                 xpad_scr, kmean_scr,
                 *, ml_width, ml_heads, moba_width, mem_width, n_sel, gate_slots):
    t = pl.program_id(1)
    tm = x_ref.shape[0]
    hd = ml_width // ml_heads
    moba_hd = moba_width // MOBA_HEADS
    mem_hd = mem_width // MEM_HEADS
    c_zm = ml_width
    c_qa = 2 * ml_width
    c_ka = c_qa + moba_width
    c_va = c_ka + moba_width
    c_za = c_va + moba_width
    c_qc = c_za + moba_width
    c_zc = c_qc + mem_width

    xb = x_ref[...].astype(BF16)
    pad = xpad_scr.shape[0] - tm
    k_w = convw_ref.shape[0]
    nb_lanes = gate_slots

    @pl.when(t == 0)
    def _():
        xpad_scr[0:pad, :] = jnp.zeros((pad, ml_width), F32)
        kmean_scr[...] = jnp.zeros(kmean_scr.shape, F32)

    @pl.when(t > 0)
    def _():
        xpad_scr[0:pad, :] = xpad_scr[tm:tm + pad, :]

    def x_cols(c0, width):
        return _dot(xb, win_ref[:, c0:c0 + width])

    def rotary_pair(p, qp, kp):
        rots, means = [], []
        for i in range(2):
            h = 2 * p + i
            ls = slice(i * moba_hd, (i + 1) * moba_hd)
            q_rot = qp[:, ls] * cosf + pltpu.roll(qp[:, ls], moba_hd // 2, 1) * sins
            k_rot = kp[:, ls] * cosf + pltpu.roll(kp[:, ls], moba_hd // 2, 1) * sins
            qa_ref[:, 2 * h * moba_hd:(2 * h + 1) * moba_hd] = (q_rot * (moba_hd ** -0.5)).astype(BF16)
            ka_ref[:, 2 * h * moba_hd:(2 * h + 1) * moba_hd] = k_rot.astype(BF16)
            rots.append(q_rot)
            means.append(jnp.mean(k_rot, axis=0, keepdims=True))
        return rots, means

    def conv_gate_head(h, xm_h, zm_h):
        hs = slice(h * hd, (h + 1) * hd)
        xpad_scr[pad:pad + tm, hs] = xm_h
        conv = convb_ref[:, hs] + xm_h * convw_ref[k_w - 1:k_w, hs]
        for j in range(k_w - 1):
            conv = conv + xpad_scr[pl.ds(pad - (k_w - 1) + j, tm), hs] * convw_ref[j:j + 1, hs]
        xc_h = _silu(conv)
        sz_h = _silu(zm_h)
        add_ref[:, hs] = (skip_ref[:, hs] * xc_h * sz_h).astype(BF16)
        mul_ref[:, hs] = (normg_ref[:, hs] * sz_h).astype(BF16)
        return xc_h.astype(BF16), xm_h.astype(BF16)

    def blockdiag_head(h, xc_b, xm_b):
        hs = slice(h * hd, (h + 1) * hd)
        q_h = _dot(xc_b, wq_ref[h]).astype(BF16)
        k_f = _dot(xc_b, wk_ref[h])
        v_h = _dot(xm_b, wv_ref[h]).astype(BF16)
        qm_ref[:, hs] = q_h
        vm_ref[:, hs] = v_h
        kmt_ref[hs, :] = (jnp.transpose(k_f) * (hd ** -0.5)).astype(BF16)
        return q_h, k_f.astype(BF16), v_h

    pw = 2 * moba_hd
    qa0, ka0 = x_cols(c_qa, pw), x_cols(c_ka, pw)
    ang = pos_ref[...] * invf_ref[...]
    cosf = jnp.cos(ang)
    sins = jnp.sin(ang) * sgn_ref[...]
    qa1, ka1 = x_cols(c_qa + pw, pw), x_cols(c_ka + pw, pw)
    rots0, means0 = rotary_pair(0, qa0, ka0)
    xm0, zm0 = x_cols(0, hd), x_cols(c_zm, hd)
    rots1, means1 = rotary_pair(1, qa1, ka1)
    q_rots, kmean_rows = rots0 + rots1, means0 + means1
    q_all = jnp.concatenate(q_rots, axis=1)
    km = kmean_scr[...]
    q_hi = q_all.astype(BF16)
    q_lo = (q_all - q_hi.astype(F32)).astype(BF16)
    k_hi = km.astype(BF16)
    k_lo = (km - k_hi.astype(F32)).astype(BF16)
    xz = {0: (xm0, zm0), 1: (x_cols(hd, hd), x_cols(c_zm + hd, hd))}
    gate = _dot_nt(q_hi, k_hi) + _dot_nt(q_lo, k_hi) + _dot_nt(q_hi, k_lo)
    pending = [("xz", h) for h in range(2, ml_heads)] + [("va", None), ("za", None)]
    def gate_terms(h, q_h, k_h, v_h):
        hs = slice(h * hd, (h + 1) * hd)
        return (_dot(q_h, wg_ref[hs, :])
                + _dot(k_h, wg_ref[ml_width + h * hd:ml_width + (h + 1) * hd, :])
                + _dot(v_h, wg_ref[2 * ml_width + h * hd:2 * ml_width + (h + 1) * hd, :]))

    def select_blocks(gate):
        lane = lax.broadcasted_iota(jnp.int32, (tm, V7X_LANES), 1)
        n_l = lane & (nb_lanes - 1)
        valid = (n_l < t) & (lane < MOBA_HEADS * nb_lanes)
        gate = jnp.where(valid, gate, -jnp.inf)
        cnt = jnp.zeros((tm, V7X_LANES), jnp.int32)
        for r in range(1, nb_lanes):
            up = pltpu.roll(gate, r, 1)
            cnt = cnt + ((n_l >= r) & (up >= gate)).astype(jnp.int32)
            dn = pltpu.roll(gate, V7X_LANES - r, 1)
            cnt = cnt + ((n_l < nb_lanes - r) & (dn > gate)).astype(jnp.int32)
        keep = (valid & (cnt < n_sel)) | ((n_l == t) & (lane < MOBA_HEADS * nb_lanes))
        selb = jnp.where(keep, 0.0, MASK_VALUE)
        k_tag = (lane == t).astype(BF16)
        v_tag = (lane == 0).astype(BF16)
        for h in range(MOBA_HEADS):
            shift = (V7X_LANES - h * nb_lanes) % V7X_LANES
            sel_h = selb if shift == 0 else pltpu.roll(selb, shift, 1)
            aug = slice((2 * h + 1) * moba_hd, (2 * h + 2) * moba_hd)
            qa_ref[:, aug] = sel_h.astype(BF16)
            ka_ref[:, aug] = k_tag
            va_ref[:, aug] = v_tag

    qkv = []
    va = za = None
    g = jnp.zeros((tm, V7X_LANES), F32) + bg_ref[...]
    for h in range(ml_heads):
        xc_b, xm_b = conv_gate_head(h, *xz[h])
        if pending:
            kind, arg = pending.pop(0)
            if kind == "xz":
                xz[arg] = (x_cols(arg * hd, hd), x_cols(c_zm + arg * hd, hd))
            elif kind == "va":
                va = x_cols(c_va, moba_width).astype(BF16)
            else:
                za = x_cols(c_za, moba_width)
        qkv.append(blockdiag_head(h, xc_b, xm_b))
    if va is None:
        va = x_cols(c_va, moba_width).astype(BF16)
    if za is None:
        za = x_cols(c_za, moba_width)
    for h in range(ml_heads):
        g = g + gate_terms(h, *qkv[h])
    sza_ref[...] = _silu(za).astype(BF16)
    qc = x_cols(c_qc, mem_width)
    zc = x_cols(c_zc, mem_width)

    nrow = grow_ref.shape[0]
    gt = jnp.transpose(g)[0:V7X_SUBLANES, :]
    sub = lax.broadcasted_iota(jnp.int32, gt.shape, 0)
    tok_i = lax.broadcasted_iota(jnp.int32, gt.shape, 1)
    b = _lane_scan(jnp.where(sub >= ml_heads, _log_sigmoid(gt), 0.0), jnp.add, 0.0, tok_i)
    a = _lane_scan(jnp.where(sub >= ml_heads, pltpu.roll(gt, ml_heads, 0) - b, -jnp.inf),
                   jnp.maximum, -jnp.inf, tok_i)
    rows =jnp.concatenate([jnp.where(sub < ml_heads, gt, b), pltpu.roll(a, ml_heads, 0)], axis=0)
    grow_ref[...] = rows
    gcol_ref[...] = jnp.transpose(
        jnp.concatenate([rows, jnp.zeros((V7X_LANES - nrow, tm), F32)], axis=0))

    km_row = lax.broadcasted_iota(jnp.int32, kmean_scr.shape, 0)
    km_head = lax.broadcasted_iota(jnp.int32, kmean_scr.shape, 1) // moba_hd
    kmean_scr[...] = jnp.where(km_row == km_head * nb_lanes + t,
                               jnp.concatenate(kmean_rows, axis=1), kmean_scr[...])
    select_blocks(gate)
    for h in range(MOBA_HEADS):
        va_ref[:, 2 * h * moba_hd:(2 * h + 1) * moba_hd] = va[:, h * moba_hd:(h + 1) * moba_hd]
    qc_ref[...] = (qc * (mem_hd ** -0.5)).astype(BF16)
    szc_ref[...] = _silu(zc).astype(BF16)


def _resident(shape):
    nd = len(shape)
    return pl.BlockSpec(shape, lambda b, t, _nd=nd: (0,) * _nd)


def _proj_call(x, pos, win, convw, convb, wq, wk, wv, wg, bg, normg, skip, invf, sgn,
               *, ml_width, ml_heads, moba_width, mem_width):
    bsz, s, d = x.shape
    tm = TOKEN_TILE
    nt = s // tm
    n_sel = min(MOBA_TOP_K, nt - 1)
    gate_slots = max(V7X_SUBLANES, pl.next_power_of_2(nt))
    assert MOBA_HEADS * gate_slots <= V7X_LANES and s % tm == 0
    tok = lambda w: pl.BlockSpec((None, tm, w), lambda b, t: (b, t, 0))
    out_shapes = (
        jax.ShapeDtypeStruct((bsz, s, ml_width), BF16),
        jax.ShapeDtypeStruct((bsz, ml_width, s), BF16),
        jax.ShapeDtypeStruct((bsz, s, ml_width), BF16),
        jax.ShapeDtypeStruct((bsz, s, ml_width), BF16),
        jax.ShapeDtypeStruct((bsz, s, ml_width), BF16),
        jax.ShapeDtypeStruct((bsz, s, V7X_LANES), F32),
        jax.ShapeDtypeStruct((bsz, GATE_ROWS, s), F32),
        jax.ShapeDtypeStruct((bsz, s, 2 * moba_width), BF16),
        jax.ShapeDtypeStruct((bsz, s, 2 * moba_width), BF16),
        jax.ShapeDtypeStruct((bsz, s, 2 * moba_width), BF16),
        jax.ShapeDtypeStruct((bsz, s, moba_width), BF16),
        jax.ShapeDtypeStruct((bsz, s, mem_width), BF16),
        jax.ShapeDtypeStruct((bsz, s, mem_width), BF16),
    )
    out_specs = (
        tok(ml_width),
        pl.BlockSpec((None, ml_width, tm), lambda b, t: (b, 0, t)),
        tok(ml_width), tok(ml_width), tok(ml_width),
        tok(V7X_LANES),
        pl.BlockSpec((None, GATE_ROWS, tm), lambda b, t: (b, 0, t)),
        tok(2 * moba_width), tok(2 * moba_width), tok(2 * moba_width), tok(moba_width),
        tok(mem_width), tok(mem_width),
    )
    in_specs = [tok(d), tok(1)] + [_resident(a.shape) for a in
                                   (win, convw, convb, wq, wk, wv, wg, bg, normg, skip, invf, sgn)]
    kern = functools.partial(_proj_kernel, ml_width=ml_width, ml_heads=ml_heads,
                             moba_width=moba_width, mem_width=mem_width, n_sel=n_sel,
                             gate_slots=gate_slots)
    return pl.pallas_call(
        kern,
        grid=(bsz, nt),
        in_specs=in_specs,
        out_specs=out_specs,
        out_shape=out_shapes,
        scratch_shapes=[pltpu.VMEM((tm + V7X_SUBLANES, ml_width), F32),
                        pltpu.VMEM((V7X_LANES, moba_width), F32)],
        compiler_params=pltpu.CompilerParams(dimension_semantics=("arbitrary", "arbitrary"),
                                             vmem_limit_bytes=V7X_VMEM_LIMIT_BYTES),
        name="proj",
    )(x, pos, win, convw, convb, wq, wk, wv, wg, bg, normg, skip, invf, sgn)


def _mlstm_kernel(qm_ref, kmt_ref, vm_ref, gcol_ref, grow_ref, add_ref, mul_ref, out_ref,
                  c_scr, n_scr, m_scr, *, heads):
    c = pl.program_id(1)
    L = qm_ref.shape[0]
    hd = qm_ref.shape[1] // heads

    @pl.when(c == 0)
    def _():
        c_scr[...] = jnp.zeros(c_scr.shape, F32)
        n_scr[...] = jnp.zeros(n_scr.shape, F32)
        m_scr[...] = jnp.zeros(m_scr.shape, F32)

    row_i = lax.broadcasted_iota(jnp.int32, (L, L), 0)
    col_i = lax.broadcasted_iota(jnp.int32, (L, L), 1)
    causal = row_i >= col_i
    gcol = gcol_ref[...]
    grow = grow_ref[...]
    ones = jnp.ones((L, V7X_LANES), BF16)
    head_slices = [slice(h * hd, (h + 1) * hd) for h in range(heads)]

    qk, inter, qn, mm_l, r_l, w_inter_l = [], [], [], [], [], []
    for h, hs in enumerate(head_slices):
        q = qm_ref[:, hs]
        kt = kmt_ref[hs, :]
        v = vm_ref[:, hs]
        i_row = grow[h:h + 1, :]
        b_row = grow[heads + h:heads + h + 1, :]
        a_col = gcol[:, 2 * heads + h:2 * heads + h + 1]
        m_prev = m_scr[h][0:1, 0:1]
        c_prev = c_scr[h]
        n_prev = n_scr[h]
        qk.append(_dot(q, kt))
        inter.append(_dot(q, c_prev.astype(BF16)))
        qn.append(_dot(q, n_prev.astype(BF16)))
        r_row = i_row - b_row
        mm = jnp.maximum(m_prev, a_col)
        r_l.append(r_row)
        mm_l.append(mm)
        w_inter_l.append(jnp.exp(m_prev - mm))
        b_end = b_row[:, L - 1:L]
        log_w = b_end + r_row
        m_new = jnp.maximum(b_end + m_prev, jnp.max(log_w, axis=1, keepdims=True))
        decay = jnp.exp(b_end + m_prev - m_new)
        kw = (kt.astype(F32) * jnp.exp(log_w - m_new)).astype(BF16)
        c_scr[h] = decay * c_prev + _dot(kw, v)
        n_scr[h] = decay * n_prev + _dot(kw, ones)
        m_scr[h] = jnp.broadcast_to(m_new, m_scr.shape[1:])

    s_qk = [(qk[h] * jnp.exp(jnp.where(causal, r_l[h] - mm_l[h], -jnp.inf))).astype(BF16)
            for h in range(heads)]
    pv = [_dot(s_qk[h], vm_ref[:, hs]) for h, hs in enumerate(head_slices)]
    rs = [_dot(s_qk[h], ones) for h in range(heads)]

    for h, hs in enumerate(head_slices):
        b_col = gcol[:, heads + h:heads + h + 1]
        num = w_inter_l[h] * inter[h] + pv[h]
        den = w_inter_l[h] * qn[h] + rs[h]
        rec = 1.0 / jnp.maximum(jnp.abs(den), jnp.exp(-(b_col + mm_l[h])))
        hh = num * jnp.concatenate([rec] * (hd // V7X_LANES), axis=1)
        mu = jnp.mean(hh, axis=1, keepdims=True)
        hc = hh - mu
        var = jnp.mean(hc * hc, axis=1, keepdims=True)
        hn = hc * lax.rsqrt(var + LN_EPS)
        out_ref[:, hs] = (hn * mul_ref[:, hs].astype(F32) + add_ref[:, hs].astype(F32)).astype(BF16)


def _mlstm_call(qm, kmt, vm, gcol, grow, add, mul, *, heads):
    bsz, s, w = qm.shape
    L = TOKEN_TILE
    hd = w // heads
    tok = lambda width: pl.BlockSpec((None, L, width), lambda b, c: (b, c, 0))
    tmaj = lambda rows: pl.BlockSpec((None, rows, L), lambda b, c: (b, 0, c))
    return pl.pallas_call(
        functools.partial(_mlstm_kernel, heads=heads),
        grid=(bsz, s // L),
        in_specs=[tok(w), tmaj(w), tok(w), tok(V7X_LANES), tmaj(GATE_ROWS), tok(w), tok(w)],
        out_specs=tok(w),
        out_shape=jax.ShapeDtypeStruct((bsz, s, w), BF16),
        scratch_shapes=[pltpu.VMEM((heads, hd, hd), F32),
                        pltpu.VMEM((heads, hd, V7X_LANES), F32),
                        pltpu.VMEM((heads, V7X_SUBLANES, V7X_LANES), F32)],
        compiler_params=pltpu.CompilerParams(dimension_semantics=("arbitrary", "arbitrary"),
                                             vmem_limit_bytes=V7X_VMEM_LIMIT_BYTES),
        name="mlstm",
    )(qm, kmt, vm, gcol, grow, add, mul)


def _moba_kernel(q_ref, k_ref, v_ref, sz_ref, out_ref, s_scr, m_scr):
    j = pl.program_id(1)
    tq = q_ref.shape[0]
    bs = MOBA_BLOCK
    nblk = k_ref.shape[0] // bs
    aw = q_ref.shape[1] // MOBA_HEADS
    hd = aw // 2
    half = bs // 2
    heads = [slice(h * aw, (h + 1) * aw) for h in range(MOBA_HEADS)]

    def attend(own):
        causal = (lax.broadcasted_iota(jnp.int32, (tq, bs), 0)
                  >= lax.broadcasted_iota(jnp.int32, (tq, bs), 1))
        for hs_i, hs in enumerate(heads):
            mx = None
            for n in range(own + 1):
                s = _dot_nt(q_ref[:, hs], k_ref[n * bs:(n + 1) * bs, hs])
                if n == own:
                    s = jnp.where(causal, s, MASK_VALUE)
                s_scr[hs_i, n] = s
                sm = jnp.maximum(s[:, :half], s[:, half:])
                mx = sm if mx is None else jnp.maximum(mx, sm)
            m_scr[hs_i] = jnp.broadcast_to(jnp.max(mx, axis=1, keepdims=True), m_scr.shape[1:])
        for hs_i, hs in enumerate(heads):
            m = m_scr[hs_i]
            acc = None
            for n in range(own + 1):
                s = s_scr[hs_i, n]
                p = jnp.concatenate([jnp.exp(s[:, :half] - m), jnp.exp(s[:, half:] - m)], axis=1)
                pv = _dot(p.astype(BF16), v_ref[n * bs:(n + 1) * bs, hs])
                acc = pv if acc is None else acc + pv
            o = acc[:, :hd] * (1.0 / acc[:, hd:hd + 1])
            out_ref[:, hs_i * hd:(hs_i + 1) * hd] = (
                o * sz_ref[:, hs_i * hd:(hs_i + 1) * hd].astype(F32)).astype(BF16)

    for own in range(nblk):
        pl.when(j == own)(functools.partial(attend, own))


def _moba_call(qa, ka, va, sza):
    bsz, s, aw_all = qa.shape
    w = sza.shape[2]
    tq = TOKEN_TILE
    assert tq == MOBA_BLOCK and aw_all // MOBA_HEADS == MOBA_BLOCK
    nblk = s // MOBA_BLOCK
    tok = lambda width: pl.BlockSpec((None, tq, width), lambda b, j: (b, j, 0))
    kvspec = pl.BlockSpec((None, s, aw_all), lambda b, j: (b, 0, 0))
    return pl.pallas_call(
        _moba_kernel,
        grid=(bsz, s // tq),
        in_specs=[tok(aw_all), kvspec, kvspec, tok(w)],
        out_specs=tok(w),
        out_shape=jax.ShapeDtypeStruct((bsz, s, w), BF16),
        scratch_shapes=[pltpu.VMEM((MOBA_HEADS, nblk, tq, MOBA_BLOCK), F32),
                        pltpu.VMEM((MOBA_HEADS, tq, MOBA_BLOCK // 2), F32)],
        compiler_params=pltpu.CompilerParams(dimension_semantics=("arbitrary", "arbitrary"),
                                             vmem_limit_bytes=V7X_VMEM_LIMIT_BYTES),
        name="moba",
    )(qa, ka, va, sza)


def _memattn_kernel(q_ref, mem_ref, wkv_ref, sz_ref, out_ref, kv_scr):
    j = pl.program_id(1)
    width = q_ref.shape[1]
    hd = width // MEM_HEADS

    @pl.when(j == 0)
    def _():
        kv_scr[...] = _dot(mem_ref[...].astype(BF16), wkv_ref[...]).astype(BF16)

    head_slices = [slice(h * hd, (h + 1) * hd) for h in range(MEM_HEADS)]
    scores = [_dot_nt(q_ref[:, hs], kv_scr[:, hs]) for hs in head_slices]
    probs = [jnp.exp(s - jnp.max(s, axis=1, keepdims=True)) for s in scores]
    outs = [_dot(p.astype(BF16), kv_scr[:, width + h * hd:width + (h + 1) * hd])
            for h, p in enumerate(probs)]
    for h, hs in enumerate(head_slices):
        l = jnp.sum(probs[h], axis=1, keepdims=True)
        out_ref[:, hs] = (outs[h] * (1.0 / l) * sz_ref[:, hs].astype(F32)).astype(BF16)


def _memattn_call(qc, mem, wkv, szc):
    bsz, s, w = qc.shape
    _, m_tok, d = mem.shape
    tq = TOKEN_TILE
    tok = pl.BlockSpec((None, tq, w), lambda b, j: (b, j, 0))
    return pl.pallas_call(
        _memattn_kernel,
        grid=(bsz, s // tq),
        in_specs=[tok, pl.BlockSpec((None, m_tok, d), lambda b, j: (b, 0, 0)),
                  pl.BlockSpec(wkv.shape, lambda b, j: (0, 0)), tok],
        out_specs=tok,
        out_shape=jax.ShapeDtypeStruct((bsz, s, w), BF16),
        scratch_shapes=[pltpu.VMEM((m_tok, 2 * w), BF16)],
        compiler_params=pltpu.CompilerParams(dimension_semantics=("arbitrary", "arbitrary"),
                                             vmem_limit_bytes=V7X_VMEM_LIMIT_BYTES),
        name="memattn",
    )(qc, mem, wkv, szc)


def _outproj_kernel(x_ref, om_ref, oa_ref, oc_ref, wo_ref, lng_ref, lnb_ref, y_ref):
    w_m = om_ref.shape[1]
    w_a = oa_ref.shape[1]
    mixed = (_dot(om_ref[...], wo_ref[0:w_m, :])
             + _dot(oa_ref[...], wo_ref[w_m:w_m + w_a, :])
             + _dot(oc_ref[...], wo_ref[w_m + w_a:, :]))
    y = DEEPNORM_ALPHA * x_ref[...] + mixed
    mu = jnp.mean(y, axis=1, keepdims=True)
    yc = y - mu
    var = jnp.mean(yc * yc, axis=1, keepdims=True)
    y_ref[...] = yc * lax.rsqrt(var + LN_EPS) * lng_ref[...] + lnb_ref[...]


def _outproj_call(x, om, oa, oc, wo, lng, lnb):
    bsz, s, d = x.shape
    tm = 2 * TOKEN_TILE
    tok = lambda w: pl.BlockSpec((None, tm, w), lambda b, t: (b, t, 0))
    res = lambda a: pl.BlockSpec(a.shape, lambda b, t: (0, 0))
    return pl.pallas_call(
        _outproj_kernel,
        grid=(bsz, s // tm),
        in_specs=[tok(d), tok(om.shape[2]), tok(oa.shape[2]), tok(oc.shape[2]), res(wo), res(lng), res(lnb)],
        out_specs=tok(d),
        out_shape=jax.ShapeDtypeStruct((bsz, s, d), x.dtype),
        compiler_params=pltpu.CompilerParams(dimension_semantics=("arbitrary", "arbitrary"),
                                             vmem_limit_bytes=V7X_VMEM_LIMIT_BYTES),
        name="outproj",
    )(x, om, oa, oc, wo, lng, lnb)


def _diag_tiles(w, heads):
    groups, blk, _ = w.shape
    hd = groups // heads * blk
    rows = w.reshape(heads, hd, blk)
    idx = jnp.arange(hd)
    same_group = (idx[:, None] // blk) == (idx[None, :] // blk)
    return jnp.where(same_group, jnp.tile(rows, (1, 1, hd // blk)), 0.0)


def kernel(x, mem, positions, w_in, mlstm_conv_w, mlstm_conv_b, mlstm_wq, mlstm_wk, mlstm_wv, mlstm_w_gates, mlstm_b_gates, mlstm_norm_g, mlstm_skip, w_mem_kv, w_out, ln_g, ln_b):
    bsz, s, d = x.shape
    ml_width = mlstm_conv_w.shape[1]
    ml_heads = mlstm_b_gates.shape[0] // 2
    mem_width = w_mem_kv.shape[1] // 2
    moba_width = (w_in.shape[1] - 2 * ml_width - 2 * mem_width) // 4
    moba_hd = moba_width // MOBA_HEADS
    assert moba_hd == V7X_LANES and mem_width // MEM_HEADS == V7X_LANES
    assert s % (2 * TOKEN_TILE) == 0

    row = lambda a: a.reshape(1, -1).astype(F32)
    wq = _diag_tiles(mlstm_wq, ml_heads).astype(BF16)
    wk = _diag_tiles(mlstm_wk, ml_heads).astype(BF16)
    wv = _diag_tiles(mlstm_wv, ml_heads).astype(BF16)
    n_gates = mlstm_w_gates.shape[1]
    wg = jnp.pad(mlstm_w_gates, ((0, 0), (0, V7X_LANES - n_gates))).astype(BF16)
    bg = jnp.pad(mlstm_b_gates, (0, V7X_LANES - n_gates)).reshape(1, -1).astype(F32)
    half = moba_hd // 2
    inv_freq = ROPE_THETA ** (-jnp.arange(half, dtype=F32) * 2.0 / moba_hd)
    invf = jnp.concatenate([inv_freq, inv_freq]).reshape(1, -1)
    sgn = jnp.concatenate([-jnp.ones((half,), F32), jnp.ones((half,), F32)]).reshape(1, -1)
    pos = positions.astype(F32).reshape(bsz, s, 1)

    (qm, kmt, vm, add, mul, gcol, grow, qa, ka, va, sza, qc, szc) = _proj_call(
        x, pos, w_in.astype(BF16), mlstm_conv_w.astype(F32), row(mlstm_conv_b), wq, wk, wv, wg, bg,
        row(mlstm_norm_g), row(mlstm_skip), invf, sgn,
        ml_width=ml_width, ml_heads=ml_heads, moba_width=moba_width, mem_width=mem_width)
    out_m = _mlstm_call(qm, kmt, vm, gcol, grow, add, mul, heads=ml_heads)
    out_a = _moba_call(qa, ka, va, sza)
    out_c = _memattn_call(qc, mem, w_mem_kv.astype(BF16), szc)
    return _outproj_call(x, out_m, out_a, out_c, w_out.astype(BF16), row(ln_g), row(ln_b))
```

```python
import functools
import math

import jax
import jax.numpy as jnp
from jax import lax
from jax.experimental import pallas as pl
from jax.experimental.pallas import tpu as pltpu

MOBA_HEADS = 4
MOBA_BLOCK = 256
MOBA_TOP_K = 3
MEM_HEADS = 4
ROPE_THETA = 10000.0
DEPTH = 1
DEEPNORM_ALPHA = (2 * DEPTH) ** 0.25
LN_EPS = 1e-5

V7X_LANES = 128
V7X_SUBLANES = 8
V7X_VMEM_LIMIT_BYTES = 56 * 1024 * 1024

TOKEN_TILE = MOBA_BLOCK
MASK_VALUE = -1e30
GATE_ROWS = 2 * V7X_SUBLANES

F32 = jnp.float32
BF16 = jnp.bfloat16


def _dot(a, b):
    return jnp.dot(a, b, preferred_element_type=F32)


def _dot_nt(a, b):
    return lax.dot_general(a, b, (((1,), (1,)), ((), ())), preferred_element_type=F32)


def _split3(v):
    hi = v.astype(BF16)
    r1 = v - hi.astype(F32)
    mid = r1.astype(BF16)
    lo = (r1 - mid.astype(F32)).astype(BF16)
    return hi, mid, lo


def _lane_scan(v, op, identity, lane_idx):
    shift = 1
    while shift < v.shape[-1]:
        v = op(v, jnp.where(lane_idx >= shift, pltpu.roll(v, shift, v.ndim - 1), identity))
        shift *= 2
    return v


def _silu(v):
    return v * jax.nn.sigmoid(v)


def _log_sigmoid(v):
    return jnp.minimum(v, 0.0) - jnp.log1p(jnp.exp(-jnp.abs(v)))


def _proj_kernel(x_ref, pos_ref, win_ref, convw_ref, convb_ref, wq_ref, wk_ref, wv_ref,
                 wg_ref, bg_ref, normg_ref, skip_ref, invf_ref, sgn_ref,
                 qm_ref, kmt_ref, vm_ref, add_ref, mul_ref, gcol_ref, grow_ref,
                 qa_ref, ka_ref, va_ref, sza_ref, qc_ref, szc_ref,
                 xpad_scr, kmean_scr,
                 *, ml_width, ml_heads, moba_width, mem_width, n_sel, gate_slots):
    t = pl.program_id(1)
    tm = x_ref.shape[0]
    hd = ml_width // ml_heads
    moba_hd = moba_width // MOBA_HEADS
    mem_hd = mem_width // MEM_HEADS
    c_zm = ml_width
    c_qa = 2 * ml_width
    c_ka = c_qa + moba_width
    c_va = c_ka + moba_width
    c_za = c_va + moba_width
    c_qc = c_za + moba_width
    c_zc = c_qc + mem_width

    xb = x_ref[...].astype(BF16)
    pad = xpad_scr.shape[0] - tm
    k_w = convw_ref.shape[0]
    nb_lanes = gate_slots

    @pl.when(t == 0)
    def _():
        xpad_scr[0:pad, :] = jnp.zeros((pad, ml_width), F32)
        kmean_scr[...] = jnp.zeros(kmean_scr.shape, F32)

    @pl.when(t > 0)
    def _():
        xpad_scr[0:pad, :] = xpad_scr[tm:tm + pad, :]

    def x_cols(c0, width):
        return _dot(xb, win_ref[:, c0:c0 + width])

    def rotary_pair(p, qp, kp):
        rots, means = [], []
        for i in range(2):
            h = 2 * p + i
            ls = slice(i * moba_hd, (i + 1) * moba_hd)
            q_rot = qp[:, ls] * cosf + pltpu.roll(qp[:, ls], moba_hd // 2, 1) * sins
            k_rot = kp[:, ls] * cosf + pltpu.roll(kp[:, ls], moba_hd // 2, 1) * sins
            qa_ref[:, 2 * h * moba_hd:(2 * h + 1) * moba_hd] = (q_rot * (moba_hd ** -0.5)).astype(BF16)
            ka_ref[:, h * moba_hd:(h + 1) * moba_hd] = k_rot.astype(BF16)
            rots.append(q_rot)
            means.append(jnp.mean(k_rot, axis=0, keepdims=True))
        return rots, means

    def conv_gate_head(h, xm_h, zm_h):
        hs = slice(h * hd, (h + 1) * hd)
        xpad_scr[pad:pad + tm, hs] = xm_h
        conv = convb_ref[:, hs] + xm_h * convw_ref[k_w - 1:k_w, hs]
        for j in range(k_w - 1):
            conv = conv + xpad_scr[pl.ds(pad - (k_w - 1) + j, tm), hs] * convw_ref[j:j + 1, hs]
        xc_h = _silu(conv)
        sz_h = _silu(zm_h)
        add_ref[:, hs] = (skip_ref[:, hs] * xc_h * sz_h).astype(BF16)
        mul_ref[:, hs] = (normg_ref[:, hs] * sz_h).astype(BF16)
        return xc_h.astype(BF16), xm_h.astype(BF16)

    def blockdiag_head(h, xc_b, xm_b):
        hs = slice(h * hd, (h + 1) * hd)
        q_h = _dot(xc_b, wq_ref[h]).astype(BF16)
        k_f = _dot(xc_b, wk_ref[h])
        v_h = _dot(xm_b, wv_ref[h]).astype(BF16)
        qm_ref[:, hs] = q_h
        vm_ref[:, hs] = v_h
        kmt_ref[hs, :] = (jnp.transpose(k_f) * (hd ** -0.5)).astype(BF16)
        return q_h, k_f.astype(BF16), v_h

    pw = 2 * moba_hd
    qa0, ka0 = x_cols(c_qa, pw), x_cols(c_ka, pw)
    ang = pos_ref[...] * invf_ref[...]
    cosf = jnp.cos(ang)
    sins = jnp.sin(ang) * sgn_ref[...]
    qa1, ka1 = x_cols(c_qa + pw, pw), x_cols(c_ka + pw, pw)
    rots0, means0 = rotary_pair(0, qa0, ka0)
    xm0, zm0 = x_cols(0, hd), x_cols(c_zm, hd)
    rots1, means1 = rotary_pair(1, qa1, ka1)
    q_rots, kmean_rows = rots0 + rots1, means0 + means1
    q_all = jnp.concatenate(q_rots, axis=1)
    km = kmean_scr[...]
    q_hi = q_all.astype(BF16)
    q_lo = (q_all - q_hi.astype(F32)).astype(BF16)
    k_hi = km.astype(BF16)
    k_lo = (km - k_hi.astype(F32)).astype(BF16)
    xz = {0: (xm0, zm0), 1: (x_cols(hd, hd), x_cols(c_zm + hd, hd))}
    gate = _dot_nt(q_hi, k_hi) + _dot_nt(q_lo, k_hi) + _dot_nt(q_hi, k_lo)
    pending = [("xz", h) for h in range(2, ml_heads)] + [("va", None), ("za", None)]
    def gate_terms(h, q_h, k_h, v_h):
        hs = slice(h * hd, (h + 1) * hd)
        return (_dot(q_h, wg_ref[hs, :])
                + _dot(k_h, wg_ref[ml_width + h * hd:ml_width + (h + 1) * hd, :])
                + _dot(v_h, wg_ref[2 * ml_width + h * hd:2 * ml_width + (h + 1) * hd, :]))

    def select_blocks(gate):
        lane = lax.broadcasted_iota(jnp.int32, (tm, V7X_LANES), 1)
        n_l = lane & (nb_lanes - 1)
        valid = (n_l < t) & (lane < MOBA_HEADS * nb_lanes)
        gate = jnp.where(valid, gate, -jnp.inf)
        cnt = jnp.zeros((tm, V7X_LANES), jnp.int32)
        for r in range(1, nb_lanes):
            up = pltpu.roll(gate, r, 1)
            cnt = cnt + ((n_l >= r) & (up >= gate)).astype(jnp.int32)
            dn = pltpu.roll(gate, V7X_LANES - r, 1)
            cnt = cnt + ((n_l < nb_lanes - r) & (dn > gate)).astype(jnp.int32)
        keep = (valid & (cnt < n_sel)) | ((n_l == t) & (lane < MOBA_HEADS * nb_lanes))
        selb = jnp.where(keep, 0.0, MASK_VALUE)
        for h in range(MOBA_HEADS):
            shift = (V7X_LANES - h * nb_lanes) % V7X_LANES
            sel_h = selb if shift == 0 else pltpu.roll(selb, shift, 1)
            qa_ref[:, (2 * h + 1) * moba_hd:(2 * h + 2) * moba_hd] = sel_h.astype(BF16)

    qkv = []
    va = za = None
    g = jnp.zeros((tm, V7X_LANES), F32) + bg_ref[...]
    for h in range(ml_heads):
        xc_b, xm_b = conv_gate_head(h, *xz[h])
        if pending:
            kind, arg = pending.pop(0)
            if kind == "xz":
                xz[arg] = (x_cols(arg * hd, hd), x_cols(c_zm + arg * hd, hd))
            elif kind == "va":
                va = x_cols(c_va, moba_width).astype(BF16)
            else:
                za = x_cols(c_za, moba_width)
        qkv.append(blockdiag_head(h, xc_b, xm_b))
    if va is None:
        va = x_cols(c_va, moba_width).astype(BF16)
    if za is None:
        za = x_cols(c_za, moba_width)
    for h in range(ml_heads):
        g = g + gate_terms(h, *qkv[h])
    sza_ref[...] = _silu(za).astype(BF16)
    qc = x_cols(c_qc, mem_width)
    zc = x_cols(c_zc, mem_width)

    nrow = grow_ref.shape[0]
    gt = jnp.transpose(g)[0:V7X_SUBLANES, :]
    sub = lax.broadcasted_iota(jnp.int32, gt.shape, 0)
    tok_i = lax.broadcasted_iota(jnp.int32, gt.shape, 1)
    b = _lane_scan(jnp.where(sub >= ml_heads, _log_sigmoid(gt), 0.0), jnp.add, 0.0, tok_i)
    a = _lane_scan(jnp.where(sub >= ml_heads, pltpu.roll(gt, ml_heads, 0) - b, -jnp.inf),
                   jnp.maximum, -jnp.inf, tok_i)
    rows =jnp.concatenate([jnp.where(sub < ml_heads, gt, b), pltpu.roll(a, ml_heads, 0)], axis=0)
    grow_ref[...] = rows
    gcol_ref[...] = jnp.transpose(
        jnp.concatenate([rows, jnp.zeros((V7X_LANES - nrow, tm), F32)], axis=0))

    km_row = lax.broadcasted_iota(jnp.int32, kmean_scr.shape, 0)
    km_head = lax.broadcasted_iota(jnp.int32, kmean_scr.shape, 1) // moba_hd
    kmean_scr[...] = jnp.where(km_row == km_head * nb_lanes + t,
                               jnp.concatenate(kmean_rows, axis=1), kmean_scr[...])
    select_blocks(gate)
    va_ref[...] = va
    qc_ref[...] = (qc * (mem_hd ** -0.5)).astype(BF16)
    szc_ref[...] = _silu(zc).astype(BF16)


def _resident(shape):
    nd = len(shape)
    return pl.BlockSpec(shape, lambda b, t, _nd=nd: (0,) * _nd)


def _proj_call(x, pos, win, convw, convb, wq, wk, wv, wg, bg, normg, skip, invf, sgn,
               *, ml_width, ml_heads, moba_width, mem_width):
    bsz, s, d = x.shape
    tm = TOKEN_TILE
    nt = s // tm
    n_sel = min(MOBA_TOP_K, nt - 1)
    gate_slots = max(V7X_SUBLANES, pl.next_power_of_2(nt))
    assert MOBA_HEADS * gate_slots <= V7X_LANES and s % tm == 0
    tok = lambda w: pl.BlockSpec((None, tm, w), lambda b, t: (b, t, 0))
    out_shapes = (
        jax.ShapeDtypeStruct((bsz, s, ml_width), BF16),
        jax.ShapeDtypeStruct((bsz, ml_width, s), BF16),
        jax.ShapeDtypeStruct((bsz, s, ml_width), BF16),
        jax.ShapeDtypeStruct((bsz, s, ml_width), BF16),
        jax.ShapeDtypeStruct((bsz, s, ml_width), BF16),
        jax.ShapeDtypeStruct((bsz, s, V7X_LANES), F32),
        jax.ShapeDtypeStruct((bsz, GATE_ROWS, s), F32),
        jax.ShapeDtypeStruct((bsz, s, 2 * moba_width), BF16),
        jax.ShapeDtypeStruct((bsz, s, 2 * moba_width), BF16),
        jax.ShapeDtypeStruct((bsz, s, 2 * moba_width), BF16),
        jax.ShapeDtypeStruct((bsz, s, moba_width), BF16),
        jax.ShapeDtypeStruct((bsz, s, mem_width), BF16),
        jax.ShapeDtypeStruct((bsz, s, mem_width), BF16),
    )
    out_specs = (
        tok(ml_width),
        pl.BlockSpec((None, ml_width, tm), lambda b, t: (b, 0, t)),
        tok(ml_width), tok(ml_width), tok(ml_width),
        tok(V7X_LANES),
        pl.BlockSpec((None, GATE_ROWS, tm), lambda b, t: (b, 0, t)),
        tok(2 * moba_width), tok(2 * moba_width), tok(2 * moba_width), tok(moba_width),
        tok(mem_width), tok(mem_width),
    )
    in_specs = [tok(d), tok(1)] + [_resident(a.shape) for a in
                                   (win, convw, convb, wq, wk, wv, wg, bg, normg, skip, invf, sgn)]
    kern = functools.partial(_proj_kernel, ml_width=ml_width, ml_heads=ml_heads,
                             moba_width=moba_width, mem_width=mem_width, n_sel=n_sel,
                             gate_slots=gate_slots)
    return pl.pallas_call(
        kern,
        grid=(bsz, nt),
        in_specs=in_specs,
        out_specs=out_specs,
        out_shape=out_shapes,
        scratch_shapes=[pltpu.VMEM((tm + V7X_SUBLANES, ml_width), F32),
                        pltpu.VMEM((V7X_LANES, moba_width), F32)],
        compiler_params=pltpu.CompilerParams(dimension_semantics=("arbitrary", "arbitrary"),
                                             vmem_limit_bytes=V7X_VMEM_LIMIT_BYTES),
        name="proj",
    )(x, pos, win, convw, convb, wq, wk, wv, wg, bg, normg, skip, invf, sgn)


def _mlstm_kernel(qm_ref, kmt_ref, vm_ref, gcol_ref, grow_ref, add_ref, mul_ref, out_ref,
                  c_scr, n_scr, m_scr, *, heads):
    c = pl.program_id(1)
    L = qm_ref.shape[0]
    hd = qm_ref.shape[1] // heads

    @pl.when(c == 0)
    def _():
        c_scr[...] = jnp.zeros(c_scr.shape, F32)
        n_scr[...] = jnp.zeros(n_scr.shape, F32)
        m_scr[...] = jnp.zeros(m_scr.shape, F32)

    row_i = lax.broadcasted_iota(jnp.int32, (L, L), 0)
    col_i = lax.broadcasted_iota(jnp.int32, (L, L), 1)
    causal = row_i >= col_i
    gcol = gcol_ref[...]
    grow = grow_ref[...]
    ones = jnp.ones((L, V7X_LANES), BF16)
    head_slices = [slice(h * hd, (h + 1) * hd) for h in range(heads)]

    qk, inter, qn, mm_l, r_l, w_inter_l = [], [], [], [], [], []
    for h, hs in enumerate(head_slices):
        q = qm_ref[:, hs]
        kt = kmt_ref[hs, :]
        v = vm_ref[:, hs]
        i_row = grow[h:h + 1, :]
        b_row = grow[heads + h:heads + h + 1, :]
        a_col = gcol[:, 2 * heads + h:2 * heads + h + 1]
        m_prev = m_scr[h][0:1, 0:1]
        c_prev = c_scr[h]
        n_prev = n_scr[h]
        qk.append(_dot(q, kt))
        inter.append(_dot(q, c_prev.astype(BF16)))
        qn.append(_dot(q, n_prev.astype(BF16)))
        r_row = i_row - b_row
        mm = jnp.maximum(m_prev, a_col)
        r_l.append(r_row)
        mm_l.append(mm)
        w_inter_l.append(jnp.exp(m_prev - mm))
        b_end = b_row[:, L - 1:L]
        log_w = b_end + r_row
        m_new = jnp.maximum(b_end + m_prev, jnp.max(log_w, axis=1, keepdims=True))
        decay = jnp.exp(b_end + m_prev - m_new)
        kw = (kt.astype(F32) * jnp.exp(log_w - m_new)).astype(BF16)
        c_scr[h] = decay * c_prev + _dot(kw, v)
        n_scr[h] = decay * n_prev + _dot(kw, ones)
        m_scr[h] = jnp.broadcast_to(m_new, m_scr.shape[1:])

    s_qk = [(qk[h] * jnp.exp(jnp.where(causal, r_l[h] - mm_l[h], -jnp.inf))).astype(BF16)
            for h in range(heads)]
    pv = [_dot(s_qk[h], vm_ref[:, hs]) for h, hs in enumerate(head_slices)]
    rs = [_dot(s_qk[h], ones) for h in range(heads)]

    for h, hs in enumerate(head_slices):
        b_col = gcol[:, heads + h:heads + h + 1]
        num = w_inter_l[h] * inter[h] + pv[h]
        den = w_inter_l[h] * qn[h] + rs[h]
        rec = 1.0 / jnp.maximum(jnp.abs(den), jnp.exp(-(b_col + mm_l[h])))
        hh = num * jnp.concatenate([rec] * (hd // V7X_LANES), axis=1)
        mu = jnp.mean(hh, axis=1, keepdims=True)
        hc = hh - mu
        var = jnp.mean(hc * hc, axis=1, keepdims=True)
        hn = hc * lax.rsqrt(var + LN_EPS)
        out_ref[:, hs] = (hn * mul_ref[:, hs].astype(F32) + add_ref[:, hs].astype(F32)).astype(BF16)


def _mlstm_call(qm, kmt, vm, gcol, grow, add, mul, *, heads):
    bsz, s, w = qm.shape
    L = TOKEN_TILE
    hd = w // heads
    tok = lambda width: pl.BlockSpec((None, L, width), lambda b, c: (b, c, 0))
    tmaj = lambda rows: pl.BlockSpec((None, rows, L), lambda b, c: (b, 0, c))
    return pl.pallas_call(
        functools.partial(_mlstm_kernel, heads=heads),
        grid=(bsz, s // L),
        in_specs=[tok(w), tmaj(w), tok(w), tok(V7X_LANES), tmaj(GATE_ROWS), tok(w), tok(w)],
        out_specs=tok(w),
        out_shape=jax.ShapeDtypeStruct((bsz, s, w), BF16),
        scratch_shapes=[pltpu.VMEM((heads, hd, hd), F32),
                        pltpu.VMEM((heads, hd, V7X_LANES), F32),
                        pltpu.VMEM((heads, V7X_SUBLANES, V7X_LANES), F32)],
        compiler_params=pltpu.CompilerParams(dimension_semantics=("arbitrary", "arbitrary"),
                                             vmem_limit_bytes=V7X_VMEM_LIMIT_BYTES),
        name="mlstm",
    )(qm, kmt, vm, gcol, grow, add, mul)


def _moba_kernel(q_ref, k_ref, v_ref, sz_ref, out_ref, s_scr, m_scr):
    j = pl.program_id(1)
    tq = q_ref.shape[0]
    bs = MOBA_BLOCK
    nblk = k_ref.shape[0] // bs
    aw = q_ref.shape[1] // MOBA_HEADS
    hd = aw // 2
    half = bs // 2
    heads = [slice(h * aw, (h + 1) * aw) for h in range(MOBA_HEADS)]
    kv_heads = [slice(h * hd, (h + 1) * hd) for h in range(MOBA_HEADS)]
    tag_lane = lax.broadcasted_iota(jnp.int32, (bs, hd), 1)
    ones_col = (tag_lane == 0).astype(BF16)

    def attend(own):
        causal = (lax.broadcasted_iota(jnp.int32, (tq, bs), 0)
                  >= lax.broadcasted_iota(jnp.int32, (tq, bs), 1))
        for hs_i, hs in enumerate(heads):
            mx = None
            for n in range(own + 1):
                k_aug = jnp.concatenate([k_ref[n * bs:(n + 1) * bs, kv_heads[hs_i]],
                                         (tag_lane == n).astype(BF16)], axis=1)
                s = _dot_nt(q_ref[:, hs], k_aug)
                if n == own:
                    s = jnp.where(causal, s, MASK_VALUE)
                s_scr[hs_i, n] = s
                sm = jnp.maximum(s[:, :half], s[:, half:])
                mx = sm if mx is None else jnp.maximum(mx, sm)
            m_scr[hs_i] = jnp.broadcast_to(jnp.max(mx, axis=1, keepdims=True), m_scr.shape[1:])
        for hs_i, hs in enumerate(heads):
            m = m_scr[hs_i]
            acc = None
            for n in range(own + 1):
                s = s_scr[hs_i, n]
                p = jnp.concatenate([jnp.exp(s[:, :half] - m), jnp.exp(s[:, half:] - m)], axis=1)
                v_aug = jnp.concatenate([v_ref[n * bs:(n + 1) * bs, kv_heads[hs_i]], ones_col], axis=1)
                pv = _dot(p.astype(BF16), v_aug)
                acc = pv if acc is None else acc + pv
            o = acc[:, :hd] * (1.0 / acc[:, hd:hd + 1])
            out_ref[:, hs_i * hd:(hs_i + 1) * hd] = (
                o * sz_ref[:, hs_i * hd:(hs_i + 1) * hd].astype(F32)).astype(BF16)

    for own in range(nblk):
        pl.when(j == own)(functools.partial(attend, own))


def _moba_call(qa, ka, va, sza):
    bsz, s, aw_all = qa.shape
    w = sza.shape[2]
    tq = TOKEN_TILE
    assert tq == MOBA_BLOCK and aw_all // MOBA_HEADS == MOBA_BLOCK
    nblk = s // MOBA_BLOCK
    tok = lambda width: pl.BlockSpec((None, tq, width), lambda b, j: (b, j, 0))
    kvspec = pl.BlockSpec((None, s, aw_all), lambda b, j: (b, 0, 0))
    return pl.pallas_call(
        _moba_kernel,
        grid=(bsz, s // tq),
        in_specs=[tok(aw_all), kvspec, kvspec, tok(w)],
        out_specs=tok(w),
        out_shape=jax.ShapeDtypeStruct((bsz, s, w), BF16),
        scratch_shapes=[pltpu.VMEM((MOBA_HEADS, nblk, tq, MOBA_BLOCK), F32),
                        pltpu.VMEM((MOBA_HEADS, tq, MOBA_BLOCK // 2), F32)],
        compiler_params=pltpu.CompilerParams(dimension_semantics=("arbitrary", "arbitrary"),
                                             vmem_limit_bytes=V7X_VMEM_LIMIT_BYTES),
        name="moba",
    )(qa, ka, va, sza)


def _memattn_kernel(q_ref, mem_ref, wkv_ref, sz_ref, out_ref, kv_scr):
    j = pl.program_id(1)
    width = q_ref.shape[1]
    hd = width // MEM_HEADS

    @pl.when(j == 0)
    def _():
        kv_scr[...] = _dot(mem_ref[...].astype(BF16), wkv_ref[...]).astype(BF16)

    head_slices = [slice(h * hd, (h + 1) * hd) for h in range(MEM_HEADS)]
    scores = [_dot_nt(q_ref[:, hs], kv_scr[:, hs]) for hs in head_slices]
    probs = [jnp.exp(s - jnp.max(s, axis=1, keepdims=True)) for s in scores]
    outs = [_dot(p.astype(BF16), kv_scr[:, width + h * hd:width + (h + 1) * hd])
            for h, p in enumerate(probs)]
    for h, hs in enumerate(head_slices):
        l = jnp.sum(probs[h], axis=1, keepdims=True)
        out_ref[:, hs] = (outs[h] * (1.0 / l) * sz_ref[:, hs].astype(F32)).astype(BF16)


def _memattn_call(qc, mem, wkv, szc):
    bsz, s, w = qc.shape
    _, m_tok, d = mem.shape
    tq = TOKEN_TILE
    tok = pl.BlockSpec((None, tq, w), lambda b, j: (b, j, 0))
    return pl.pallas_call(
        _memattn_kernel,
        grid=(bsz, s // tq),
        in_specs=[tok, pl.BlockSpec((None, m_tok, d), lambda b, j: (b, 0, 0)),
                  pl.BlockSpec(wkv.shape, lambda b, j: (0, 0)), tok],
        out_specs=tok,
        out_shape=jax.ShapeDtypeStruct((bsz, s, w), BF16),
        scratch_shapes=[pltpu.VMEM((m_tok, 2 * w), BF16)],
        compiler_params=pltpu.CompilerParams(dimension_semantics=("arbitrary", "arbitrary"),
                                             vmem_limit_bytes=V7X_VMEM_LIMIT_BYTES),
        name="memattn",
    )(qc, mem, wkv, szc)


def _outproj_kernel(x_ref, om_ref, oa_ref, oc_ref, wo_ref, lng_ref, lnb_ref, y_ref):
    w_m = om_ref.shape[1]
    w_a = oa_ref.shape[1]
    mixed = (_dot(om_ref[...], wo_ref[0:w_m, :])
             + _dot(oa_ref[...], wo_ref[w_m:w_m + w_a, :])
             + _dot(oc_ref[...], wo_ref[w_m + w_a:, :]))
    y = DEEPNORM_ALPHA * x_ref[...] + mixed
    mu = jnp.mean(y, axis=1, keepdims=True)
    yc = y - mu
    var = jnp.mean(yc * yc, axis=1, keepdims=True)
    y_ref[...] = yc * lax.rsqrt(var + LN_EPS) * lng_ref[...] + lnb_ref[...]


def _outproj_call(x, om, oa, oc, wo, lng, lnb):
    bsz, s, d = x.shape
    tm = 2 * TOKEN_TILE
    tok = lambda w: pl.BlockSpec((None, tm, w), lambda b, t: (b, t, 0))
    res = lambda a: pl.BlockSpec(a.shape, lambda b, t: (0, 0))
    return pl.pallas_call(
        _outproj_kernel,
        grid=(bsz, s // tm),
        in_specs=[tok(d), tok(om.shape[2]), tok(oa.shape[2]), tok(oc.shape[2]), res(wo), res(lng), res(lnb)],
        out_specs=tok(d),
        out_shape=jax.ShapeDtypeStruct((bsz, s, d), x.dtype),
        compiler_params=pltpu.CompilerParams(dimension_semantics=("arbitrary", "arbitrary"),
                                             vmem_limit_bytes=V7X_VMEM_LIMIT_BYTES),
        name="outproj",
    )(x, om, oa, oc, wo, lng, lnb)


def _layer_kernel(x_ref, pos_ref, mem_ref, win_ref, convw_ref, convb_ref, wq_ref, wk_ref, wv_ref,
                  wg_ref, bg_ref, normg_ref, skip_ref, invf_ref, sgn_ref, wkv_ref, wo_ref, lng_ref, lnb_ref,
                  y_ref,
                  qm_s, kmt_s, vm_s, add_s, mul_s, gcol_s, grow_s, qa_s, ka_s, va_s, sza_s, qc_s, szc_s,
                  om_s, oa_s, oc_s, xpad_s, kmean_s, c_s, n_s, m_s, score_s, rowmax_s, kv_s,
                  *, ml_width, ml_heads, moba_width, mem_width, n_sel, gate_slots):
    t = pl.program_id(1)
    tm = x_ref.shape[0]
    rows = pl.ds(pl.multiple_of(t * tm, tm), tm)
    _proj_kernel(x_ref, pos_ref, win_ref, convw_ref, convb_ref, wq_ref, wk_ref, wv_ref,
                 wg_ref, bg_ref, normg_ref, skip_ref, invf_ref, sgn_ref,
                 qm_s, kmt_s, vm_s, add_s, mul_s, gcol_s, grow_s,
                 qa_s, ka_s.at[rows], va_s.at[rows], sza_s, qc_s, szc_s, xpad_s, kmean_s,
                 ml_width=ml_width, ml_heads=ml_heads, moba_width=moba_width, mem_width=mem_width,
                 n_sel=n_sel, gate_slots=gate_slots)
    _mlstm_kernel(qm_s, kmt_s, vm_s, gcol_s, grow_s, add_s, mul_s, om_s, c_s, n_s, m_s, heads=ml_heads)
    _moba_kernel(qa_s, ka_s, va_s, sza_s, oa_s, score_s, rowmax_s)
    _memattn_kernel(qc_s, mem_ref, wkv_ref, szc_s, oc_s, kv_s)
    _outproj_kernel(x_ref, om_s, oa_s, oc_s, wo_ref, lng_ref, lnb_ref, y_ref)


def _layer_call(x, pos, mem, win, convw, convb, wq, wk, wv, wg, bg, normg, skip, invf, sgn, wkv, wo, lng, lnb,
                *, ml_width, ml_heads, moba_width, mem_width):
    bsz, s, d = x.shape
    m_tok = mem.shape[1]
    tm = TOKEN_TILE
    nt = s // tm
    n_sel = min(MOBA_TOP_K, nt - 1)
    gate_slots = max(V7X_SUBLANES, pl.next_power_of_2(nt))
    assert MOBA_HEADS * gate_slots <= V7X_LANES and s % tm == 0 and ml_heads >= 2
    hd = ml_width // ml_heads
    tok = lambda w: pl.BlockSpec((None, tm, w), lambda b, t: (b, t, 0))

    def resident(a):
        nd = a.ndim
        return pl.BlockSpec(a.shape, lambda b, t, _nd=nd: (0,) * _nd, pipeline_mode=pl.Buffered(1))

    weights = (win, convw, convb, wq, wk, wv, wg, bg, normg, skip, invf, sgn, wkv, wo, lng, lnb)
    in_specs = ([tok(d), tok(1), pl.BlockSpec((None, m_tok, d), lambda b, t: (b, 0, 0))]
                + [resident(a) for a in weights])
    vmem = pltpu.VMEM
    scratch = [
        vmem((tm, ml_width), BF16),
        vmem((ml_width, tm), BF16),
        vmem((tm, ml_width), BF16),
        vmem((tm, ml_width), BF16),
        vmem((tm, ml_width), BF16),
        vmem((tm, V7X_LANES), F32),
        vmem((GATE_ROWS, tm), F32),
        vmem((tm, 2 * moba_width), BF16),
        vmem((s, moba_width), BF16),
        vmem((s, moba_width), BF16),
        vmem((tm, moba_width), BF16),
        vmem((tm, mem_width), BF16),
        vmem((tm, mem_width), BF16),
        vmem((tm, ml_width), BF16),
        vmem((tm, moba_width), BF16),
        vmem((tm, mem_width), BF16),
        vmem((tm + V7X_SUBLANES, ml_width), F32),
        vmem((V7X_LANES, moba_width), F32),
        vmem((ml_heads, hd, hd), F32),
        vmem((ml_heads, hd, V7X_LANES), F32),
        vmem((ml_heads, V7X_SUBLANES, V7X_LANES), F32),
        vmem((MOBA_HEADS, nt, tm, MOBA_BLOCK), F32),
        vmem((MOBA_HEADS, tm, MOBA_BLOCK // 2), F32),
        vmem((m_tok, 2 * mem_width), BF16),
    ]
    kern = functools.partial(_layer_kernel, ml_width=ml_width, ml_heads=ml_heads, moba_width=moba_width,
                             mem_width=mem_width, n_sel=n_sel, gate_slots=gate_slots)
    return pl.pallas_call(
        kern,
        grid=(bsz, nt),
        in_specs=in_specs,
        out_specs=tok(d),
        out_shape=jax.ShapeDtypeStruct((bsz, s, d), x.dtype),
        scratch_shapes=scratch,
        compiler_params=pltpu.CompilerParams(dimension_semantics=("arbitrary", "arbitrary"),
                                             vmem_limit_bytes=V7X_VMEM_LIMIT_BYTES),
        name="layer",
    )(x, pos, mem, *weights)


def _diag_tiles(w, heads):
    groups, blk, _ = w.shape
    hd = groups // heads * blk
    rows = w.reshape(heads, hd, blk)
    idx = jnp.arange(hd)
    same_group = (idx[:, None] // blk) == (idx[None, :] // blk)
    return jnp.where(same_group, jnp.tile(rows, (1, 1, hd // blk)), 0.0)


def kernel(x, mem, positions, w_in, mlstm_conv_w, mlstm_conv_b, mlstm_wq, mlstm_wk, mlstm_wv, mlstm_w_gates, mlstm_b_gates, mlstm_norm_g, mlstm_skip, w_mem_kv, w_out, ln_g, ln_b):
    bsz, s, d = x.shape
    ml_width = mlstm_conv_w.shape[1]
    ml_heads = mlstm_b_gates.shape[0] // 2
    mem_width = w_mem_kv.shape[1] // 2
    moba_width = (w_in.shape[1] - 2 * ml_width - 2 * mem_width) // 4
    moba_hd = moba_width // MOBA_HEADS
    assert moba_hd == V7X_LANES and mem_width // MEM_HEADS == V7X_LANES
    assert s % (2 * TOKEN_TILE) == 0

    row = lambda a: a.reshape(1, -1).astype(F32)
    wq = _diag_tiles(mlstm_wq, ml_heads).astype(BF16)
    wk = _diag_tiles(mlstm_wk, ml_heads).astype(BF16)
    wv = _diag_tiles(mlstm_wv, ml_heads).astype(BF16)
    n_gates = mlstm_w_gates.shape[1]
    wg = jnp.pad(mlstm_w_gates, ((0, 0), (0, V7X_LANES - n_gates))).astype(BF16)
    bg = jnp.pad(mlstm_b_gates, (0, V7X_LANES - n_gates)).reshape(1, -1).astype(F32)
    half = moba_hd // 2
    inv_freq = ROPE_THETA ** (-jnp.arange(half, dtype=F32) * 2.0 / moba_hd)
    invf = jnp.concatenate([inv_freq, inv_freq]).reshape(1, -1)
    sgn = jnp.concatenate([-jnp.ones((half,), F32), jnp.ones((half,), F32)]).reshape(1, -1)
    pos = positions.astype(F32).reshape(bsz, s, 1)

    return _layer_call(
        x, pos, mem, w_in.astype(BF16), mlstm_conv_w.astype(F32), row(mlstm_conv_b), wq, wk, wv, wg, bg,
        row(mlstm_norm_g), row(mlstm_skip), invf, sgn, w_mem_kv.astype(BF16), w_out.astype(BF16),
        row(ln_g), row(ln_b),
        ml_width=ml_width, ml_heads=ml_heads, moba_width=moba_width, mem_width=mem_width)
```

```python
import functools

import jax
import jax.numpy as jnp
from jax import lax
from jax.experimental import pallas as pl
from jax.experimental.pallas import tpu as pltpu

MOBA_HEADS = 4
MOBA_BLOCK = 256
MOBA_TOP_K = 3
MEM_HEADS = 4
ROPE_THETA = 10000.0
DEPTH = 1
DEEPNORM_ALPHA = (2 * DEPTH) ** 0.25
LN_EPS = 1e-5

V7X_LANES = 128
V7X_SUBLANES = 8
V7X_VMEM_LIMIT_BYTES = 56 * 1024 * 1024

TOKEN_TILE = MOBA_BLOCK
MASK_VALUE = -1e30
GATE_ROWS = 2 * V7X_SUBLANES

F32 = jnp.float32
BF16 = jnp.bfloat16


def _dot(a, b):
    return jnp.dot(a, b, preferred_element_type=F32)


def _dot_nt(a, b):
    return lax.dot_general(a, b, (((1,), (1,)), ((), ())), preferred_element_type=F32)


def _lane_scan(v, op, identity, lane_idx):
    shift = 1
    while shift < v.shape[-1]:
        v = op(v, jnp.where(lane_idx >= shift, pltpu.roll(v, shift, v.ndim - 1), identity))
        shift *= 2
    return v


def _silu(v):
    return v * jax.nn.sigmoid(v)


def _log_sigmoid(v):
    return jnp.minimum(v, 0.0) - jnp.log1p(jnp.exp(-jnp.abs(v)))


def _proj_stage(x_ref, pos_ref, win_ref, convw_ref, convb_ref, wq_ref, wk_ref, wv_ref,
                wg_ref, bg_ref, normg_ref, skip_ref, invf_ref, sgn_ref,
                qm_ref, kmt_ref, vm_ref, add_ref, mul_ref, gcol_ref, grow_ref,
                qa_ref, ka_ref, va_ref, sza_ref, qc_ref, szc_ref,
                xpad_scr, kmean_scr,
                *, ml_width, ml_heads, moba_width, mem_width, n_sel, gate_slots):
    t = pl.program_id(1)
    tm = x_ref.shape[0]
    hd = ml_width // ml_heads
    moba_hd = moba_width // MOBA_HEADS
    mem_hd = mem_width // MEM_HEADS
    c_zm = ml_width
    c_qa = 2 * ml_width
    c_ka = c_qa + moba_width
    c_va = c_ka + moba_width
    c_za = c_va + moba_width
    c_qc = c_za + moba_width
    c_zc = c_qc + mem_width

    xb = x_ref[...].astype(BF16)
    pad = xpad_scr.shape[0] - tm
    k_w = convw_ref.shape[0]
    nb_lanes = gate_slots

    def x_cols(c0, width):
        return _dot(xb, win_ref[:, c0:c0 + width])

    def rotary_pair(p, qp, kp):
        rots, means = [], []
        for i in range(2):
            h = 2 * p + i
            ls = slice(i * moba_hd, (i + 1) * moba_hd)
            q_rot = qp[:, ls] * cosf + pltpu.roll(qp[:, ls], moba_hd // 2, 1) * sins
            k_rot = kp[:, ls] * cosf + pltpu.roll(kp[:, ls], moba_hd // 2, 1) * sins
            qa_ref[:, 2 * h * moba_hd:(2 * h + 1) * moba_hd] = (q_rot * (moba_hd ** -0.5)).astype(BF16)
            ka_ref[:, h * moba_hd:(h + 1) * moba_hd] = k_rot.astype(BF16)
            rots.append(q_rot)
            means.append(jnp.mean(k_rot, axis=0, keepdims=True))
        return rots, means

    def conv_gate_head(h, xm_h, zm_h):
        hs = slice(h * hd, (h + 1) * hd)
        xpad_scr[pad:pad + tm, hs] = xm_h
        conv = convb_ref[:, hs] + xm_h * convw_ref[k_w - 1:k_w, hs]
        for j in range(k_w - 1):
            conv = conv + xpad_scr[pl.ds(pad - (k_w - 1) + j, tm), hs] * convw_ref[j:j + 1, hs]
        xc_h = _silu(conv)
        sz_h = _silu(zm_h)
        add_ref[:, hs] = (skip_ref[:, hs] * xc_h * sz_h).astype(BF16)
        mul_ref[:, hs] = (normg_ref[:, hs] * sz_h).astype(BF16)
        return xc_h.astype(BF16), xm_h.astype(BF16)

    def blockdiag_head(h, xc_b, xm_b):
        hs = slice(h * hd, (h + 1) * hd)
        q_h = _dot(xc_b, wq_ref[h]).astype(BF16)
        k_f = _dot(xc_b, wk_ref[h])
        v_h = _dot(xm_b, wv_ref[h]).astype(BF16)
        qm_ref[:, hs] = q_h
        vm_ref[:, hs] = v_h
        kmt_ref[hs, :] = (jnp.transpose(k_f) * (hd ** -0.5)).astype(BF16)
        return q_h, k_f.astype(BF16), v_h

    def gate_terms(h, q_h, k_h, v_h):
        hs = slice(h * hd, (h + 1) * hd)
        return (_dot(q_h, wg_ref[hs, :])
                + _dot(k_h, wg_ref[ml_width + h * hd:ml_width + (h + 1) * hd, :])
                + _dot(v_h, wg_ref[2 * ml_width + h * hd:2 * ml_width + (h + 1) * hd, :]))

    def select_blocks(gate):
        lane = lax.broadcasted_iota(jnp.int32, (tm, V7X_LANES), 1)
        n_l = lane & (nb_lanes - 1)
        valid = (n_l < t) & (lane < MOBA_HEADS * nb_lanes)
        gate = jnp.where(valid, gate, -jnp.inf)
        cnt = jnp.zeros((tm, V7X_LANES), jnp.int32)
        for r in range(1, nb_lanes):
            up = pltpu.roll(gate, r, 1)
            cnt = cnt + ((n_l >= r) & (up >= gate)).astype(jnp.int32)
            dn = pltpu.roll(gate, V7X_LANES - r, 1)
            cnt = cnt + ((n_l < nb_lanes - r) & (dn > gate)).astype(jnp.int32)
        keep = (valid & (cnt < n_sel)) | ((n_l == t) & (lane < MOBA_HEADS * nb_lanes))
        selb = jnp.where(keep, 0.0, MASK_VALUE)
        for h in range(MOBA_HEADS):
            shift = (V7X_LANES - h * nb_lanes) % V7X_LANES
            sel_h = selb if shift == 0 else pltpu.roll(selb, shift, 1)
            qa_ref[:, (2 * h + 1) * moba_hd:(2 * h + 2) * moba_hd] = sel_h.astype(BF16)

    pw = 2 * moba_hd
    qa0, ka0 = x_cols(c_qa, pw), x_cols(c_ka, pw)
    ang = pos_ref[...] * invf_ref[...]
    cosf = jnp.cos(ang)
    sins = jnp.sin(ang) * sgn_ref[...]
    qa1, ka1 = x_cols(c_qa + pw, pw), x_cols(c_ka + pw, pw)
    rots0, means0 = rotary_pair(0, qa0, ka0)
    xm0, zm0 = x_cols(0, hd), x_cols(c_zm, hd)
    rots1, means1 = rotary_pair(1, qa1, ka1)
    q_rots, kmean_rows = rots0 + rots1, means0 + means1
    q_all = jnp.concatenate(q_rots, axis=1)
    km = kmean_scr[...]
    q_hi = q_all.astype(BF16)
    q_lo = (q_all - q_hi.astype(F32)).astype(BF16)
    k_hi = km.astype(BF16)
    k_lo = (km - k_hi.astype(F32)).astype(BF16)
    xz = {0: (xm0, zm0), 1: (x_cols(hd, hd), x_cols(c_zm + hd, hd))}
    gate = _dot_nt(q_hi, k_hi) + _dot_nt(q_lo, k_hi) + _dot_nt(q_hi, k_lo)
    pending = [("xz", h) for h in range(2, ml_heads)] + [("va", None), ("za", None)]
    qkv = []
    va = za = None
    for h in range(ml_heads):
        xc_b, xm_b = conv_gate_head(h, *xz[h])
        if pending:
            kind, arg = pending.pop(0)
            if kind == "xz":
                xz[arg] = (x_cols(arg * hd, hd), x_cols(c_zm + arg * hd, hd))
            elif kind == "va":
                va = x_cols(c_va, moba_width).astype(BF16)
            else:
                za = x_cols(c_za, moba_width)
        qkv.append(blockdiag_head(h, xc_b, xm_b))
    if va is None:
        va = x_cols(c_va, moba_width).astype(BF16)
    if za is None:
        za = x_cols(c_za, moba_width)
    g = jnp.zeros((tm, V7X_LANES), F32) + bg_ref[...]
    for h in range(ml_heads):
        g = g + gate_terms(h, *qkv[h])
    sza_ref[...] = _silu(za).astype(BF16)
    qc = x_cols(c_qc, mem_width)
    zc = x_cols(c_zc, mem_width)

    nrow = grow_ref.shape[0]
    gt = jnp.transpose(g)[0:V7X_SUBLANES, :]
    sub = lax.broadcasted_iota(jnp.int32, gt.shape, 0)
    tok_i = lax.broadcasted_iota(jnp.int32, gt.shape, 1)
    b = _lane_scan(jnp.where(sub >= ml_heads, _log_sigmoid(gt), 0.0), jnp.add, 0.0, tok_i)
    a = _lane_scan(jnp.where(sub >= ml_heads, pltpu.roll(gt, ml_heads, 0) - b, -jnp.inf),
                   jnp.maximum, -jnp.inf, tok_i)
    rows = jnp.concatenate([jnp.where(sub < ml_heads, gt, b), pltpu.roll(a, ml_heads, 0)], axis=0)
    grow_ref[...] = rows
    gcol_ref[...] = jnp.transpose(
        jnp.concatenate([rows, jnp.zeros((V7X_LANES - nrow, tm), F32)], axis=0))

    km_row = lax.broadcasted_iota(jnp.int32, kmean_scr.shape, 0)
    km_head = lax.broadcasted_iota(jnp.int32, kmean_scr.shape, 1) // moba_hd
    kmean_scr[...] = jnp.where(km_row == km_head * nb_lanes + t,
                               jnp.concatenate(kmean_rows, axis=1), kmean_scr[...])
    select_blocks(gate)
    va_ref[...] = va
    qc_ref[...] = (qc * (mem_hd ** -0.5)).astype(BF16)
    szc_ref[...] = _silu(zc).astype(BF16)


def _mlstm_stage(qm_ref, kmt_ref, vm_ref, gcol_ref, grow_ref, add_ref, mul_ref, out_ref,
                 c_scr, n_scr, m_scr, *, heads):
    L = qm_ref.shape[0]
    hd = qm_ref.shape[1] // heads
    row_i = lax.broadcasted_iota(jnp.int32, (L, L), 0)
    col_i = lax.broadcasted_iota(jnp.int32, (L, L), 1)
    causal = row_i >= col_i
    gcol = gcol_ref[...]
    grow = grow_ref[...]
    ones = jnp.ones((L, V7X_LANES), BF16)
    head_slices = [slice(h * hd, (h + 1) * hd) for h in range(heads)]

    qk, inter, qn, mm_l, r_l, w_inter_l = [], [], [], [], [], []
    for h, hs in enumerate(head_slices):
        q = qm_ref[:, hs]
        kt = kmt_ref[hs, :]
        v = vm_ref[:, hs]
        i_row = grow[h:h + 1, :]
        b_row = grow[heads + h:heads + h + 1, :]
        a_col = gcol[:, 2 * heads + h:2 * heads + h + 1]
        m_prev = m_scr[h][0:1, 0:1]
        c_prev = c_scr[h]
        n_prev = n_scr[h]
        qk.append(_dot(q, kt))
        inter.append(_dot(q, c_prev.astype(BF16)))
        qn.append(_dot(q, n_prev.astype(BF16)))
        r_row = i_row - b_row
        mm = jnp.maximum(m_prev, a_col)
        r_l.append(r_row)
        mm_l.append(mm)
        w_inter_l.append(jnp.exp(m_prev - mm))
        b_end = b_row[:, L - 1:L]
        log_w = b_end + r_row
        m_new = jnp.maximum(b_end + m_prev, jnp.max(log_w, axis=1, keepdims=True))
        decay = jnp.exp(b_end + m_prev - m_new)
        kw = (kt.astype(F32) * jnp.exp(log_w - m_new)).astype(BF16)
        c_scr[h] = decay * c_prev + _dot(kw, v)
        n_scr[h] = decay * n_prev + _dot(kw, ones)
        m_scr[h] = jnp.broadcast_to(m_new, m_scr.shape[1:])

    s_qk = [(qk[h] * jnp.exp(jnp.where(causal, r_l[h] - mm_l[h], -jnp.inf))).astype(BF16)
            for h in range(heads)]
    pv = [_dot(s_qk[h], vm_ref[:, hs]) for h, hs in enumerate(head_slices)]
    rs = [_dot(s_qk[h], ones) for h in range(heads)]

    for h, hs in enumerate(head_slices):
        b_col = gcol[:, heads + h:heads + h + 1]
        num = w_inter_l[h] * inter[h] + pv[h]
        den = w_inter_l[h] * qn[h] + rs[h]
        rec = 1.0 / jnp.maximum(jnp.abs(den), jnp.exp(-(b_col + mm_l[h])))
        hh = num * jnp.concatenate([rec] * (hd // V7X_LANES), axis=1)
        mu = jnp.mean(hh, axis=1, keepdims=True)
        hc = hh - mu
        var = jnp.mean(hc * hc, axis=1, keepdims=True)
        hn = hc * lax.rsqrt(var + LN_EPS)
        out_ref[:, hs] = (hn * mul_ref[:, hs].astype(F32) + add_ref[:, hs].astype(F32)).astype(BF16)


def _moba_stage(q_ref, k_ref, v_ref, sz_ref, out_ref, s_scr, m_scr):
    j = pl.program_id(1)
    tq = q_ref.shape[0]
    bs = MOBA_BLOCK
    nblk = k_ref.shape[0] // bs
    aw = q_ref.shape[1] // MOBA_HEADS
    hd = aw // 2
    half = bs // 2
    heads = [slice(h * aw, (h + 1) * aw) for h in range(MOBA_HEADS)]
    kv_heads = [slice(h * hd, (h + 1) * hd) for h in range(MOBA_HEADS)]
    tag_lane = lax.broadcasted_iota(jnp.int32, (bs, hd), 1)
    ones_col = (tag_lane == 0).astype(BF16)

    def attend(own):
        causal = (lax.broadcasted_iota(jnp.int32, (tq, bs), 0)
                  >= lax.broadcasted_iota(jnp.int32, (tq, bs), 1))
        for hs_i, hs in enumerate(heads):
            mx = None
            for n in range(own + 1):
                k_aug = jnp.concatenate([k_ref[n * bs:(n + 1) * bs, kv_heads[hs_i]],
                                         (tag_lane == n).astype(BF16)], axis=1)
                s = _dot_nt(q_ref[:, hs], k_aug)
                if n == own:
                    s = jnp.where(causal, s, MASK_VALUE)
                s_scr[hs_i, n] = s
                sm = jnp.maximum(s[:, :half], s[:, half:])
                mx = sm if mx is None else jnp.maximum(mx, sm)
            m_scr[hs_i] = jnp.broadcast_to(jnp.max(mx, axis=1, keepdims=True), m_scr.shape[1:])
        for hs_i, hs in enumerate(heads):
            m = m_scr[hs_i]
            acc = None
            for n in range(own + 1):
                s = s_scr[hs_i, n]
                p = jnp.concatenate([jnp.exp(s[:, :half] - m), jnp.exp(s[:, half:] - m)], axis=1)
                v_aug = jnp.concatenate([v_ref[n * bs:(n + 1) * bs, kv_heads[hs_i]], ones_col], axis=1)
                pv = _dot(p.astype(BF16), v_aug)
                acc = pv if acc is None else acc + pv
            o = acc[:, :hd] * (1.0 / acc[:, hd:hd + 1])
            out_ref[:, hs_i * hd:(hs_i + 1) * hd] = (
                o * sz_ref[:, hs_i * hd:(hs_i + 1) * hd].astype(F32)).astype(BF16)

    for own in range(nblk):
        pl.when(j == own)(functools.partial(attend, own))


def _mix_stage(x_ref, om_ref, oa_ref, qc_ref, szc_ref, kv_scr, wo_ref, lng_ref, lnb_ref, y_ref):
    width = qc_ref.shape[1]
    hd = width // MEM_HEADS
    w_m = om_ref.shape[1]
    w_a = oa_ref.shape[1]
    head_slices = [slice(h * hd, (h + 1) * hd) for h in range(MEM_HEADS)]
    scores = [_dot_nt(qc_ref[:, hs], kv_scr[:, hs]) for hs in head_slices]
    mixed = _dot(om_ref[...], wo_ref[0:w_m, :]) + _dot(oa_ref[...], wo_ref[w_m:w_m + w_a, :])
    probs = [jnp.exp(s - jnp.max(s, axis=1, keepdims=True)) for s in scores]
    outs = [_dot(p.astype(BF16), kv_scr[:, width + h * hd:width + (h + 1) * hd])
            for h, p in enumerate(probs)]
    oc = jnp.concatenate(
        [(outs[h] * (1.0 / jnp.sum(probs[h], axis=1, keepdims=True))
          * szc_ref[:, hs].astype(F32)).astype(BF16) for h, hs in enumerate(head_slices)], axis=1)
    mixed = mixed + _dot(oc, wo_ref[w_m + w_a:, :])
    y = DEEPNORM_ALPHA * x_ref[...] + mixed
    mu = jnp.mean(y, axis=1, keepdims=True)
    yc = y - mu
    var = jnp.mean(yc * yc, axis=1, keepdims=True)
    y_ref[...] = yc * lax.rsqrt(var + LN_EPS) * lng_ref[...] + lnb_ref[...]


def _layer_kernel(x_ref, pos_ref, mem_ref, win_ref, convw_ref, convb_ref, wq_ref, wk_ref, wv_ref,
                  wg_ref, bg_ref, normg_ref, skip_ref, invf_ref, sgn_ref, wkv_ref, wo_ref, lng_ref, lnb_ref,
                  y_ref,
                  qm_s, kmt_s, vm_s, add_s, mul_s, gcol_s, grow_s, qa_s, ka_s, va_s, sza_s, qc_s, szc_s,
                  om_s, oa_s, xpad_s, kmean_s, c_s, n_s, m_s, score_s, rowmax_s, kv_s,
                  *, ml_width, ml_heads, moba_width, mem_width, n_sel, gate_slots):
    t = pl.program_id(1)
    tm = x_ref.shape[0]
    pad = xpad_s.shape[0] - tm

    @pl.when(t == 0)
    def _():
        xpad_s[0:pad, :] = jnp.zeros((pad, ml_width), F32)
        kmean_s[...] = jnp.zeros(kmean_s.shape, F32)
        c_s[...] = jnp.zeros(c_s.shape, F32)
        n_s[...] = jnp.zeros(n_s.shape, F32)
        m_s[...] = jnp.zeros(m_s.shape, F32)
        kv_s[...] = _dot(mem_ref[...].astype(BF16), wkv_ref[...]).astype(BF16)

    @pl.when(t > 0)
    def _():
        xpad_s[0:pad, :] = xpad_s[tm:tm + pad, :]

    rows = pl.ds(pl.multiple_of(t * tm, tm), tm)
    _proj_stage(x_ref, pos_ref, win_ref, convw_ref, convb_ref, wq_ref, wk_ref, wv_ref,
                wg_ref, bg_ref, normg_ref, skip_ref, invf_ref, sgn_ref,
                qm_s, kmt_s, vm_s, add_s, mul_s, gcol_s, grow_s,
                qa_s, ka_s.at[rows], va_s.at[rows], sza_s, qc_s, szc_s, xpad_s, kmean_s,
                ml_width=ml_width, ml_heads=ml_heads, moba_width=moba_width, mem_width=mem_width,
                n_sel=n_sel, gate_slots=gate_slots)
    _mlstm_stage(qm_s, kmt_s, vm_s, gcol_s, grow_s, add_s, mul_s, om_s, c_s, n_s, m_s, heads=ml_heads)
    _moba_stage(qa_s, ka_s, va_s, sza_s, oa_s, score_s, rowmax_s)
    _mix_stage(x_ref, om_s, oa_s, qc_s, szc_s, kv_s, wo_ref, lng_ref, lnb_ref, y_ref)


def _layer_call(x, pos, mem, win, convw, convb, wq, wk, wv, wg, bg, normg, skip, invf, sgn, wkv, wo, lng, lnb,
                *, ml_width, ml_heads, moba_width, mem_width):
    bsz, s, d = x.shape
    m_tok = mem.shape[1]
    tm = TOKEN_TILE
    nt = s // tm
    n_sel = min(MOBA_TOP_K, nt - 1)
    gate_slots = max(V7X_SUBLANES, pl.next_power_of_2(nt))
    assert MOBA_HEADS * gate_slots <= V7X_LANES and s % tm == 0 and ml_heads >= 2
    hd = ml_width // ml_heads
    tok = lambda w: pl.BlockSpec((None, tm, w), lambda b, t: (b, t, 0))

    def resident(a):
        nd = a.ndim
        return pl.BlockSpec(a.shape, lambda b, t, _nd=nd: (0,) * _nd, pipeline_mode=pl.Buffered(1))

    weights = (win, convw, convb, wq, wk, wv, wg, bg, normg, skip, invf, sgn, wkv, wo, lng, lnb)
    in_specs = ([tok(d), tok(1), pl.BlockSpec((None, m_tok, d), lambda b, t: (b, 0, 0))]
                + [resident(a) for a in weights])
    vmem = pltpu.VMEM
    scratch = [
        vmem((tm, ml_width), BF16),
        vmem((ml_width, tm), BF16),
        vmem((tm, ml_width), BF16),
        vmem((tm, ml_width), BF16),
        vmem((tm, ml_width), BF16),
        vmem((tm, V7X_LANES), F32),
        vmem((GATE_ROWS, tm), F32),
        vmem((tm, 2 * moba_width), BF16),
        vmem((s, moba_width), BF16),
        vmem((s, moba_width), BF16),
        vmem((tm, moba_width), BF16),
        vmem((tm, mem_width), BF16),
        vmem((tm, mem_width), BF16),
        vmem((tm, ml_width), BF16),
        vmem((tm, moba_width), BF16),
        vmem((tm + V7X_SUBLANES, ml_width), F32),
        vmem((V7X_LANES, moba_width), F32),
        vmem((ml_heads, hd, hd), F32),
        vmem((ml_heads, hd, V7X_LANES), F32),
        vmem((ml_heads, V7X_SUBLANES, V7X_LANES), F32),
        vmem((MOBA_HEADS, nt, tm, MOBA_BLOCK), F32),
        vmem((MOBA_HEADS, tm, MOBA_BLOCK // 2), F32),
        vmem((m_tok, 2 * mem_width), BF16),
    ]
    kern = functools.partial(_layer_kernel, ml_width=ml_width, ml_heads=ml_heads, moba_width=moba_width,
                             mem_width=mem_width, n_sel=n_sel, gate_slots=gate_slots)
    return pl.pallas_call(
        kern,
        grid=(bsz, nt),
        in_specs=in_specs,
        out_specs=tok(d),
        out_shape=jax.ShapeDtypeStruct((bsz, s, d), x.dtype),
        scratch_shapes=scratch,
        compiler_params=pltpu.CompilerParams(dimension_semantics=("arbitrary", "arbitrary"),
                                             vmem_limit_bytes=V7X_VMEM_LIMIT_BYTES),
        name="layer",
    )(x, pos, mem, *weights)


def _diag_tiles(w, heads):
    groups, blk, _ = w.shape
    hd = groups // heads * blk
    rows = w.reshape(heads, hd, blk)
    idx = jnp.arange(hd)
    same_group = (idx[:, None] // blk) == (idx[None, :] // blk)
    return jnp.where(same_group, jnp.tile(rows, (1, 1, hd // blk)), 0.0)


def kernel(x, mem, positions, w_in, mlstm_conv_w, mlstm_conv_b, mlstm_wq, mlstm_wk, mlstm_wv, mlstm_w_gates, mlstm_b_gates, mlstm_norm_g, mlstm_skip, w_mem_kv, w_out, ln_g, ln_b):
    bsz, s, d = x.shape
    ml_width = mlstm_conv_w.shape[1]
    ml_heads = mlstm_b_gates.shape[0] // 2
    mem_width = w_mem_kv.shape[1] // 2
    moba_width = (w_in.shape[1] - 2 * ml_width - 2 * mem_width) // 4
    moba_hd = moba_width // MOBA_HEADS
    assert moba_hd == V7X_LANES and mem_width // MEM_HEADS == V7X_LANES and MOBA_HEADS == 4
    assert s % TOKEN_TILE == 0

    row = lambda a: a.reshape(1, -1).astype(F32)
    wq = _diag_tiles(mlstm_wq, ml_heads).astype(BF16)
    wk = _diag_tiles(mlstm_wk, ml_heads).astype(BF16)
    wv = _diag_tiles(mlstm_wv, ml_heads).astype(BF16)
    n_gates = mlstm_w_gates.shape[1]
    wg = jnp.pad(mlstm_w_gates, ((0, 0), (0, V7X_LANES - n_gates))).astype(BF16)
    bg = jnp.pad(mlstm_b_gates, (0, V7X_LANES - n_gates)).reshape(1, -1).astype(F32)
    half = moba_hd // 2
    inv_freq = ROPE_THETA ** (-jnp.arange(half, dtype=F32) * 2.0 / moba_hd)
    invf = jnp.concatenate([inv_freq, inv_freq]).reshape(1, -1)
    sgn = jnp.concatenate([-jnp.ones((half,), F32), jnp.ones((half,), F32)]).reshape(1, -1)
    pos = positions.astype(F32).reshape(bsz, s, 1)

    return _layer_call(
        x, pos, mem, w_in.astype(BF16), mlstm_conv_w.astype(F32), row(mlstm_conv_b), wq, wk, wv, wg, bg,
        row(mlstm_norm_g), row(mlstm_skip), invf, sgn, w_mem_kv.astype(BF16), w_out.astype(BF16),
        row(ln_g), row(ln_b),
        ml_width=ml_width, ml_heads=ml_heads, moba_width=moba_width, mem_width=mem_width)
```

```python
import functools

import jax
import jax.numpy as jnp
from jax import lax
from jax.experimental import pallas as pl
from jax.experimental.pallas import tpu as pltpu

MOBA_HEADS = 4
MOBA_BLOCK = 256
MOBA_TOP_K = 3
MEM_HEADS = 4
ROPE_THETA = 10000.0
DEPTH = 1
DEEPNORM_ALPHA = (2 * DEPTH) ** 0.25
LN_EPS = 1e-5

V7X_LANES = 128
V7X_SUBLANES = 8
V7X_VMEM_BYTES = 64 * 1024 * 1024
V7X_VMEM_LIMIT_BYTES = V7X_VMEM_BYTES - 5 * 1024 * 1024

TOKEN_TILE = MOBA_BLOCK
MASK_VALUE = -1e30
GATE_ROWS = 2 * V7X_SUBLANES

F32 = jnp.float32
BF16 = jnp.bfloat16


def _dot(a, b):
    return jnp.dot(a, b, preferred_element_type=F32)


def _dot_nt(a, b):
    return lax.dot_general(a, b, (((1,), (1,)), ((), ())), preferred_element_type=F32)


def _lane_scan(v, op, identity, lane_idx):
    shift = 1
    while shift < v.shape[-1]:
        v = op(v, jnp.where(lane_idx >= shift, pltpu.roll(v, shift, v.ndim - 1), identity))
        shift *= 2
    return v


def _silu(v):
    return v * jax.nn.sigmoid(v)


def _log_sigmoid(v):
    return jnp.minimum(v, 0.0) - jnp.log1p(jnp.exp(-jnp.abs(v)))


def _proj_stage(x_ref, pos_ref, win_ref, convw_ref, convb_ref, wq_ref, wk_ref, wv_ref,
                wg_ref, bg_ref, normg_ref, skip_ref, invf_ref, sgn_ref,
                qm_ref, kmt_ref, vm_ref, add_ref, mul_ref, gcol_ref, grow_ref,
                qa_ref, ka_ref, va_ref, sza_ref, qc_ref, szc_ref,
                xpad_scr, kmean_scr,
                *, t, ml_width, ml_heads, moba_width, mem_width, n_sel, gate_slots):
    tm = x_ref.shape[0]
    hd = ml_width // ml_heads
    moba_hd = moba_width // MOBA_HEADS
    mem_hd = mem_width // MEM_HEADS
    c_zm = ml_width
    c_qa = 2 * ml_width
    c_ka = c_qa + moba_width
    c_va = c_ka + moba_width
    c_za = c_va + moba_width
    c_qc = c_za + moba_width
    c_zc = c_qc + mem_width

    xb = x_ref[...].astype(BF16)
    pad = xpad_scr.shape[0] - tm
    k_w = convw_ref.shape[0]
    nb_lanes = gate_slots

    def x_cols(c0, width):
        return _dot(xb, win_ref[:, c0:c0 + width])

    def rotary_pair(p, qp, kp):
        rots, means = [], []
        for i in range(2):
            h = 2 * p + i
            ls = slice(i * moba_hd, (i + 1) * moba_hd)
            q_rot = qp[:, ls] * cosf + pltpu.roll(qp[:, ls], moba_hd // 2, 1) * sins
            k_rot = kp[:, ls] * cosf + pltpu.roll(kp[:, ls], moba_hd // 2, 1) * sins
            qa_ref[:, 2 * h * moba_hd:(2 * h + 1) * moba_hd] = (q_rot * (moba_hd ** -0.5)).astype(BF16)
            ka_ref[:, h * moba_hd:(h + 1) * moba_hd] = k_rot.astype(BF16)
            rots.append(q_rot)
            means.append(jnp.mean(k_rot, axis=0, keepdims=True))
        return rots, means

    def conv_gate_head(h, xm_h, zm_h):
        hs = slice(h * hd, (h + 1) * hd)
        xpad_scr[pad:pad + tm, hs] = xm_h
        conv = convb_ref[:, hs] + xm_h * convw_ref[k_w - 1:k_w, hs]
        for j in range(k_w - 1):
            conv = conv + xpad_scr[pl.ds(pad - (k_w - 1) + j, tm), hs] * convw_ref[j:j + 1, hs]
        xc_h = _silu(conv)
        sz_h = _silu(zm_h)
        add_ref[:, hs] = (skip_ref[:, hs] * xc_h * sz_h).astype(BF16)
        mul_ref[:, hs] = (normg_ref[:, hs] * sz_h).astype(BF16)
        return xc_h.astype(BF16), xm_h.astype(BF16)

    def blockdiag_head(h, xc_b, xm_b):
        hs = slice(h * hd, (h + 1) * hd)
        q_h = _dot(xc_b, wq_ref[h]).astype(BF16)
        k_f = _dot(xc_b, wk_ref[h])
        v_h = _dot(xm_b, wv_ref[h]).astype(BF16)
        qm_ref[:, hs] = q_h
        vm_ref[:, hs] = v_h
        kmt_ref[hs, :] = (jnp.transpose(k_f) * (hd ** -0.5)).astype(BF16)
        return q_h, k_f.astype(BF16), v_h

    def gate_terms(h, q_h, k_h, v_h):
        hs = slice(h * hd, (h + 1) * hd)
        return (_dot(q_h, wg_ref[hs, :])
                + _dot(k_h, wg_ref[ml_width + h * hd:ml_width + (h + 1) * hd, :])
                + _dot(v_h, wg_ref[2 * ml_width + h * hd:2 * ml_width + (h + 1) * hd, :]))

    def select_blocks(gate):
        lane = lax.broadcasted_iota(jnp.int32, (tm, V7X_LANES), 1)
        n_l = lane & (nb_lanes - 1)
        valid = (n_l < t) & (lane < MOBA_HEADS * nb_lanes)
        gate = jnp.where(valid, gate, -jnp.inf)
        cnt = jnp.zeros((tm, V7X_LANES), jnp.int32)
        for r in range(1, nb_lanes):
            up = pltpu.roll(gate, r, 1)
            cnt = cnt + ((n_l >= r) & (up >= gate)).astype(jnp.int32)
            dn = pltpu.roll(gate, V7X_LANES - r, 1)
            cnt = cnt + ((n_l < nb_lanes - r) & (dn > gate)).astype(jnp.int32)
        keep = (valid & (cnt < n_sel)) | ((n_l == t) & (lane < MOBA_HEADS * nb_lanes))
        selb = jnp.where(keep, 0.0, MASK_VALUE)
        for h in range(MOBA_HEADS):
            shift = (V7X_LANES - h * nb_lanes) % V7X_LANES
            sel_h = selb if shift == 0 else pltpu.roll(selb, shift, 1)
            qa_ref[:, (2 * h + 1) * moba_hd:(2 * h + 2) * moba_hd] = sel_h.astype(BF16)

    pw = 2 * moba_hd
    qa0, ka0 = x_cols(c_qa, pw), x_cols(c_ka, pw)
    ang = pos_ref[...] * invf_ref[...]
    cosf = jnp.cos(ang)
    sins = jnp.sin(ang) * sgn_ref[...]
    qa1, ka1 = x_cols(c_qa + pw, pw), x_cols(c_ka + pw, pw)
    rots0, means0 = rotary_pair(0, qa0, ka0)
    xm0, zm0 = x_cols(0, hd), x_cols(c_zm, hd)
    rots1, means1 = rotary_pair(1, qa1, ka1)
    q_rots, kmean_rows = rots0 + rots1, means0 + means1
    q_all = jnp.concatenate(q_rots, axis=1)
    km = kmean_scr[...]
    q_hi = q_all.astype(BF16)
    q_lo = (q_all - q_hi.astype(F32)).astype(BF16)
    k_hi = km.astype(BF16)
    k_lo = (km - k_hi.astype(F32)).astype(BF16)
    xz = {0: (xm0, zm0), 1: (x_cols(hd, hd), x_cols(c_zm + hd, hd))}
    gate = _dot_nt(q_hi, k_hi) + _dot_nt(q_lo, k_hi) + _dot_nt(q_hi, k_lo)
    pending = [("xz", h) for h in range(2, ml_heads)] + [("va", None), ("za", None)]
    qkv = []
    va = za = None
    for h in range(ml_heads):
        xc_b, xm_b = conv_gate_head(h, *xz[h])
        if pending:
            kind, arg = pending.pop(0)
            if kind == "xz":
                xz[arg] = (x_cols(arg * hd, hd), x_cols(c_zm + arg * hd, hd))
            elif kind == "va":
                va = x_cols(c_va, moba_width).astype(BF16)
            else:
                za = x_cols(c_za, moba_width)
        qkv.append(blockdiag_head(h, xc_b, xm_b))
    if va is None:
        va = x_cols(c_va, moba_width).astype(BF16)
    if za is None:
        za = x_cols(c_za, moba_width)
    g = jnp.zeros((tm, V7X_LANES), F32) + bg_ref[...]
    for h in range(ml_heads):
        g = g + gate_terms(h, *qkv[h])
    sza_ref[...] = _silu(za).astype(BF16)
    qc = x_cols(c_qc, mem_width)
    zc = x_cols(c_zc, mem_width)

    nrow = grow_ref.shape[0]
    gt = jnp.transpose(g)[0:V7X_SUBLANES, :]
    sub = lax.broadcasted_iota(jnp.int32, gt.shape, 0)
    tok_i = lax.broadcasted_iota(jnp.int32, gt.shape, 1)
    b = _lane_scan(jnp.where(sub >= ml_heads, _log_sigmoid(gt), 0.0), jnp.add, 0.0, tok_i)
    a = _lane_scan(jnp.where(sub >= ml_heads, pltpu.roll(gt, ml_heads, 0) - b, -jnp.inf),
                   jnp.maximum, -jnp.inf, tok_i)
    rows = jnp.concatenate([jnp.where(sub < ml_heads, gt, b), pltpu.roll(a, ml_heads, 0)], axis=0)
    grow_ref[...] = rows
    gcol_ref[...] = jnp.transpose(
        jnp.concatenate([rows, jnp.zeros((V7X_LANES - nrow, tm), F32)], axis=0))

    km_row = lax.broadcasted_iota(jnp.int32, kmean_scr.shape, 0)
    km_head = lax.broadcasted_iota(jnp.int32, kmean_scr.shape, 1) // moba_hd
    kmean_scr[...] = jnp.where(km_row == km_head * nb_lanes + t,
                               jnp.concatenate(kmean_rows, axis=1), kmean_scr[...])
    select_blocks(gate)
    va_ref[...] = va
    qc_ref[...] = (qc * (mem_hd ** -0.5)).astype(BF16)
    szc_ref[...] = _silu(zc).astype(BF16)


def _mlstm_stage(qm_ref, kmt_ref, vm_ref, gcol_ref, grow_ref, add_ref, mul_ref, out_ref,
                 c_scr, n_scr, m_scr, *, heads):
    L = qm_ref.shape[0]
    hd = qm_ref.shape[1] // heads
    row_i = lax.broadcasted_iota(jnp.int32, (L, L), 0)
    col_i = lax.broadcasted_iota(jnp.int32, (L, L), 1)
    causal = row_i >= col_i
    gcol = gcol_ref[...]
    grow = grow_ref[...]
    ones = jnp.ones((L, V7X_LANES), BF16)
    head_slices = [slice(h * hd, (h + 1) * hd) for h in range(heads)]

    qk, inter, qn, mm_l, r_l, w_inter_l = [], [], [], [], [], []
    for h, hs in enumerate(head_slices):
        q = qm_ref[:, hs]
        kt = kmt_ref[hs, :]
        v = vm_ref[:, hs]
        i_row = grow[h:h + 1, :]
        b_row = grow[heads + h:heads + h + 1, :]
        a_col = gcol[:, 2 * heads + h:2 * heads + h + 1]
        m_prev = m_scr[h][0:1, 0:1]
        c_prev = c_scr[h]
        n_prev = n_scr[h]
        qk.append(_dot(q, kt))
        inter.append(_dot(q, c_prev.astype(BF16)))
        qn.append(_dot(q, n_prev.astype(BF16)))
        r_row = i_row - b_row
        mm = jnp.maximum(m_prev, a_col)
        r_l.append(r_row)
        mm_l.append(mm)
        w_inter_l.append(jnp.exp(m_prev - mm))
        b_end = b_row[:, L - 1:L]
        log_w = b_end + r_row
        m_new = jnp.maximum(b_end + m_prev, jnp.max(log_w, axis=1, keepdims=True))
        decay = jnp.exp(b_end + m_prev - m_new)
        kw = (kt.astype(F32) * jnp.exp(log_w - m_new)).astype(BF16)
        c_scr[h] = decay * c_prev + _dot(kw, v)
        n_scr[h] = decay * n_prev + _dot(kw, ones)
        m_scr[h] = jnp.broadcast_to(m_new, m_scr.shape[1:])

    s_qk = [(qk[h] * jnp.exp(jnp.where(causal, r_l[h] - mm_l[h], -jnp.inf))).astype(BF16)
            for h in range(heads)]
    pv = [_dot(s_qk[h], vm_ref[:, hs]) for h, hs in enumerate(head_slices)]
    rs = [_dot(s_qk[h], ones) for h in range(heads)]

    for h, hs in enumerate(head_slices):
        b_col = gcol[:, heads + h:heads + h + 1]
        num = w_inter_l[h] * inter[h] + pv[h]
        den = w_inter_l[h] * qn[h] + rs[h]
        rec = 1.0 / jnp.maximum(jnp.abs(den), jnp.exp(-(b_col + mm_l[h])))
        hh = num * jnp.concatenate([rec] * (hd // V7X_LANES), axis=1)
        mu = jnp.mean(hh, axis=1, keepdims=True)
        hc = hh - mu
        var = jnp.mean(hc * hc, axis=1, keepdims=True)
        hn = hc * lax.rsqrt(var + LN_EPS)
        out_ref[:, hs] = (hn * mul_ref[:, hs].astype(F32) + add_ref[:, hs].astype(F32)).astype(BF16)


def _moba_stage(q_ref, k_ref, v_ref, sz_ref, out_ref, s_scr, m_scr, *, j):
    tq = q_ref.shape[0]
    bs = MOBA_BLOCK
    nblk = k_ref.shape[0] // bs
    aw = q_ref.shape[1] // MOBA_HEADS
    hd = aw // 2
    half = bs // 2
    heads = [slice(h * aw, (h + 1) * aw) for h in range(MOBA_HEADS)]
    kv_heads = [slice(h * hd, (h + 1) * hd) for h in range(MOBA_HEADS)]
    tag_lane = lax.broadcasted_iota(jnp.int32, (bs, hd), 1)
    ones_col = (tag_lane == 0).astype(BF16)

    n_slots = s_scr.shape[0]

    def attend(own):
        causal = (lax.broadcasted_iota(jnp.int32, (tq, bs), 0)
                  >= lax.broadcasted_iota(jnp.int32, (tq, bs), 1))

        def score_pass(hi):
            slot = hi % n_slots
            mx = None
            for n in range(own + 1):
                k_aug = jnp.concatenate([k_ref[n * bs:(n + 1) * bs, kv_heads[hi]],
                                         (tag_lane == n).astype(BF16)], axis=1)
                s = _dot_nt(q_ref[:, heads[hi]], k_aug)
                if n == own:
                    s = jnp.where(causal, s, MASK_VALUE)
                s_scr[slot, n] = s
                sm = jnp.maximum(s[:, :half], s[:, half:])
                mx = sm if mx is None else jnp.maximum(mx, sm)
            m_scr[slot] = jnp.broadcast_to(jnp.max(mx, axis=1, keepdims=True), m_scr.shape[1:])

        def value_pass(hi):
            slot = hi % n_slots
            m = m_scr[slot]
            acc = None
            for n in range(own + 1):
                s = s_scr[slot, n]
                p = jnp.concatenate([jnp.exp(s[:, :half] - m), jnp.exp(s[:, half:] - m)], axis=1)
                v_aug = jnp.concatenate([v_ref[n * bs:(n + 1) * bs, kv_heads[hi]], ones_col], axis=1)
                pv = _dot(p.astype(BF16), v_aug)
                acc = pv if acc is None else acc + pv
            o = acc[:, :hd] * (1.0 / acc[:, hd:hd + 1])
            out_ref[:, hi * hd:(hi + 1) * hd] = (o * sz_ref[:, hi * hd:(hi + 1) * hd].astype(F32)).astype(BF16)

        score_pass(0)
        for hi in range(MOBA_HEADS):
            if hi + 1 < MOBA_HEADS:
                score_pass(hi + 1)
            value_pass(hi)

    for own in range(nblk):
        pl.when(j == own)(functools.partial(attend, own))


def _mix_stage(x_ref, om_ref, oa_ref, qc_ref, szc_ref, kv_scr, wo_ref, lng_ref, lnb_ref, y_ref):
    width = qc_ref.shape[1]
    hd = width // MEM_HEADS
    w_m = om_ref.shape[1]
    w_a = oa_ref.shape[1]
    head_slices = [slice(h * hd, (h + 1) * hd) for h in range(MEM_HEADS)]
    scores = [_dot_nt(qc_ref[:, hs], kv_scr[:, hs]) for hs in head_slices]
    mixed = _dot(om_ref[...], wo_ref[0:w_m, :]) + _dot(oa_ref[...], wo_ref[w_m:w_m + w_a, :])
    probs = [jnp.exp(s - jnp.max(s, axis=1, keepdims=True)) for s in scores]
    outs = [_dot(p.astype(BF16), kv_scr[:, width + h * hd:width + (h + 1) * hd])
            for h, p in enumerate(probs)]
    oc = jnp.concatenate(
        [(outs[h] * (1.0 / jnp.sum(probs[h], axis=1, keepdims=True))
          * szc_ref[:, hs].astype(F32)).astype(BF16) for h, hs in enumerate(head_slices)], axis=1)
    mixed = mixed + _dot(oc, wo_ref[w_m + w_a:, :])
    y = DEEPNORM_ALPHA * x_ref[...] + mixed
    mu = jnp.mean(y, axis=1, keepdims=True)
    yc = y - mu
    var = jnp.mean(yc * yc, axis=1, keepdims=True)
    y_ref[...] = yc * lax.rsqrt(var + LN_EPS) * lng_ref[...] + lnb_ref[...]


def _layer_kernel(x_ref, pos_ref, xres_ref, mem_ref, win_ref, convw_ref, convb_ref, wq_ref, wk_ref, wv_ref,
                  wg_ref, bg_ref, normg_ref, skip_ref, invf_ref, sgn_ref, wkv_ref, wo_ref, lng_ref, lnb_ref,
                  y_ref,
                  qm_s, kmt_s, vm_s, add_s, mul_s, gcol_s, grow_s, qa_s, sza_s, qc_s, szc_s,
                  ka_s, va_s, om_s, oa_s, xpad_s, kmean_s, c_s, n_s, m_s, score_s, rowmax_s, kv_s,
                  *, tiles_per_row, n_tiles, ml_width, ml_heads, moba_width, mem_width, n_sel, gate_slots):
    g = pl.program_id(0)
    tm = x_ref.shape[0]
    pad = xpad_s.shape[0] - tm
    t_p = lax.rem(jnp.minimum(g, n_tiles - 1), tiles_per_row)
    t_c = lax.rem(jnp.maximum(g - 1, 0), tiles_per_row)
    slot_p = lax.rem(g, 2)
    slot_c = 1 - slot_p
    handoff = (qm_s, kmt_s, vm_s, add_s, mul_s, gcol_s, grow_s, qa_s, sza_s, qc_s, szc_s)

    @pl.when(g == 0)
    def _():
        for ref in handoff:
            ref[1] = jnp.zeros(ref.shape[1:], ref.dtype)
        ka_s[...] = jnp.zeros(ka_s.shape, BF16)
        va_s[...] = jnp.zeros(va_s.shape, BF16)

    @pl.when(t_p == 0)
    def _():
        xpad_s[0:pad, :] = jnp.zeros((pad, ml_width), F32)
        kmean_s[...] = jnp.zeros(kmean_s.shape, F32)

    @pl.when(t_p > 0)
    def _():
        xpad_s[0:pad, :] = xpad_s[tm:tm + pad, :]

    @pl.when(t_c == 0)
    def _():
        c_s[...] = jnp.zeros(c_s.shape, F32)
        n_s[...] = jnp.zeros(n_s.shape, F32)
        m_s[...] = jnp.zeros(m_s.shape, F32)
        kv_s[...] = _dot(mem_ref[...].astype(BF16), wkv_ref[...]).astype(BF16)

    qm_c, kmt_c, vm_c, add_c, mul_c, gcol_c, grow_c, qa_c, sza_c, qc_c, szc_c = [r.at[slot_c] for r in handoff]
    _moba_stage(qa_c, ka_s, va_s, sza_c, oa_s, score_s, rowmax_s, j=t_c)
    rows = pl.ds(pl.multiple_of(t_p * tm, tm), tm)
    qm_p, kmt_p, vm_p, add_p, mul_p, gcol_p, grow_p, qa_p, sza_p, qc_p, szc_p = [r.at[slot_p] for r in handoff]
    _proj_stage(x_ref, pos_ref, win_ref, convw_ref, convb_ref, wq_ref, wk_ref, wv_ref,
                wg_ref, bg_ref, normg_ref, skip_ref, invf_ref, sgn_ref,
                qm_p, kmt_p, vm_p, add_p, mul_p, gcol_p, grow_p,
                qa_p, ka_s.at[rows], va_s.at[rows], sza_p, qc_p, szc_p, xpad_s, kmean_s,
                t=t_p, ml_width=ml_width, ml_heads=ml_heads, moba_width=moba_width, mem_width=mem_width,
                n_sel=n_sel, gate_slots=gate_slots)
    _mlstm_stage(qm_c, kmt_c, vm_c, gcol_c, grow_c, add_c, mul_c, om_s, c_s, n_s, m_s, heads=ml_heads)
    _mix_stage(xres_ref, om_s, oa_s, qc_c, szc_c, kv_s, wo_ref, lng_ref, lnb_ref, y_ref)


def _layer_call(x, pos, mem, win, convw, convb, wq, wk, wv, wg, bg, normg, skip, invf, sgn, wkv, wo, lng, lnb,
                *, ml_width, ml_heads, moba_width, mem_width):
    bsz, s, d = x.shape
    m_tok = mem.shape[1]
    tm = TOKEN_TILE
    nt = s // tm
    n_tiles = bsz * nt
    n_sel = min(MOBA_TOP_K, nt - 1)
    gate_slots = max(V7X_SUBLANES, pl.next_power_of_2(nt))
    assert MOBA_HEADS * gate_slots <= V7X_LANES and s % tm == 0 and ml_heads >= 2
    hd = ml_width // ml_heads
    proj_tile = lambda g: (jnp.minimum(g, n_tiles - 1), 0, 0)
    mix_tile = lambda g: (jnp.maximum(g - 1, 0), 0, 0)

    def resident(a):
        nd = a.ndim
        return pl.BlockSpec(a.shape, lambda g, _nd=nd: (0,) * _nd, pipeline_mode=pl.Buffered(1))

    weights = (win, convw, convb, wq, wk, wv, wg, bg, normg, skip, invf, sgn, wkv, wo, lng, lnb)
    x_tiles = x.reshape(n_tiles, tm, d)
    in_specs = ([pl.BlockSpec((None, tm, d), proj_tile),
                 pl.BlockSpec((None, tm, 1), proj_tile),
                 pl.BlockSpec((None, tm, d), mix_tile),
                 pl.BlockSpec((None, m_tok, d), lambda g: (jnp.maximum(g - 1, 0) // nt, 0, 0),
                              pipeline_mode=pl.Buffered(1))]
                + [resident(a) for a in weights])
    vmem = pltpu.VMEM
    score_heads = 2
    scratch = [
        vmem((2, tm, ml_width), BF16),
        vmem((2, ml_width, tm), BF16),
        vmem((2, tm, ml_width), BF16),
        vmem((2, tm, ml_width), BF16),
        vmem((2, tm, ml_width), BF16),
        vmem((2, tm, V7X_LANES), F32),
        vmem((2, GATE_ROWS, tm), F32),
        vmem((2, tm, 2 * moba_width), BF16),
        vmem((2, tm, moba_width), BF16),
        vmem((2, tm, mem_width), BF16),
        vmem((2, tm, mem_width), BF16),
        vmem((s, moba_width), BF16),
        vmem((s, moba_width), BF16),
        vmem((tm, ml_width), BF16),
        vmem((tm, moba_width), BF16),
        vmem((tm + V7X_SUBLANES, ml_width), F32),
        vmem((V7X_LANES, moba_width), F32),
        vmem((ml_heads, hd, hd), F32),
        vmem((ml_heads, hd, V7X_LANES), F32),
        vmem((ml_heads, V7X_SUBLANES, V7X_LANES), F32),
        vmem((score_heads, nt, tm, MOBA_BLOCK), F32),
        vmem((score_heads, tm, MOBA_BLOCK // 2), F32),
        vmem((m_tok, 2 * mem_width), BF16),
    ]
    kern = functools.partial(_layer_kernel, tiles_per_row=nt, n_tiles=n_tiles, ml_width=ml_width,
                             ml_heads=ml_heads, moba_width=moba_width, mem_width=mem_width, n_sel=n_sel,
                             gate_slots=gate_slots)
    y = pl.pallas_call(
        kern,
        grid=(n_tiles + 1,),
        in_specs=in_specs,
        out_specs=pl.BlockSpec((None, tm, d), mix_tile),
        out_shape=jax.ShapeDtypeStruct((n_tiles, tm, d), x.dtype),
        scratch_shapes=scratch,
        compiler_params=pltpu.CompilerParams(dimension_semantics=("arbitrary",),
                                             vmem_limit_bytes=V7X_VMEM_LIMIT_BYTES),
        name="layer",
    )(x_tiles, pos.reshape(n_tiles, tm, 1), x_tiles, mem, *weights)
    return y.reshape(bsz, s, d)


def _diag_tiles(w, heads):
    groups, blk, _ = w.shape
    hd = groups // heads * blk
    rows = w.reshape(heads, hd, blk)
    idx = jnp.arange(hd)
    same_group = (idx[:, None] // blk) == (idx[None, :] // blk)
    return jnp.where(same_group, jnp.tile(rows, (1, 1, hd // blk)), 0.0)


def kernel(x, mem, positions, w_in, mlstm_conv_w, mlstm_conv_b, mlstm_wq, mlstm_wk, mlstm_wv, mlstm_w_gates, mlstm_b_gates, mlstm_norm_g, mlstm_skip, w_mem_kv, w_out, ln_g, ln_b):
    bsz, s, d = x.shape
    ml_width = mlstm_conv_w.shape[1]
    ml_heads = mlstm_b_gates.shape[0] // 2
    mem_width = w_mem_kv.shape[1] // 2
    moba_width = (w_in.shape[1] - 2 * ml_width - 2 * mem_width) // 4
    moba_hd = moba_width // MOBA_HEADS
    assert moba_hd == V7X_LANES and mem_width // MEM_HEADS == V7X_LANES and MOBA_HEADS == 4
    assert s % TOKEN_TILE == 0

    row = lambda a: a.reshape(1, -1).astype(F32)
    wq = _diag_tiles(mlstm_wq, ml_heads).astype(BF16)
    wk = _diag_tiles(mlstm_wk, ml_heads).astype(BF16)
    wv = _diag_tiles(mlstm_wv, ml_heads).astype(BF16)
    n_gates = mlstm_w_gates.shape[1]
    wg = jnp.pad(mlstm_w_gates, ((0, 0), (0, V7X_LANES - n_gates))).astype(BF16)
    bg = jnp.pad(mlstm_b_gates, (0, V7X_LANES - n_gates)).reshape(1, -1).astype(F32)
    half = moba_hd // 2
    inv_freq = ROPE_THETA ** (-jnp.arange(half, dtype=F32) * 2.0 / moba_hd)
    invf = jnp.concatenate([inv_freq, inv_freq]).reshape(1, -1)
    sgn = jnp.concatenate([-jnp.ones((half,), F32), jnp.ones((half,), F32)]).reshape(1, -1)
    pos = positions.astype(F32).reshape(bsz, s, 1)

    return _layer_call(
        x, pos, mem, w_in.astype(BF16), mlstm_conv_w.astype(F32), row(mlstm_conv_b), wq, wk, wv, wg, bg,
        row(mlstm_norm_g), row(mlstm_skip), invf, sgn, w_mem_kv.astype(BF16), w_out.astype(BF16),
        row(ln_g), row(ln_b),
        ml_width=ml_width, ml_heads=ml_heads, moba_width=moba_width, mem_width=mem_width)
```

```python
import functools

import jax
import jax.numpy as jnp
from jax import lax
from jax.experimental import pallas as pl
from jax.experimental.pallas import tpu as pltpu

MOBA_HEADS = 4
MOBA_BLOCK = 256
MOBA_TOP_K = 3
MEM_HEADS = 4
ROPE_THETA = 10000.0
DEPTH = 1
DEEPNORM_ALPHA = (2 * DEPTH) ** 0.25
LN_EPS = 1e-5

V7X_LANES = 128
V7X_SUBLANES = 8
V7X_VMEM_BYTES = 64 * 1024 * 1024
V7X_VMEM_LIMIT_BYTES = V7X_VMEM_BYTES - 8 * 1024 * 1024

TOKEN_TILE = MOBA_BLOCK
MASK_VALUE = -1e30
GATE_ROWS = 2 * V7X_SUBLANES

F32 = jnp.float32
BF16 = jnp.bfloat16


def _dot(a, b):
    return jnp.dot(a, b, preferred_element_type=F32)


def _dot_nt(a, b):
    return lax.dot_general(a, b, (((1,), (1,)), ((), ())), preferred_element_type=F32)


def _lane_scan(v, op, identity, lane_idx):
    shift = 1
    while shift < v.shape[-1]:
        v = op(v, jnp.where(lane_idx >= shift, pltpu.roll(v, shift, v.ndim - 1), identity))
        shift *= 2
    return v


def _silu(v):
    return v * jax.nn.sigmoid(v)


def _log_sigmoid(v):
    return jnp.minimum(v, 0.0) - jnp.log1p(jnp.exp(-jnp.abs(v)))


def _proj_stage(x_ref, pos_ref, win_ref, convw_ref, convb_ref, wq_ref, wk_ref, wv_ref,
                wg_ref, bg_ref, normg_ref, skip_ref, invf_ref,
                qm_ref, kmt_ref, vm_ref, add_ref, mul_ref, gcol_ref, grow_ref,
                qa_ref, ka_ref, va_ref, sza_ref, qc_ref, szc_ref,
                xpad_scr, kmean_scr,
                *, t, ml_width, ml_heads, moba_width, mem_width, n_sel, gate_slots):
    tm = x_ref.shape[0]
    hd = ml_width // ml_heads
    moba_hd = moba_width // MOBA_HEADS
    mem_hd = mem_width // MEM_HEADS
    c_zm = ml_width
    c_qa = 2 * ml_width
    c_ka = c_qa + moba_width
    c_va = c_ka + moba_width
    c_za = c_va + moba_width
    c_qc = c_za + moba_width
    c_zc = c_qc + mem_width

    xb = x_ref[...].astype(BF16)
    pad = xpad_scr.shape[0] - tm
    k_w = convw_ref.shape[0]
    nb_lanes = gate_slots

    def x_cols(c0, width):
        return _dot(xb, win_ref[:, c0:c0 + width])

    def rotary_pair(p, qp, kp):
        rots, means = [], []
        for i in range(2):
            h = 2 * p + i
            ls = slice(i * moba_hd, (i + 1) * moba_hd)
            q_rot = qp[:, ls] * cosf + pltpu.roll(qp[:, ls], moba_hd // 2, 1) * sins
            k_rot = kp[:, ls] * cosf + pltpu.roll(kp[:, ls], moba_hd // 2, 1) * sins
            qa_ref[:, 2 * h * moba_hd:(2 * h + 1) * moba_hd] = (q_rot * (moba_hd ** -0.5)).astype(BF16)
            ka_ref[:, h * moba_hd:(h + 1) * moba_hd] = k_rot.astype(BF16)
            rots.append(q_rot)
            means.append(jnp.mean(k_rot, axis=0, keepdims=True))
        return rots, means

    def conv_gate_head(h, xm_h, zm_h):
        hs = slice(h * hd, (h + 1) * hd)
        xpad_scr[pad:pad + tm, hs] = xm_h
        conv = convb_ref[:, hs] + xm_h * convw_ref[k_w - 1:k_w, hs]
        for j in range(k_w - 1):
            conv = conv + xpad_scr[pl.ds(pad - (k_w - 1) + j, tm), hs] * convw_ref[j:j + 1, hs]
        xc_h = _silu(conv)
        sz_h = _silu(zm_h)
        add_ref[:, hs] = (skip_ref[:, hs] * xc_h * sz_h).astype(BF16)
        mul_ref[:, hs] = (normg_ref[:, hs] * sz_h).astype(BF16)
        return xc_h.astype(BF16), xm_h.astype(BF16)

    def blockdiag_head(h, xc_b, xm_b):
        hs = slice(h * hd, (h + 1) * hd)
        q_h = _dot(xc_b, wq_ref[h]).astype(BF16)
        k_f = _dot(xc_b, wk_ref[h])
        v_h = _dot(xm_b, wv_ref[h]).astype(BF16)
        qm_ref[:, hs] = q_h
        vm_ref[:, hs] = v_h
        kmt_ref[hs, :] = (jnp.transpose(k_f) * (hd ** -0.5)).astype(BF16)
        return q_h, k_f.astype(BF16), v_h

    def gate_terms(h, q_h, k_h, v_h):
        hs = slice(h * hd, (h + 1) * hd)
        return (_dot(q_h, wg_ref[hs, :])
                + _dot(k_h, wg_ref[ml_width + h * hd:ml_width + (h + 1) * hd, :])
                + _dot(v_h, wg_ref[2 * ml_width + h * hd:2 * ml_width + (h + 1) * hd, :]))

    def select_blocks(gate_t):
        n_rows = MOBA_HEADS * nb_lanes
        gate = gate_t[0:n_rows, :]
        row = lax.broadcasted_iota(jnp.int32, gate.shape, 0)
        n_r = row & (nb_lanes - 1)
        valid = n_r < t
        gate = jnp.where(valid, gate, -jnp.inf)
        cnt = jnp.zeros(gate.shape, jnp.int32)
        for r in range(1, nb_lanes):
            up = pltpu.roll(gate, r, 0)
            cnt = cnt + ((n_r >= r) & (up >= gate)).astype(jnp.int32)
            dn = pltpu.roll(gate, n_rows - r, 0)
            cnt = cnt + ((n_r < nb_lanes - r) & (dn > gate)).astype(jnp.int32)
        keep = (valid & (cnt < n_sel)) | (n_r == t)
        selb_t = jnp.where(keep, 0.0, MASK_VALUE)
        selb = jnp.transpose(jnp.concatenate(
            [selb_t, jnp.full((V7X_LANES - n_rows, tm), MASK_VALUE, F32)], axis=0))
        for h in range(MOBA_HEADS):
            shift = (V7X_LANES - h * nb_lanes) % V7X_LANES
            sel_h = selb if shift == 0 else pltpu.roll(selb, shift, 1)
            qa_ref[:, (2 * h + 1) * moba_hd:(2 * h + 2) * moba_hd] = sel_h.astype(BF16)

    pw = 2 * moba_hd
    qa0, ka0 = x_cols(c_qa, pw), x_cols(c_ka, pw)
    ang_t = invf_ref[...] * pos_ref[...]
    cos_t = jnp.cos(ang_t)
    sin_t = jnp.sin(ang_t)
    cosf = jnp.transpose(jnp.concatenate([cos_t, cos_t], axis=0))
    sins = jnp.transpose(jnp.concatenate([-sin_t, sin_t], axis=0))
    qa1, ka1 = x_cols(c_qa + pw, pw), x_cols(c_ka + pw, pw)
    rots0, means0 = rotary_pair(0, qa0, ka0)
    xm0, zm0 = x_cols(0, hd), x_cols(c_zm, hd)
    rots1, means1 = rotary_pair(1, qa1, ka1)
    q_rots, kmean_rows = rots0 + rots1, means0 + means1
    q_all = jnp.concatenate(q_rots, axis=1)
    km = kmean_scr[...]
    q_hi = q_all.astype(BF16)
    q_lo = (q_all - q_hi.astype(F32)).astype(BF16)
    k_hi = km.astype(BF16)
    k_lo = (km - k_hi.astype(F32)).astype(BF16)
    xz = {0: (xm0, zm0), 1: (x_cols(hd, hd), x_cols(c_zm + hd, hd))}
    gate = _dot_nt(k_hi, q_hi) + _dot_nt(k_hi, q_lo) + _dot_nt(k_lo, q_hi)
    pending = [("xz", h) for h in range(2, ml_heads)] + [("va", None), ("za", None)]
    qkv = []
    va = za = None
    for h in range(ml_heads):
        xc_b, xm_b = conv_gate_head(h, *xz[h])
        if pending:
            kind, arg = pending.pop(0)
            if kind == "xz":
                xz[arg] = (x_cols(arg * hd, hd), x_cols(c_zm + arg * hd, hd))
            elif kind == "va":
                va = x_cols(c_va, moba_width).astype(BF16)
            else:
                za = x_cols(c_za, moba_width)
        qkv.append(blockdiag_head(h, xc_b, xm_b))
    if va is None:
        va = x_cols(c_va, moba_width).astype(BF16)
    if za is None:
        za = x_cols(c_za, moba_width)
    g = jnp.zeros((tm, V7X_LANES), F32) + bg_ref[...]
    for h in range(ml_heads):
        g = g + gate_terms(h, *qkv[h])
    sza_ref[...] = _silu(za).astype(BF16)
    qc = x_cols(c_qc, mem_width)
    zc = x_cols(c_zc, mem_width)

    nrow = grow_ref.shape[0]
    gt = jnp.transpose(g)[0:V7X_SUBLANES, :]
    sub = lax.broadcasted_iota(jnp.int32, gt.shape, 0)
    tok_i = lax.broadcasted_iota(jnp.int32, gt.shape, 1)
    b = _lane_scan(jnp.where(sub >= ml_heads, _log_sigmoid(gt), 0.0), jnp.add, 0.0, tok_i)
    a = _lane_scan(jnp.where(sub >= ml_heads, pltpu.roll(gt, ml_heads, 0) - b, -jnp.inf),
                   jnp.maximum, -jnp.inf, tok_i)
    rows = jnp.concatenate([jnp.where(sub < ml_heads, gt, b), pltpu.roll(a, ml_heads, 0)], axis=0)
    grow_ref[...] = rows
    gcol_ref[...] = jnp.transpose(
        jnp.concatenate([rows, jnp.zeros((V7X_LANES - nrow, tm), F32)], axis=0))

    km_row = lax.broadcasted_iota(jnp.int32, kmean_scr.shape, 0)
    km_head = lax.broadcasted_iota(jnp.int32, kmean_scr.shape, 1) // moba_hd
    kmean_scr[...] = jnp.where(km_row == km_head * nb_lanes + t,
                               jnp.concatenate(kmean_rows, axis=1), kmean_scr[...])
    select_blocks(gate)
    va_ref[...] = va
    qc_ref[...] = (qc * (mem_hd ** -0.5)).astype(BF16)
    szc_ref[...] = _silu(zc).astype(BF16)


def _mlstm_stage(qm_ref, kmt_ref, vm_ref, gcol_ref, grow_ref, add_ref, mul_ref, out_ref,
                 c_scr, n_scr, m_scr, *, heads):
    L = qm_ref.shape[0]
    hd = qm_ref.shape[1] // heads
    row_i = lax.broadcasted_iota(jnp.int32, (L, L), 0)
    col_i = lax.broadcasted_iota(jnp.int32, (L, L), 1)
    causal = row_i >= col_i
    gcol = gcol_ref[...]
    grow = grow_ref[...]
    ones = jnp.ones((L, V7X_LANES), BF16)
    head_slices = [slice(h * hd, (h + 1) * hd) for h in range(heads)]

    qk, inter, qn, mm_l, r_l, w_inter_l = [], [], [], [], [], []
    for h, hs in enumerate(head_slices):
        q = qm_ref[:, hs]
        kt = kmt_ref[hs, :]
        v = vm_ref[:, hs]
        i_row = grow[h:h + 1, :]
        b_row = grow[heads + h:heads + h + 1, :]
        a_col = gcol[:, 2 * heads + h:2 * heads + h + 1]
        m_prev = m_scr[h][0:1, 0:1]
        c_prev = c_scr[h]
        n_prev = n_scr[h]
        qk.append(_dot(q, kt))
        inter.append(_dot(q, c_prev.astype(BF16)))
        qn.append(_dot(q, n_prev.astype(BF16)))
        r_row = i_row - b_row
        mm = jnp.maximum(m_prev, a_col)
        r_l.append(r_row)
        mm_l.append(mm)
        w_inter_l.append(jnp.exp(m_prev - mm))
        b_end = b_row[:, L - 1:L]
        log_w = b_end + r_row
        m_new = jnp.maximum(b_end + m_prev, jnp.max(log_w, axis=1, keepdims=True))
        decay = jnp.exp(b_end + m_prev - m_new)
        kw = (kt.astype(F32) * jnp.exp(log_w - m_new)).astype(BF16)
        c_scr[h] = decay * c_prev + _dot(kw, v)
        n_scr[h] = decay * n_prev + _dot(kw, ones)
        m_scr[h] = jnp.broadcast_to(m_new, m_scr.shape[1:])

    s_qk = [(qk[h] * jnp.exp(jnp.where(causal, r_l[h] - mm_l[h], -jnp.inf))).astype(BF16)
            for h in range(heads)]
    pv = [_dot(s_qk[h], vm_ref[:, hs]) for h, hs in enumerate(head_slices)]
    rs = [_dot(s_qk[h], ones) for h in range(heads)]

    for h, hs in enumerate(head_slices):
        b_col = gcol[:, heads + h:heads + h + 1]
        num = w_inter_l[h] * inter[h] + pv[h]
        den = w_inter_l[h] * qn[h] + rs[h]
        rec = 1.0 / jnp.maximum(jnp.abs(den), jnp.exp(-(b_col + mm_l[h])))
        hh = num * jnp.concatenate([rec] * (hd // V7X_LANES), axis=1)
        mu = jnp.mean(hh, axis=1, keepdims=True)
        hc = hh - mu
        var = jnp.mean(hc * hc, axis=1, keepdims=True)
        hn = hc * lax.rsqrt(var + LN_EPS)
        out_ref[:, hs] = (hn * mul_ref[:, hs].astype(F32) + add_ref[:, hs].astype(F32)).astype(BF16)


def _moba_stage(q_ref, k_ref, v_ref, sz_ref, out_ref, s_scr, m_scr, *, j, then):
    tq = q_ref.shape[0]
    bs = MOBA_BLOCK
    nblk = k_ref.shape[0] // bs
    aw = q_ref.shape[1] // MOBA_HEADS
    hd = aw // 2
    half = bs // 2
    heads = [slice(h * aw, (h + 1) * aw) for h in range(MOBA_HEADS)]
    kv_heads = [slice(h * hd, (h + 1) * hd) for h in range(MOBA_HEADS)]
    tag_lane = lax.broadcasted_iota(jnp.int32, (bs, hd), 1)
    ones_col = (tag_lane == 0).astype(BF16)

    n_slots = s_scr.shape[0]

    def attend(own):
        causal = (lax.broadcasted_iota(jnp.int32, (tq, bs), 0)
                  >= lax.broadcasted_iota(jnp.int32, (tq, bs), 1))

        def score_pass(hi):
            slot = hi % n_slots
            mx = None
            for n in range(own + 1):
                k_aug = jnp.concatenate([k_ref[n * bs:(n + 1) * bs, kv_heads[hi]],
                                         (tag_lane == n).astype(BF16)], axis=1)
                s = _dot_nt(q_ref[:, heads[hi]], k_aug)
                if n == own:
                    s = jnp.where(causal, s, MASK_VALUE)
                s_scr[slot, n] = s
                sm = jnp.maximum(s[:, :half], s[:, half:])
                mx = sm if mx is None else jnp.maximum(mx, sm)
            m_scr[slot] = jnp.broadcast_to(jnp.max(mx, axis=1, keepdims=True), m_scr.shape[1:])

        def value_pass(hi):
            slot = hi % n_slots
            m = m_scr[slot]
            acc = None
            for n in range(own + 1):
                s = s_scr[slot, n]
                p = jnp.concatenate([jnp.exp(s[:, :half] - m), jnp.exp(s[:, half:] - m)], axis=1)
                v_aug = jnp.concatenate([v_ref[n * bs:(n + 1) * bs, kv_heads[hi]], ones_col], axis=1)
                pv = _dot(p.astype(BF16), v_aug)
                acc = pv if acc is None else acc + pv
            o = acc[:, :hd] * (1.0 / acc[:, hd:hd + 1])
            out_ref[:, hi * hd:(hi + 1) * hd] = (o * sz_ref[:, hi * hd:(hi + 1) * hd].astype(F32)).astype(BF16)

        ahead = min(n_slots, MOBA_HEADS)
        for hi in range(ahead):
            score_pass(hi)
        for hi in range(MOBA_HEADS):
            value_pass(hi)
            if hi + ahead < MOBA_HEADS:
                score_pass(hi + ahead)
        then()

    for own in range(nblk):
        pl.when(j == own)(functools.partial(attend, own))


def _mix_stage(x_ref, om_ref, oa_ref, qc_ref, szc_ref, kv_scr, wo_ref, lng_ref, lnb_ref, y_ref):
    width = qc_ref.shape[1]
    hd = width // MEM_HEADS
    w_m = om_ref.shape[1]
    w_a = oa_ref.shape[1]
    head_slices = [slice(h * hd, (h + 1) * hd) for h in range(MEM_HEADS)]
    scores = [_dot_nt(qc_ref[:, hs], kv_scr[:, hs]) for hs in head_slices]
    mixed = _dot(oa_ref[...], wo_ref[w_m:w_m + w_a, :])
    probs = [jnp.exp(s - jnp.max(s, axis=1, keepdims=True)) for s in scores]
    outs = [_dot(p.astype(BF16), kv_scr[:, width + h * hd:width + (h + 1) * hd])
            for h, p in enumerate(probs)]
    oc = jnp.concatenate(
        [(outs[h] * (1.0 / jnp.sum(probs[h], axis=1, keepdims=True))
          * szc_ref[:, hs].astype(F32)).astype(BF16) for h, hs in enumerate(head_slices)], axis=1)
    mixed = mixed + _dot(oc, wo_ref[w_m + w_a:, :])
    mixed = mixed + _dot(om_ref[...], wo_ref[0:w_m, :])
    y = DEEPNORM_ALPHA * x_ref[...] + mixed
    mu = jnp.mean(y, axis=1, keepdims=True)
    yc = y - mu
    var = jnp.mean(yc * yc, axis=1, keepdims=True)
    y_ref[...] = yc * lax.rsqrt(var + LN_EPS) * lng_ref[...] + lnb_ref[...]


def _layer_kernel(x_ref, pos_ref, mem_ref, win_ref, convw_ref, convb_ref, wq_ref, wk_ref, wv_ref,
                  wg_ref, bg_ref, normg_ref, skip_ref, invf_ref, wkv_ref, wo_ref, lng_ref, lnb_ref,
                  y_ref,
                  qm_s, kmt_s, vm_s, add_s, mul_s, gcol_s, grow_s, qa_s, sza_s, qc_s, szc_s,
                  ka_s, va_s, om_s, oa_s, xpad_s, kmean_s, c_s, n_s, m_s, score_s, rowmax_s, kv_s,
                  *, ml_width, ml_heads, moba_width, mem_width, n_sel, gate_slots):
    t = pl.program_id(1)
    tm = x_ref.shape[0]
    pad = xpad_s.shape[0] - tm

    @pl.when(t == 0)
    def _():
        xpad_s[0:pad, :] = jnp.zeros((pad, ml_width), F32)
        kmean_s[...] = jnp.zeros(kmean_s.shape, F32)
        c_s[...] = jnp.zeros(c_s.shape, F32)
        n_s[...] = jnp.zeros(n_s.shape, F32)
        m_s[...] = jnp.zeros(m_s.shape, F32)
        kv_s[...] = _dot(mem_ref[...].astype(BF16), wkv_ref[...]).astype(BF16)

    @pl.when(t > 0)
    def _():
        xpad_s[0:pad, :] = xpad_s[tm:tm + pad, :]

    rows = pl.ds(pl.multiple_of(t * tm, tm), tm)
    _proj_stage(x_ref, pos_ref, win_ref, convw_ref, convb_ref, wq_ref, wk_ref, wv_ref,
                wg_ref, bg_ref, normg_ref, skip_ref, invf_ref,
                qm_s, kmt_s, vm_s, add_s, mul_s, gcol_s, grow_s,
                qa_s, ka_s.at[rows], va_s.at[rows], sza_s, qc_s, szc_s, xpad_s, kmean_s,
                t=t, ml_width=ml_width, ml_heads=ml_heads, moba_width=moba_width,
                mem_width=mem_width, n_sel=n_sel, gate_slots=gate_slots)
    _mlstm_stage(qm_s, kmt_s, vm_s, gcol_s, grow_s, add_s, mul_s, om_s, c_s, n_s, m_s, heads=ml_heads)
    _moba_stage(qa_s, ka_s, va_s, sza_s, oa_s, score_s, rowmax_s, j=t,
                then=functools.partial(_mix_stage, x_ref, om_s, oa_s, qc_s, szc_s, kv_s,
                                       wo_ref, lng_ref, lnb_ref, y_ref))


def _layer_call(x, pos, mem, win, convw, convb, wq, wk, wv, wg, bg, normg, skip, invf, wkv, wo, lng, lnb,
                *, ml_width, ml_heads, moba_width, mem_width):
    bsz, s, d = x.shape
    m_tok = mem.shape[1]
    tm = TOKEN_TILE
    nt = s // tm
    n_sel = min(MOBA_TOP_K, nt - 1)
    gate_slots = max(V7X_SUBLANES, pl.next_power_of_2(nt))
    assert MOBA_HEADS * gate_slots <= V7X_LANES and s % tm == 0 and ml_heads >= 2
    hd = ml_width // ml_heads
    tok = lambda w: pl.BlockSpec((None, tm, w), lambda b, t: (b, t, 0))

    def resident(a):
        nd = a.ndim
        return pl.BlockSpec(a.shape, lambda b, t, _nd=nd: (0,) * _nd, pipeline_mode=pl.Buffered(1))

    weights = (win, convw, convb, wq, wk, wv, wg, bg, normg, skip, invf, wkv, wo, lng, lnb)
    in_specs = ([tok(d),
                 pl.BlockSpec((None, None, 1, tm), lambda b, t: (b, t, 0, 0)),
                 pl.BlockSpec((None, m_tok, d), lambda b, t: (b, 0, 0))]
                + [resident(a) for a in weights])
    vmem = pltpu.VMEM
    score_heads = MOBA_HEADS
    scratch = [
        vmem((tm, ml_width), BF16),
        vmem((ml_width, tm), BF16),
        vmem((tm, ml_width), BF16),
        vmem((tm, ml_width), BF16),
        vmem((tm, ml_width), BF16),
        vmem((tm, V7X_LANES), F32),
        vmem((GATE_ROWS, tm), F32),
        vmem((tm, 2 * moba_width), BF16),
        vmem((tm, moba_width), BF16),
        vmem((tm, mem_width), BF16),
        vmem((tm, mem_width), BF16),
        vmem((s, moba_width), BF16),
        vmem((s, moba_width), BF16),
        vmem((tm, ml_width), BF16),
        vmem((tm, moba_width), BF16),
        vmem((tm + V7X_SUBLANES, ml_width), F32),
        vmem((V7X_LANES, moba_width), F32),
        vmem((ml_heads, hd, hd), F32),
        vmem((ml_heads, hd, V7X_LANES), F32),
        vmem((ml_heads, V7X_SUBLANES, V7X_LANES), F32),
        vmem((score_heads, nt, tm, MOBA_BLOCK), F32),
        vmem((score_heads, tm, MOBA_BLOCK // 2), F32),
        vmem((m_tok, 2 * mem_width), BF16),
    ]
    kern = functools.partial(_layer_kernel, ml_width=ml_width, ml_heads=ml_heads, moba_width=moba_width,
                             mem_width=mem_width, n_sel=n_sel, gate_slots=gate_slots)
    return pl.pallas_call(
        kern,
        grid=(bsz, nt),
        in_specs=in_specs,
        out_specs=tok(d),
        out_shape=jax.ShapeDtypeStruct((bsz, s, d), x.dtype),
        scratch_shapes=scratch,
        compiler_params=pltpu.CompilerParams(dimension_semantics=("arbitrary", "arbitrary"),
                                             vmem_limit_bytes=V7X_VMEM_LIMIT_BYTES),
        name="layer",
    )(x, pos.reshape(bsz, nt, 1, tm), mem, *weights)


def _diag_tiles(w, heads):
    groups, blk, _ = w.shape
    hd = groups // heads * blk
    rows = w.reshape(heads, hd, blk)
    idx = jnp.arange(hd)
    same_group = (idx[:, None] // blk) == (idx[None, :] // blk)
    return jnp.where(same_group, jnp.tile(rows, (1, 1, hd // blk)), 0.0)


def kernel(x, mem, positions, w_in, mlstm_conv_w, mlstm_conv_b, mlstm_wq, mlstm_wk, mlstm_wv, mlstm_w_gates, mlstm_b_gates, mlstm_norm_g, mlstm_skip, w_mem_kv, w_out, ln_g, ln_b):
    bsz, s, d = x.shape
    ml_width = mlstm_conv_w.shape[1]
    ml_heads = mlstm_b_gates.shape[0] // 2
    mem_width = w_mem_kv.shape[1] // 2
    moba_width = (w_in.shape[1] - 2 * ml_width - 2 * mem_width) // 4
    moba_hd = moba_width // MOBA_HEADS
    assert moba_hd == V7X_LANES and mem_width // MEM_HEADS == V7X_LANES and MOBA_HEADS == 4
    assert s % TOKEN_TILE == 0

    row = lambda a: a.reshape(1, -1).astype(F32)
    wq = _diag_tiles(mlstm_wq, ml_heads).astype(BF16)
    wk = _diag_tiles(mlstm_wk, ml_heads).astype(BF16)
    wv = _diag_tiles(mlstm_wv, ml_heads).astype(BF16)
    n_gates = mlstm_w_gates.shape[1]
    wg = jnp.pad(mlstm_w_gates, ((0, 0), (0, V7X_LANES - n_gates))).astype(BF16)
    bg = jnp.pad(mlstm_b_gates, (0, V7X_LANES - n_gates)).reshape(1, -1).astype(F32)
    half = moba_hd // 2
    inv_freq = ROPE_THETA ** (-jnp.arange(half, dtype=F32) * 2.0 / moba_hd)
    invf = inv_freq.reshape(-1, 1)
    pos = positions.astype(F32)

    return _layer_call(
        x, pos, mem, w_in.astype(BF16), mlstm_conv_w.astype(F32), row(mlstm_conv_b), wq, wk, wv, wg, bg,
        row(mlstm_norm_g), row(mlstm_skip), invf, w_mem_kv.astype(BF16), w_out.astype(BF16),
        row(ln_g), row(ln_b),
        ml_width=ml_width, ml_heads=ml_heads, moba_width=moba_width, mem_width=mem_width)
```

```python
import functools

import jax
import jax.numpy as jnp
from jax import lax
from jax.experimental import pallas as pl
from jax.experimental.pallas import tpu as pltpu

MOBA_HEADS = 4
MOBA_BLOCK = 256
MOBA_TOP_K = 3
MEM_HEADS = 4
ROPE_THETA = 10000.0
DEPTH = 1
DEEPNORM_ALPHA = (2 * DEPTH) ** 0.25
LN_EPS = 1e-5

V7X_LANES = 128
V7X_SUBLANES = 8
V7X_VMEM_BYTES = 64 * 1024 * 1024
V7X_VMEM_LIMIT_BYTES = V7X_VMEM_BYTES - 8 * 1024 * 1024

TOKEN_TILE = MOBA_BLOCK
MASK_VALUE = -1e30
GATE_ROWS = 2 * V7X_SUBLANES

F32 = jnp.float32
BF16 = jnp.bfloat16


def _dot(a, b):
    return jnp.dot(a, b, preferred_element_type=F32)


def _dot_nt(a, b):
    return lax.dot_general(a, b, (((1,), (1,)), ((), ())), preferred_element_type=F32)


def _lane_scan(v, op, identity, lane_idx):
    shift = 1
    while shift < v.shape[-1]:
        v = op(v, jnp.where(lane_idx >= shift, pltpu.roll(v, shift, v.ndim - 1), identity))
        shift *= 2
    return v


def _silu(v):
    return v * jax.nn.sigmoid(v)


def _log_sigmoid(v):
    return jnp.minimum(v, 0.0) - jnp.log1p(jnp.exp(-jnp.abs(v)))


def _proj_stage(x_ref, pos_ref, win_ref, convw_ref, convb_ref, wq_ref, wk_ref, wv_ref,
                wg_ref, bg_ref, normg_ref, skip_ref, invf_ref,
                qm_ref, kmt_ref, vm_ref, add_ref, mul_ref, gcol_ref, grow_ref,
                qa_ref, ka_ref, va_ref, sza_ref, qc_ref, szc_ref,
                xpad_scr, kmean_scr,
                *, t, ml_width, ml_heads, moba_width, mem_width, n_sel, gate_slots):
    tm = x_ref.shape[0]
    hd = ml_width // ml_heads
    moba_hd = moba_width // MOBA_HEADS
    mem_hd = mem_width // MEM_HEADS
    c_zm = ml_width
    c_qa = 2 * ml_width
    c_ka = c_qa + moba_width
    c_va = c_ka + moba_width
    c_za = c_va + moba_width
    c_qc = c_za + moba_width
    c_zc = c_qc + mem_width

    xb = x_ref[...].astype(BF16)
    pad = xpad_scr.shape[0] - tm
    k_w = convw_ref.shape[0]
    nb_lanes = gate_slots

    def x_cols(c0, width):
        return _dot(xb, win_ref[:, c0:c0 + width])

    def rotary_pair(p, qp, kp):
        rots, means = [], []
        for i in range(2):
            h = 2 * p + i
            ls = slice(i * moba_hd, (i + 1) * moba_hd)
            q_rot = qp[:, ls] * cosf + pltpu.roll(qp[:, ls], moba_hd // 2, 1) * sins
            k_rot = kp[:, ls] * cosf + pltpu.roll(kp[:, ls], moba_hd // 2, 1) * sins
            qa_ref[:, 2 * h * moba_hd:(2 * h + 1) * moba_hd] = (q_rot * (moba_hd ** -0.5)).astype(BF16)
            ka_ref[:, h * moba_hd:(h + 1) * moba_hd] = k_rot.astype(BF16)
            rots.append(q_rot)
            means.append(jnp.mean(k_rot, axis=0, keepdims=True))
        return rots, means

    def conv_gate_head(h, xm_h, zm_h):
        hs = slice(h * hd, (h + 1) * hd)
        xpad_scr[pad:pad + tm, hs] = xm_h
        conv = convb_ref[:, hs] + xm_h * convw_ref[k_w - 1:k_w, hs]
        for j in range(k_w - 1):
            conv = conv + xpad_scr[pl.ds(pad - (k_w - 1) + j, tm), hs] * convw_ref[j:j + 1, hs]
        xc_h = _silu(conv)
        sz_h = _silu(zm_h)
        add_ref[:, hs] = (skip_ref[:, hs] * xc_h * sz_h).astype(BF16)
        mul_ref[:, hs] = (normg_ref[:, hs] * sz_h).astype(BF16)
        return xc_h.astype(BF16), xm_h.astype(BF16)

    def blockdiag_head(h, xc_b, xm_b):
        hs = slice(h * hd, (h + 1) * hd)
        q_h = _dot(xc_b, wq_ref[h]).astype(BF16)
        k_f = _dot(xc_b, wk_ref[h])
        v_h = _dot(xm_b, wv_ref[h]).astype(BF16)
        qm_ref[:, hs] = q_h
        vm_ref[:, hs] = v_h
        kmt_ref[hs, :] = (jnp.transpose(k_f) * (hd ** -0.5)).astype(BF16)
        return q_h, k_f.astype(BF16), v_h

    def gate_terms(h, q_h, k_h, v_h):
        hs = slice(h * hd, (h + 1) * hd)
        return (_dot(q_h, wg_ref[hs, :])
                + _dot(k_h, wg_ref[ml_width + h * hd:ml_width + (h + 1) * hd, :])
                + _dot(v_h, wg_ref[2 * ml_width + h * hd:2 * ml_width + (h + 1) * hd, :]))

    def select_blocks(gate_t):
        n_rows = MOBA_HEADS * nb_lanes
        gate = gate_t[0:n_rows, :]
        row = lax.broadcasted_iota(jnp.int32, gate.shape, 0)
        n_r = row & (nb_lanes - 1)
        valid = n_r < t
        gate = jnp.where(valid, gate, -jnp.inf)
        cnt = jnp.zeros(gate.shape, jnp.int32)
        for r in range(1, nb_lanes):
            up = pltpu.roll(gate, r, 0)
            cnt = cnt + ((n_r >= r) & (up >= gate)).astype(jnp.int32)
            dn = pltpu.roll(gate, n_rows - r, 0)
            cnt = cnt + ((n_r < nb_lanes - r) & (dn > gate)).astype(jnp.int32)
        keep = (valid & (cnt < n_sel)) | (n_r == t)
        selb_t = jnp.where(keep, 0.0, MASK_VALUE)
        selb = jnp.transpose(jnp.concatenate(
            [selb_t, jnp.full((V7X_LANES - n_rows, tm), MASK_VALUE, F32)], axis=0))
        for h in range(MOBA_HEADS):
            shift = (V7X_LANES - h * nb_lanes) % V7X_LANES
            sel_h = selb if shift == 0 else pltpu.roll(selb, shift, 1)
            qa_ref[:, (2 * h + 1) * moba_hd:(2 * h + 2) * moba_hd] = sel_h.astype(BF16)

    pw = 2 * moba_hd
    qa0, ka0 = x_cols(c_qa, pw), x_cols(c_ka, pw)
    ang_t = invf_ref[...] * pos_ref[...]
    cos_t = jnp.cos(ang_t)
    sin_t = jnp.sin(ang_t)
    cosf = jnp.transpose(jnp.concatenate([cos_t, cos_t], axis=0))
    sins = jnp.transpose(jnp.concatenate([-sin_t, sin_t], axis=0))
    qa1, ka1 = x_cols(c_qa + pw, pw), x_cols(c_ka + pw, pw)
    rots0, means0 = rotary_pair(0, qa0, ka0)
    xm0, zm0 = x_cols(0, hd), x_cols(c_zm, hd)
    rots1, means1 = rotary_pair(1, qa1, ka1)
    q_rots, kmean_rows = rots0 + rots1, means0 + means1
    q_all = jnp.concatenate(q_rots, axis=1)
    km = kmean_scr[...]
    q_hi = q_all.astype(BF16)
    q_lo = (q_all - q_hi.astype(F32)).astype(BF16)
    k_hi = km.astype(BF16)
    k_lo = (km - k_hi.astype(F32)).astype(BF16)
    xz = {0: (xm0, zm0), 1: (x_cols(hd, hd), x_cols(c_zm + hd, hd))}
    gate = _dot_nt(k_hi, q_hi) + _dot_nt(k_hi, q_lo) + _dot_nt(k_lo, q_hi)
    pending = [("xz", h) for h in range(2, ml_heads)] + [("va", None), ("za", None)]
    qkv = []
    va = za = None
    for h in range(ml_heads):
        xc_b, xm_b = conv_gate_head(h, *xz[h])
        if pending:
            kind, arg = pending.pop(0)
            if kind == "xz":
                xz[arg] = (x_cols(arg * hd, hd), x_cols(c_zm + arg * hd, hd))
            elif kind == "va":
                va = x_cols(c_va, moba_width).astype(BF16)
            else:
                za = x_cols(c_za, moba_width)
        qkv.append(blockdiag_head(h, xc_b, xm_b))
    if va is None:
        va = x_cols(c_va, moba_width).astype(BF16)
    if za is None:
        za = x_cols(c_za, moba_width)
    g = jnp.zeros((tm, V7X_LANES), F32) + bg_ref[...]
    for h in range(ml_heads):
        g = g + gate_terms(h, *qkv[h])
    sza_ref[...] = _silu(za).astype(BF16)
    qc = x_cols(c_qc, mem_width)
    zc = x_cols(c_zc, mem_width)

    nrow = grow_ref.shape[0]
    gt = jnp.transpose(g)[0:V7X_SUBLANES, :]
    sub = lax.broadcasted_iota(jnp.int32, gt.shape, 0)
    tok_i = lax.broadcasted_iota(jnp.int32, gt.shape, 1)
    b = _lane_scan(jnp.where(sub >= ml_heads, _log_sigmoid(gt), 0.0), jnp.add, 0.0, tok_i)
    a = _lane_scan(jnp.where(sub >= ml_heads, pltpu.roll(gt, ml_heads, 0) - b, -jnp.inf),
                   jnp.maximum, -jnp.inf, tok_i)
    rows = jnp.concatenate([jnp.where(sub < ml_heads, gt, b), pltpu.roll(a, ml_heads, 0)], axis=0)
    grow_ref[...] = rows
    gcol_ref[...] = jnp.transpose(
        jnp.concatenate([rows, jnp.zeros((V7X_LANES - nrow, tm), F32)], axis=0))

    km_row = lax.broadcasted_iota(jnp.int32, kmean_scr.shape, 0)
    km_head = lax.broadcasted_iota(jnp.int32, kmean_scr.shape, 1) // moba_hd
    kmean_scr[...] = jnp.where(km_row == km_head * nb_lanes + t,
                               jnp.concatenate(kmean_rows, axis=1), kmean_scr[...])
    select_blocks(gate)
    va_ref[...] = va
    qc_ref[...] = (qc * (mem_hd ** -0.5)).astype(BF16)
    szc_ref[...] = _silu(zc).astype(BF16)


def _mlstm_stage(qm_ref, kmt_ref, vm_ref, gcol_ref, grow_ref, add_ref, mul_ref, out_ref,
                 c_scr, n_scr, m_scr, *, heads):
    L = qm_ref.shape[0]
    hd = qm_ref.shape[1] // heads
    row_i = lax.broadcasted_iota(jnp.int32, (L, L), 0)
    col_i = lax.broadcasted_iota(jnp.int32, (L, L), 1)
    causal = row_i >= col_i
    gcol = gcol_ref[...]
    grow = grow_ref[...]
    ones = jnp.ones((L, V7X_LANES), BF16)
    head_slices = [slice(h * hd, (h + 1) * hd) for h in range(heads)]

    qk, inter, qn, mm_l, r_l, w_inter_l = [], [], [], [], [], []
    for h, hs in enumerate(head_slices):
        q = qm_ref[:, hs]
        kt = kmt_ref[hs, :]
        v = vm_ref[:, hs]
        i_row = grow[h:h + 1, :]
        b_row = grow[heads + h:heads + h + 1, :]
        a_col = gcol[:, 2 * heads + h:2 * heads + h + 1]
        m_prev = m_scr[h][0:1, 0:1]
        c_prev = c_scr[h]
        n_prev = n_scr[h]
        qk.append(_dot(q, kt))
        inter.append(_dot(q, c_prev.astype(BF16)))
        qn.append(_dot(q, n_prev.astype(BF16)))
        r_row = i_row - b_row
        mm = jnp.maximum(m_prev, a_col)
        r_l.append(r_row)
        mm_l.append(mm)
        w_inter_l.append(jnp.exp(m_prev - mm))
        b_end = b_row[:, L - 1:L]
        log_w = b_end + r_row
        m_new = jnp.maximum(b_end + m_prev, jnp.max(log_w, axis=1, keepdims=True))
        decay = jnp.exp(b_end + m_prev - m_new)
        kw = (kt.astype(F32) * jnp.exp(log_w - m_new)).astype(BF16)
        c_scr[h] = decay * c_prev + _dot(kw, v)
        n_scr[h] = decay * n_prev + _dot(kw, ones)
        m_scr[h] = jnp.broadcast_to(m_new, m_scr.shape[1:])

    s_qk = [(qk[h] * jnp.exp(jnp.where(causal, r_l[h] - mm_l[h], -jnp.inf))).astype(BF16)
            for h in range(heads)]
    pv = [_dot(s_qk[h], vm_ref[:, hs]) for h, hs in enumerate(head_slices)]
    rs = [_dot(s_qk[h], ones) for h in range(heads)]

    for h, hs in enumerate(head_slices):
        b_col = gcol[:, heads + h:heads + h + 1]
        num = w_inter_l[h] * inter[h] + pv[h]
        den = w_inter_l[h] * qn[h] + rs[h]
        rec = 1.0 / jnp.maximum(jnp.abs(den), jnp.exp(-(b_col + mm_l[h])))
        hh = num * jnp.concatenate([rec] * (hd // V7X_LANES), axis=1)
        mu = jnp.mean(hh, axis=1, keepdims=True)
        hc = hh - mu
        var = jnp.mean(hc * hc, axis=1, keepdims=True)
        hn = hc * lax.rsqrt(var + LN_EPS)
        out_ref[:, hs] = (hn * mul_ref[:, hs].astype(F32) + add_ref[:, hs].astype(F32)).astype(BF16)


def _moba_stage(q_ref, k_ref, v_ref, sz_ref, out_ref, s_scr, m_scr, *, j):
    tq = q_ref.shape[0]
    bs = MOBA_BLOCK
    nblk = k_ref.shape[0] // bs
    aw = q_ref.shape[1] // MOBA_HEADS
    hd = aw // 2
    half = bs // 2
    heads = [slice(h * aw, (h + 1) * aw) for h in range(MOBA_HEADS)]
    kv_heads = [slice(h * hd, (h + 1) * hd) for h in range(MOBA_HEADS)]
    tag_lane = lax.broadcasted_iota(jnp.int32, (bs, hd), 1)
    ones_col = (tag_lane == 0).astype(BF16)

    n_slots = s_scr.shape[0]

    def attend(own):
        causal = (lax.broadcasted_iota(jnp.int32, (tq, bs), 0)
                  >= lax.broadcasted_iota(jnp.int32, (tq, bs), 1))

        def score_pass(hi):
            slot = hi % n_slots
            mx = None
            for n in range(own + 1):
                k_aug = jnp.concatenate([k_ref[n * bs:(n + 1) * bs, kv_heads[hi]],
                                         (tag_lane == n).astype(BF16)], axis=1)
                s = _dot_nt(q_ref[:, heads[hi]], k_aug)
                if n == own:
                    s = jnp.where(causal, s, MASK_VALUE)
                s_scr[slot, n] = s
                sm = jnp.maximum(s[:, :half], s[:, half:])
                mx = sm if mx is None else jnp.maximum(mx, sm)
            m_scr[slot] = jnp.broadcast_to(jnp.max(mx, axis=1, keepdims=True), m_scr.shape[1:])

        def value_pass(hi):
            slot = hi % n_slots
            m = m_scr[slot]
            acc = None
            for n in range(own + 1):
                s = s_scr[slot, n]
                p = jnp.concatenate([jnp.exp(s[:, :half] - m), jnp.exp(s[:, half:] - m)], axis=1)
                v_aug = jnp.concatenate([v_ref[n * bs:(n + 1) * bs, kv_heads[hi]], ones_col], axis=1)
                pv = _dot(p.astype(BF16), v_aug)
                acc = pv if acc is None else acc + pv
            o = acc[:, :hd] * (1.0 / acc[:, hd:hd + 1])
            out_ref[:, hi * hd:(hi + 1) * hd] = (o * sz_ref[:, hi * hd:(hi + 1) * hd].astype(F32)).astype(BF16)

        ahead = min(n_slots, MOBA_HEADS)
        for hi in range(ahead):
            score_pass(hi)
        for hi in range(MOBA_HEADS):
            value_pass(hi)
            if hi + ahead < MOBA_HEADS:
                score_pass(hi + ahead)

    for own in range(nblk):
        pl.when(j == own)(functools.partial(attend, own))


def _mix_stage(x_ref, om_ref, oa_ref, qc_ref, szc_ref, kv_scr, wo_ref, lng_ref, lnb_ref, y_ref):
    width = qc_ref.shape[1]
    hd = width // MEM_HEADS
    w_m = om_ref.shape[1]
    w_a = oa_ref.shape[1]
    head_slices = [slice(h * hd, (h + 1) * hd) for h in range(MEM_HEADS)]
    scores = [_dot_nt(qc_ref[:, hs], kv_scr[:, hs]) for hs in head_slices]
    mixed = _dot(oa_ref[...], wo_ref[w_m:w_m + w_a, :])
    probs = [jnp.exp(s - jnp.max(s, axis=1, keepdims=True)) for s in scores]
    outs = [_dot(p.astype(BF16), kv_scr[:, width + h * hd:width + (h + 1) * hd])
            for h, p in enumerate(probs)]
    oc = jnp.concatenate(
        [(outs[h] * (1.0 / jnp.sum(probs[h], axis=1, keepdims=True))
          * szc_ref[:, hs].astype(F32)).astype(BF16) for h, hs in enumerate(head_slices)], axis=1)
    mixed = mixed + _dot(oc, wo_ref[w_m + w_a:, :])
    mixed = mixed + _dot(om_ref[...], wo_ref[0:w_m, :])
    y = DEEPNORM_ALPHA * x_ref[...] + mixed
    mu = jnp.mean(y, axis=1, keepdims=True)
    yc = y - mu
    var = jnp.mean(yc * yc, axis=1, keepdims=True)
    y_ref[...] = yc * lax.rsqrt(var + LN_EPS) * lng_ref[...] + lnb_ref[...]


def _layer_kernel(x_ref, pos_ref, mem_ref, win_ref, convw_ref, convb_ref, wq_ref, wk_ref, wv_ref,
                  wg_ref, bg_ref, normg_ref, skip_ref, invf_ref, wkv_ref, wo_ref, lng_ref, lnb_ref,
                  y_ref,
                  qm_s, kmt_s, vm_s, add_s, mul_s, gcol_s, grow_s, qa_s, sza_s, qc_s, szc_s,
                  ka_s, va_s, om_s, oa_s, xpad_s, kmean_s, c_s, n_s, m_s, score_s, rowmax_s, kv_s,
                  *, ml_width, ml_heads, moba_width, mem_width, n_sel, gate_slots):
    t = pl.program_id(1)
    tm = x_ref.shape[0]
    pad = xpad_s.shape[0] - tm

    @pl.when(t == 0)
    def _():
        xpad_s[0:pad, :] = jnp.zeros((pad, ml_width), F32)
        kmean_s[...] = jnp.zeros(kmean_s.shape, F32)
        c_s[...] = jnp.zeros(c_s.shape, F32)
        n_s[...] = jnp.zeros(n_s.shape, F32)
        m_s[...] = jnp.zeros(m_s.shape, F32)
        kv_s[...] = _dot(mem_ref[...].astype(BF16), wkv_ref[...]).astype(BF16)

    @pl.when(t > 0)
    def _():
        xpad_s[0:pad, :] = xpad_s[tm:tm + pad, :]

    rows = pl.ds(pl.multiple_of(t * tm, tm), tm)
    _proj_stage(x_ref, pos_ref, win_ref, convw_ref, convb_ref, wq_ref, wk_ref, wv_ref,
                wg_ref, bg_ref, normg_ref, skip_ref, invf_ref,
                qm_s, kmt_s, vm_s, add_s, mul_s, gcol_s, grow_s,
                qa_s, ka_s.at[rows], va_s.at[rows], sza_s, qc_s, szc_s, xpad_s, kmean_s,
                t=t, ml_width=ml_width, ml_heads=ml_heads, moba_width=moba_width,
                mem_width=mem_width, n_sel=n_sel, gate_slots=gate_slots)
    _mlstm_stage(qm_s, kmt_s, vm_s, gcol_s, grow_s, add_s, mul_s, om_s, c_s, n_s, m_s, heads=ml_heads)
    _moba_stage(qa_s, ka_s, va_s, sza_s, oa_s, score_s, rowmax_s, j=t)
    _mix_stage(x_ref, om_s, oa_s, qc_s, szc_s, kv_s, wo_ref, lng_ref, lnb_ref, y_ref)


def _layer_call(x, pos, mem, win, convw, convb, wq, wk, wv, wg, bg, normg, skip, invf, wkv, wo, lng, lnb,
                *, ml_width, ml_heads, moba_width, mem_width):
    bsz, s, d = x.shape
    m_tok = mem.shape[1]
    tm = TOKEN_TILE
    nt = s // tm
    n_sel = min(MOBA_TOP_K, nt - 1)
    gate_slots = max(V7X_SUBLANES, pl.next_power_of_2(nt))
    assert MOBA_HEADS * gate_slots <= V7X_LANES and s % tm == 0 and ml_heads >= 2
    hd = ml_width // ml_heads
    tok = lambda w: pl.BlockSpec((None, tm, w), lambda b, t: (b, t, 0))

    def resident(a):
        nd = a.ndim
        return pl.BlockSpec(a.shape, lambda b, t, _nd=nd: (0,) * _nd, pipeline_mode=pl.Buffered(1))

    weights = (win, convw, convb, wq, wk, wv, wg, bg, normg, skip, invf, wkv, wo, lng, lnb)
    in_specs = ([tok(d),
                 pl.BlockSpec((None, None, 1, tm), lambda b, t: (b, t, 0, 0)),
                 pl.BlockSpec((None, m_tok, d), lambda b, t: (b, 0, 0))]
                + [resident(a) for a in weights])
    vmem = pltpu.VMEM
    score_heads = MOBA_HEADS
    scratch = [
        vmem((tm, ml_width), BF16),
        vmem((ml_width, tm), BF16),
        vmem((tm, ml_width), BF16),
        vmem((tm, ml_width), BF16),
        vmem((tm, ml_width), BF16),
        vmem((tm, V7X_LANES), F32),
        vmem((GATE_ROWS, tm), F32),
        vmem((tm, 2 * moba_width), BF16),
        vmem((tm, moba_width), BF16),
        vmem((tm, mem_width), BF16),
        vmem((tm, mem_width), BF16),
        vmem((s, moba_width), BF16),
        vmem((s, moba_width), BF16),
        vmem((tm, ml_width), BF16),
        vmem((tm, moba_width), BF16),
        vmem((tm + V7X_SUBLANES, ml_width), F32),
        vmem((V7X_LANES, moba_width), F32),
        vmem((ml_heads, hd, hd), F32),
        vmem((ml_heads, hd, V7X_LANES), F32),
        vmem((ml_heads, V7X_SUBLANES, V7X_LANES), F32),
        vmem((score_heads, nt, tm, MOBA_BLOCK), F32),
        vmem((score_heads, tm, MOBA_BLOCK // 2), F32),
        vmem((m_tok, 2 * mem_width), BF16),
    ]
    kern = functools.partial(_layer_kernel, ml_width=ml_width, ml_heads=ml_heads, moba_width=moba_width,
                             mem_width=mem_width, n_sel=n_sel, gate_slots=gate_slots)
    return pl.pallas_call(
        kern,
        grid=(bsz, nt),
        in_specs=in_specs,
        out_specs=tok(d),
        out_shape=jax.ShapeDtypeStruct((bsz, s, d), x.dtype),
        scratch_shapes=scratch,
        compiler_params=pltpu.CompilerParams(dimension_semantics=("arbitrary", "arbitrary"),
                                             vmem_limit_bytes=V7X_VMEM_LIMIT_BYTES),
        name="layer",
    )(x, pos.reshape(bsz, nt, 1, tm), mem, *weights)


def _diag_tiles(w, heads):
    groups, blk, _ = w.shape
    hd = groups // heads * blk
    rows = w.reshape(heads, hd, blk)
    idx = jnp.arange(hd)
    same_group = (idx[:, None] // blk) == (idx[None, :] // blk)
    return jnp.where(same_group, jnp.tile(rows, (1, 1, hd // blk)), 0.0)


def kernel(x, mem, positions, w_in, mlstm_conv_w, mlstm_conv_b, mlstm_wq, mlstm_wk, mlstm_wv, mlstm_w_gates, mlstm_b_gates, mlstm_norm_g, mlstm_skip, w_mem_kv, w_out, ln_g, ln_b):
    bsz, s, d = x.shape
    ml_width = mlstm_conv_w.shape[1]
    ml_heads = mlstm_b_gates.shape[0] // 2
    mem_width = w_mem_kv.shape[1] // 2
    moba_width = (w_in.shape[1] - 2 * ml_width - 2 * mem_width) // 4
    moba_hd = moba_width // MOBA_HEADS
    assert moba_hd == V7X_LANES and mem_width // MEM_HEADS == V7X_LANES and MOBA_HEADS == 4
    assert s % TOKEN_TILE == 0

    row = lambda a: a.reshape(1, -1).astype(F32)
    wq = _diag_tiles(mlstm_wq, ml_heads).astype(BF16)
    wk = _diag_tiles(mlstm_wk, ml_heads).astype(BF16)
    wv = _diag_tiles(mlstm_wv, ml_heads).astype(BF16)
    n_gates = mlstm_w_gates.shape[1]
    wg = jnp.pad(mlstm_w_gates, ((0, 0), (0, V7X_LANES - n_gates))).astype(BF16)
    bg = jnp.pad(mlstm_b_gates, (0, V7X_LANES - n_gates)).reshape(1, -1).astype(F32)
    half = moba_hd // 2
    inv_freq = ROPE_THETA ** (-jnp.arange(half, dtype=F32) * 2.0 / moba_hd)
    invf = inv_freq.reshape(-1, 1)
    pos = positions.astype(F32)

    return _layer_call(
        x, pos, mem, w_in.astype(BF16), mlstm_conv_w.astype(F32), row(mlstm_conv_b), wq, wk, wv, wg, bg,
        row(mlstm_norm_g), row(mlstm_skip), invf, w_mem_kv.astype(BF16), w_out.astype(BF16),
        row(ln_g), row(ln_b),
        ml_width=ml_width, ml_heads=ml_heads, moba_width=moba_width, mem_width=mem_width)
```

```python
import functools

import jax
import jax.numpy as jnp
from jax import lax
from jax.experimental import pallas as pl
from jax.experimental.pallas import tpu as pltpu

MOBA_HEADS = 4
MOBA_BLOCK = 256
MOBA_TOP_K = 3
MEM_HEADS = 4
ROPE_THETA = 10000.0
DEPTH = 1
DEEPNORM_ALPHA = (2 * DEPTH) ** 0.25
LN_EPS = 1e-5

V7X_LANES = 128
V7X_SUBLANES = 8
V7X_VMEM_BYTES = 64 * 1024 * 1024
V7X_VMEM_LIMIT_BYTES = V7X_VMEM_BYTES - 8 * 1024 * 1024

TOKEN_TILE = MOBA_BLOCK
MASK_VALUE = -1e30
GATE_ROWS = 2 * V7X_SUBLANES

F32 = jnp.float32
BF16 = jnp.bfloat16


def _dot(a, b):
    return jnp.dot(a, b, preferred_element_type=F32)


def _dot_nt(a, b):
    return lax.dot_general(a, b, (((1,), (1,)), ((), ())), preferred_element_type=F32)


def _lane_scan(v, op, identity, lane_idx):
    shift = 1
    while shift < v.shape[-1]:
        v = op(v, jnp.where(lane_idx >= shift, pltpu.roll(v, shift, v.ndim - 1), identity))
        shift *= 2
    return v


def _silu(v):
    return v * jax.nn.sigmoid(v)


def _log_sigmoid(v):
    return jnp.minimum(v, 0.0) - jnp.log1p(jnp.exp(-jnp.abs(v)))


def _proj_stage(x_ref, pos_ref, win_ref, convw_ref, convb_ref, wq_ref, wk_ref, wv_ref,
                wg_ref, bg_ref, normg_ref, skip_ref, invf_ref,
                qm_ref, kmt_ref, vm_ref, add_ref, mul_ref, gcol_ref, grow_ref,
                qa_ref, ka_ref, vat_ref, sza_ref, qc_ref, szc_ref,
                xpad_scr, kmean_scr,
                *, t, ml_width, ml_heads, moba_width, mem_width, n_sel, gate_slots):
    tm = x_ref.shape[0]
    hd = ml_width // ml_heads
    moba_hd = moba_width // MOBA_HEADS
    mem_hd = mem_width // MEM_HEADS
    c_zm = ml_width
    c_qa = 2 * ml_width
    c_ka = c_qa + moba_width
    c_va = c_ka + moba_width
    c_za = c_va + moba_width
    c_qc = c_za + moba_width
    c_zc = c_qc + mem_width

    xb = x_ref[...].astype(BF16)
    pad = xpad_scr.shape[0] - tm
    k_w = convw_ref.shape[0]
    nb_lanes = gate_slots

    def x_cols(c0, width):
        return _dot(xb, win_ref[:, c0:c0 + width])

    def rotary_pair(p, qp, kp):
        rots, means = [], []
        for i in range(2):
            h = 2 * p + i
            ls = slice(i * moba_hd, (i + 1) * moba_hd)
            q_rot = qp[:, ls] * cosf + pltpu.roll(qp[:, ls], moba_hd // 2, 1) * sins
            k_rot = kp[:, ls] * cosf + pltpu.roll(kp[:, ls], moba_hd // 2, 1) * sins
            qa_ref[:, 2 * h * moba_hd:(2 * h + 1) * moba_hd] = (q_rot * (moba_hd ** -0.5)).astype(BF16)
            ka_ref[:, h * moba_hd:(h + 1) * moba_hd] = k_rot.astype(BF16)
            rots.append(q_rot)
            means.append(jnp.mean(k_rot, axis=0, keepdims=True))
        return rots, means

    def conv_gate_head(h, xm_h, zm_h):
        hs = slice(h * hd, (h + 1) * hd)
        xpad_scr[pad:pad + tm, hs] = xm_h
        conv = convb_ref[:, hs] + xm_h * convw_ref[k_w - 1:k_w, hs]
        for j in range(k_w - 1):
            conv = conv + xpad_scr[pl.ds(pad - (k_w - 1) + j, tm), hs] * convw_ref[j:j + 1, hs]
        xc_h = _silu(conv)
        sz_h = _silu(zm_h)
        add_ref[:, hs] = (skip_ref[:, hs] * xc_h * sz_h).astype(BF16)
        mul_ref[:, hs] = (normg_ref[:, hs] * sz_h).astype(BF16)
        return xc_h.astype(BF16), xm_h.astype(BF16)

    def blockdiag_head(h, xc_b, xm_b):
        hs = slice(h * hd, (h + 1) * hd)
        q_h = _dot(xc_b, wq_ref[h]).astype(BF16)
        k_f = _dot(xc_b, wk_ref[h])
        v_h = _dot(xm_b, wv_ref[h]).astype(BF16)
        qm_ref[:, hs] = q_h
        vm_ref[:, hs] = v_h
        kmt_ref[hs, :] = (jnp.transpose(k_f) * (hd ** -0.5)).astype(BF16)
        return q_h, k_f.astype(BF16), v_h

    def gate_terms(h, q_h, k_h, v_h):
        hs = slice(h * hd, (h + 1) * hd)
        return (_dot(q_h, wg_ref[hs, :])
                + _dot(k_h, wg_ref[ml_width + h * hd:ml_width + (h + 1) * hd, :])
                + _dot(v_h, wg_ref[2 * ml_width + h * hd:2 * ml_width + (h + 1) * hd, :]))

    def select_blocks(gate_t):
        n_rows = MOBA_HEADS * nb_lanes
        gate = gate_t[0:n_rows, :]
        row = lax.broadcasted_iota(jnp.int32, gate.shape, 0)
        n_r = row & (nb_lanes - 1)
        valid = n_r < t
        gate = jnp.where(valid, gate, -jnp.inf)
        cnt = jnp.zeros(gate.shape, jnp.int32)
        for r in range(1, nb_lanes):
            up = pltpu.roll(gate, r, 0)
            cnt = cnt + ((n_r >= r) & (up >= gate)).astype(jnp.int32)
            dn = pltpu.roll(gate, n_rows - r, 0)
            cnt = cnt + ((n_r < nb_lanes - r) & (dn > gate)).astype(jnp.int32)
        keep = (valid & (cnt < n_sel)) | (n_r == t)
        selb_t = jnp.where(keep, 0.0, MASK_VALUE)
        selb = jnp.transpose(jnp.concatenate(
            [selb_t, jnp.full((V7X_LANES - n_rows, tm), MASK_VALUE, F32)], axis=0))
        for h in range(MOBA_HEADS):
            shift = (V7X_LANES - h * nb_lanes) % V7X_LANES
            sel_h = selb if shift == 0 else pltpu.roll(selb, shift, 1)
            qa_ref[:, (2 * h + 1) * moba_hd:(2 * h + 2) * moba_hd] = sel_h.astype(BF16)

    pw = 2 * moba_hd
    qa0, ka0 = x_cols(c_qa, pw), x_cols(c_ka, pw)
    ang_t = invf_ref[...] * pos_ref[...]
    cos_t = jnp.cos(ang_t)
    sin_t = jnp.sin(ang_t)
    cosf = jnp.transpose(jnp.concatenate([cos_t, cos_t], axis=0))
    sins = jnp.transpose(jnp.concatenate([-sin_t, sin_t], axis=0))
    qa1, ka1 = x_cols(c_qa + pw, pw), x_cols(c_ka + pw, pw)
    rots0, means0 = rotary_pair(0, qa0, ka0)
    xm0, zm0 = x_cols(0, hd), x_cols(c_zm, hd)
    rots1, means1 = rotary_pair(1, qa1, ka1)
    q_rots, kmean_rows = rots0 + rots1, means0 + means1
    q_all = jnp.concatenate(q_rots, axis=1)
    km = kmean_scr[...]
    q_hi = q_all.astype(BF16)
    q_lo = (q_all - q_hi.astype(F32)).astype(BF16)
    k_hi = km.astype(BF16)
    k_lo = (km - k_hi.astype(F32)).astype(BF16)
    xz = {0: (xm0, zm0), 1: (x_cols(hd, hd), x_cols(c_zm + hd, hd))}
    gate = _dot_nt(k_hi, q_hi) + _dot_nt(k_hi, q_lo) + _dot_nt(k_lo, q_hi)
    pending = [("xz", h) for h in range(2, ml_heads)] + [("va", None), ("za", None)]
    qkv = []
    va = za = None
    for h in range(ml_heads):
        xc_b, xm_b = conv_gate_head(h, *xz[h])
        if pending:
            kind, arg = pending.pop(0)
            if kind == "xz":
                xz[arg] = (x_cols(arg * hd, hd), x_cols(c_zm + arg * hd, hd))
            elif kind == "va":
                va = x_cols(c_va, moba_width)
            else:
                za = x_cols(c_za, moba_width)
        qkv.append(blockdiag_head(h, xc_b, xm_b))
    if va is None:
        va = x_cols(c_va, moba_width)
    if za is None:
        za = x_cols(c_za, moba_width)
    g = jnp.zeros((tm, V7X_LANES), F32) + bg_ref[...]
    for h in range(ml_heads):
        g = g + gate_terms(h, *qkv[h])
    sza_ref[...] = _silu(za).astype(BF16)
    qc = x_cols(c_qc, mem_width)
    zc = x_cols(c_zc, mem_width)

    nrow = grow_ref.shape[0]
    gt = jnp.transpose(g)[0:V7X_SUBLANES, :]
    sub = lax.broadcasted_iota(jnp.int32, gt.shape, 0)
    tok_i = lax.broadcasted_iota(jnp.int32, gt.shape, 1)
    b = _lane_scan(jnp.where(sub >= ml_heads, _log_sigmoid(gt), 0.0), jnp.add, 0.0, tok_i)
    a = _lane_scan(jnp.where(sub >= ml_heads, pltpu.roll(gt, ml_heads, 0) - b, -jnp.inf),
                   jnp.maximum, -jnp.inf, tok_i)
    rows = jnp.concatenate([jnp.where(sub < ml_heads, gt, b), pltpu.roll(a, ml_heads, 0)], axis=0)
    grow_ref[...] = rows
    gcol_ref[...] = jnp.transpose(
        jnp.concatenate([rows, jnp.zeros((V7X_LANES - nrow, tm), F32)], axis=0))

    km_row = lax.broadcasted_iota(jnp.int32, kmean_scr.shape, 0)
    km_head = lax.broadcasted_iota(jnp.int32, kmean_scr.shape, 1) // moba_hd
    kmean_scr[...] = jnp.where(km_row == km_head * nb_lanes + t,
                               jnp.concatenate(kmean_rows, axis=1), kmean_scr[...])
    select_blocks(gate)
    vat_ref[...] = jnp.transpose(va).astype(BF16)
    qc_ref[...] = (qc * (mem_hd ** -0.5)).astype(BF16)
    szc_ref[...] = _silu(zc).astype(BF16)


def _mlstm_stage(qm_ref, kmt_ref, vm_ref, gcol_ref, grow_ref, add_ref, mul_ref, out_ref,
                 c_scr, n_scr, m_scr, *, heads):
    L = qm_ref.shape[0]
    hd = qm_ref.shape[1] // heads
    row_i = lax.broadcasted_iota(jnp.int32, (L, L), 0)
    col_i = lax.broadcasted_iota(jnp.int32, (L, L), 1)
    causal = row_i >= col_i
    gcol = gcol_ref[...]
    grow = grow_ref[...]
    ones = jnp.ones((L, V7X_LANES), BF16)
    head_slices = [slice(h * hd, (h + 1) * hd) for h in range(heads)]

    qk, inter, qn, mm_l, r_l, w_inter_l = [], [], [], [], [], []
    for h, hs in enumerate(head_slices):
        q = qm_ref[:, hs]
        kt = kmt_ref[hs, :]
        v = vm_ref[:, hs]
        i_row = grow[h:h + 1, :]
        b_row = grow[heads + h:heads + h + 1, :]
        a_col = gcol[:, 2 * heads + h:2 * heads + h + 1]
        m_prev = m_scr[h][0:1, 0:1]
        c_prev = c_scr[h]
        n_prev = n_scr[h]
        qk.append(_dot(q, kt))
        inter.append(_dot(q, c_prev.astype(BF16)))
        qn.append(_dot(q, n_prev.astype(BF16)))
        r_row = i_row - b_row
        mm = jnp.maximum(m_prev, a_col)
        r_l.append(r_row)
        mm_l.append(mm)
        w_inter_l.append(jnp.exp(m_prev - mm))
        b_end = b_row[:, L - 1:L]
        log_w = b_end + r_row
        m_new = jnp.maximum(b_end + m_prev, jnp.max(log_w, axis=1, keepdims=True))
        decay = jnp.exp(b_end + m_prev - m_new)
        kw = (kt.astype(F32) * jnp.exp(log_w - m_new)).astype(BF16)
        c_scr[h] = decay * c_prev + _dot(kw, v)
        n_scr[h] = decay * n_prev + _dot(kw, ones)
        m_scr[h] = jnp.broadcast_to(m_new, m_scr.shape[1:])

    s_qk = [(qk[h] * jnp.exp(jnp.where(causal, r_l[h] - mm_l[h], -jnp.inf))).astype(BF16)
            for h in range(heads)]
    pv = [_dot(s_qk[h], vm_ref[:, hs]) for h, hs in enumerate(head_slices)]
    rs = [_dot(s_qk[h], ones) for h in range(heads)]

    for h, hs in enumerate(head_slices):
        b_col = gcol[:, heads + h:heads + h + 1]
        num = w_inter_l[h] * inter[h] + pv[h]
        den = w_inter_l[h] * qn[h] + rs[h]
        rec = 1.0 / jnp.maximum(jnp.abs(den), jnp.exp(-(b_col + mm_l[h])))
        hh = num * jnp.concatenate([rec] * (hd // V7X_LANES), axis=1)
        mu = jnp.mean(hh, axis=1, keepdims=True)
        hc = hh - mu
        var = jnp.mean(hc * hc, axis=1, keepdims=True)
        hn = hc * lax.rsqrt(var + LN_EPS)
        out_ref[:, hs] = (hn * mul_ref[:, hs].astype(F32) + add_ref[:, hs].astype(F32)).astype(BF16)


def _moba_stage(q_ref, k_ref, vt_ref, sz_ref, out_ref, s_scr, m_scr, *, j):
    tq = q_ref.shape[0]
    bs = MOBA_BLOCK
    nblk = k_ref.shape[0] // bs
    aw = q_ref.shape[1] // MOBA_HEADS
    hd = aw // 2
    ones_rows = 2 * V7X_SUBLANES
    heads = [slice(h * aw, (h + 1) * aw) for h in range(MOBA_HEADS)]
    kv_heads = [slice(h * hd, (h + 1) * hd) for h in range(MOBA_HEADS)]
    tag_lane = lax.broadcasted_iota(jnp.int32, (bs, hd), 1)
    ones_blk = jnp.ones((ones_rows, bs), BF16)

    n_slots = s_scr.shape[0]

    def attend(own):
        causal_t = (lax.broadcasted_iota(jnp.int32, (bs, tq), 0)
                    <= lax.broadcasted_iota(jnp.int32, (bs, tq), 1))

        def score_pass(hi):
            slot = hi % n_slots
            mx = None
            for n in range(own + 1):
                k_aug = jnp.concatenate([k_ref[n * bs:(n + 1) * bs, kv_heads[hi]],
                                         (tag_lane == n).astype(BF16)], axis=1)
                s = _dot_nt(k_aug, q_ref[:, heads[hi]])
                if n == own:
                    s = jnp.where(causal_t, s, MASK_VALUE)
                s_scr[slot, n] = s
                sm = jnp.max(s, axis=0, keepdims=True)
                mx = sm if mx is None else jnp.maximum(mx, sm)
            m_scr[slot] = jnp.broadcast_to(mx, m_scr.shape[1:])

        def value_pass(hi):
            slot = hi % n_slots
            m = m_scr[slot][0:1, :]
            acc = None
            for n in range(own + 1):
                p = jnp.exp(s_scr[slot, n] - m).astype(BF16)
                vt_aug = jnp.concatenate([vt_ref[n, kv_heads[hi], :], ones_blk], axis=0)
                pv = _dot(vt_aug, p)
                acc = pv if acc is None else acc + pv
            o_t = acc[0:hd, :] * (1.0 / acc[hd:hd + 1, :])
            out_ref[:, hi * hd:(hi + 1) * hd] = (
                jnp.transpose(o_t) * sz_ref[:, hi * hd:(hi + 1) * hd].astype(F32)).astype(BF16)

        ahead = min(n_slots, MOBA_HEADS)
        for hi in range(ahead):
            score_pass(hi)
        for hi in range(MOBA_HEADS):
            value_pass(hi)
            if hi + ahead < MOBA_HEADS:
                score_pass(hi + ahead)

    for own in range(nblk):
        pl.when(j == own)(functools.partial(attend, own))


def _mix_stage(x_ref, om_ref, oa_ref, qc_ref, szc_ref, kv_scr, wo_ref, lng_ref, lnb_ref, y_ref):
    width = qc_ref.shape[1]
    hd = width // MEM_HEADS
    w_m = om_ref.shape[1]
    w_a = oa_ref.shape[1]
    head_slices = [slice(h * hd, (h + 1) * hd) for h in range(MEM_HEADS)]
    scores = [_dot_nt(qc_ref[:, hs], kv_scr[:, hs]) for hs in head_slices]
    mixed = _dot(oa_ref[...], wo_ref[w_m:w_m + w_a, :])
    probs = [jnp.exp(s - jnp.max(s, axis=1, keepdims=True)) for s in scores]
    outs = [_dot(p.astype(BF16), kv_scr[:, width + h * hd:width + (h + 1) * hd])
            for h, p in enumerate(probs)]
    oc = jnp.concatenate(
        [(outs[h] * (1.0 / jnp.sum(probs[h], axis=1, keepdims=True))
          * szc_ref[:, hs].astype(F32)).astype(BF16) for h, hs in enumerate(head_slices)], axis=1)
    mixed = mixed + _dot(oc, wo_ref[w_m + w_a:, :])
    mixed = mixed + _dot(om_ref[...], wo_ref[0:w_m, :])
    y = DEEPNORM_ALPHA * x_ref[...] + mixed
    mu = jnp.mean(y, axis=1, keepdims=True)
    yc = y - mu
    var = jnp.mean(yc * yc, axis=1, keepdims=True)
    y_ref[...] = yc * lax.rsqrt(var + LN_EPS) * lng_ref[...] + lnb_ref[...]


def _layer_kernel(x_ref, pos_ref, mem_ref, win_ref, convw_ref, convb_ref, wq_ref, wk_ref, wv_ref,
                  wg_ref, bg_ref, normg_ref, skip_ref, invf_ref, wkv_ref, wo_ref, lng_ref, lnb_ref,
                  y_ref,
                  qm_s, kmt_s, vm_s, add_s, mul_s, gcol_s, grow_s, qa_s, sza_s, qc_s, szc_s,
                  ka_s, vat_s, om_s, oa_s, xpad_s, kmean_s, c_s, n_s, m_s, score_s, rowmax_s, kv_s,
                  *, ml_width, ml_heads, moba_width, mem_width, n_sel, gate_slots):
    t = pl.program_id(1)
    tm = x_ref.shape[0]
    pad = xpad_s.shape[0] - tm

    @pl.when(t == 0)
    def _():
        xpad_s[0:pad, :] = jnp.zeros((pad, ml_width), F32)
        kmean_s[...] = jnp.zeros(kmean_s.shape, F32)
        c_s[...] = jnp.zeros(c_s.shape, F32)
        n_s[...] = jnp.zeros(n_s.shape, F32)
        m_s[...] = jnp.zeros(m_s.shape, F32)
        kv_s[...] = _dot(mem_ref[...].astype(BF16), wkv_ref[...]).astype(BF16)

    @pl.when(t > 0)
    def _():
        xpad_s[0:pad, :] = xpad_s[tm:tm + pad, :]

    rows = pl.ds(pl.multiple_of(t * tm, tm), tm)
    _proj_stage(x_ref, pos_ref, win_ref, convw_ref, convb_ref, wq_ref, wk_ref, wv_ref,
                wg_ref, bg_ref, normg_ref, skip_ref, invf_ref,
                qm_s, kmt_s, vm_s, add_s, mul_s, gcol_s, grow_s,
                qa_s, ka_s.at[rows], vat_s.at[t], sza_s, qc_s, szc_s, xpad_s, kmean_s,
                t=t, ml_width=ml_width, ml_heads=ml_heads, moba_width=moba_width,
                mem_width=mem_width, n_sel=n_sel, gate_slots=gate_slots)
    _mlstm_stage(qm_s, kmt_s, vm_s, gcol_s, grow_s, add_s, mul_s, om_s, c_s, n_s, m_s, heads=ml_heads)
    _moba_stage(qa_s, ka_s, vat_s, sza_s, oa_s, score_s, rowmax_s, j=t)
    _mix_stage(x_ref, om_s, oa_s, qc_s, szc_s, kv_s, wo_ref, lng_ref, lnb_ref, y_ref)


def _layer_call(x, pos, mem, win, convw, convb, wq, wk, wv, wg, bg, normg, skip, invf, wkv, wo, lng, lnb,
                *, ml_width, ml_heads, moba_width, mem_width):
    bsz, s, d = x.shape
    m_tok = mem.shape[1]
    tm = TOKEN_TILE
    nt = s // tm
    n_sel = min(MOBA_TOP_K, nt - 1)
    gate_slots = max(V7X_SUBLANES, pl.next_power_of_2(nt))
    assert MOBA_HEADS * gate_slots <= V7X_LANES and s % tm == 0 and ml_heads >= 2
    hd = ml_width // ml_heads
    tok = lambda w: pl.BlockSpec((None, tm, w), lambda b, t: (b, t, 0))

    def resident(a):
        nd = a.ndim
        return pl.BlockSpec(a.shape, lambda b, t, _nd=nd: (0,) * _nd, pipeline_mode=pl.Buffered(1))

    weights = (win, convw, convb, wq, wk, wv, wg, bg, normg, skip, invf, wkv, wo, lng, lnb)
    in_specs = ([tok(d),
                 pl.BlockSpec((None, None, 1, tm), lambda b, t: (b, t, 0, 0)),
                 pl.BlockSpec((None, m_tok, d), lambda b, t: (b, 0, 0))]
                + [resident(a) for a in weights])
    vmem = pltpu.VMEM
    score_heads = MOBA_HEADS
    scratch = [
        vmem((tm, ml_width), BF16),
        vmem((ml_width, tm), BF16),
        vmem((tm, ml_width), BF16),
        vmem((tm, ml_width), BF16),
        vmem((tm, ml_width), BF16),
        vmem((tm, V7X_LANES), F32),
        vmem((GATE_ROWS, tm), F32),
        vmem((tm, 2 * moba_width), BF16),
        vmem((tm, moba_width), BF16),
        vmem((tm, mem_width), BF16),
        vmem((tm, mem_width), BF16),
        vmem((s, moba_width), BF16),
        vmem((nt, moba_width, tm), BF16),
        vmem((tm, ml_width), BF16),
        vmem((tm, moba_width), BF16),
        vmem((tm + V7X_SUBLANES, ml_width), F32),
        vmem((V7X_LANES, moba_width), F32),
        vmem((ml_heads, hd, hd), F32),
        vmem((ml_heads, hd, V7X_LANES), F32),
        vmem((ml_heads, V7X_SUBLANES, V7X_LANES), F32),
        vmem((score_heads, nt, tm, MOBA_BLOCK), F32),
        vmem((score_heads, V7X_SUBLANES, tm), F32),
        vmem((m_tok, 2 * mem_width), BF16),
    ]
    kern = functools.partial(_layer_kernel, ml_width=ml_width, ml_heads=ml_heads, moba_width=moba_width,
                             mem_width=mem_width, n_sel=n_sel, gate_slots=gate_slots)
    return pl.pallas_call(
        kern,
        grid=(bsz, nt),
        in_specs=in_specs,
        out_specs=tok(d),
        out_shape=jax.ShapeDtypeStruct((bsz, s, d), x.dtype),
        scratch_shapes=scratch,
        compiler_params=pltpu.CompilerParams(dimension_semantics=("arbitrary", "arbitrary"),
                                             vmem_limit_bytes=V7X_VMEM_LIMIT_BYTES),
        name="layer",
    )(x, pos.reshape(bsz, nt, 1, tm), mem, *weights)


def _diag_tiles(w, heads):
    groups, blk, _ = w.shape
    hd = groups // heads * blk
    rows = w.reshape(heads, hd, blk)
    idx = jnp.arange(hd)
    same_group = (idx[:, None] // blk) == (idx[None, :] // blk)
    return jnp.where(same_group, jnp.tile(rows, (1, 1, hd // blk)), 0.0)


def kernel(x, mem, positions, w_in, mlstm_conv_w, mlstm_conv_b, mlstm_wq, mlstm_wk, mlstm_wv, mlstm_w_gates, mlstm_b_gates, mlstm_norm_g, mlstm_skip, w_mem_kv, w_out, ln_g, ln_b):
    bsz, s, d = x.shape
    ml_width = mlstm_conv_w.shape[1]
    ml_heads = mlstm_b_gates.shape[0] // 2
    mem_width = w_mem_kv.shape[1] // 2
    moba_width = (w_in.shape[1] - 2 * ml_width - 2 * mem_width) // 4
    moba_hd = moba_width // MOBA_HEADS
    assert moba_hd == V7X_LANES and mem_width // MEM_HEADS == V7X_LANES and MOBA_HEADS == 4
    assert s % TOKEN_TILE == 0

    row = lambda a: a.reshape(1, -1).astype(F32)
    wq = _diag_tiles(mlstm_wq, ml_heads).astype(BF16)
    wk = _diag_tiles(mlstm_wk, ml_heads).astype(BF16)
    wv = _diag_tiles(mlstm_wv, ml_heads).astype(BF16)
    n_gates = mlstm_w_gates.shape[1]
    wg = jnp.pad(mlstm_w_gates, ((0, 0), (0, V7X_LANES - n_gates))).astype(BF16)
    bg = jnp.pad(mlstm_b_gates, (0, V7X_LANES - n_gates)).reshape(1, -1).astype(F32)
    half = moba_hd // 2
    inv_freq = ROPE_THETA ** (-jnp.arange(half, dtype=F32) * 2.0 / moba_hd)
    invf = inv_freq.reshape(-1, 1)
    pos = positions.astype(F32)

    return _layer_call(
        x, pos, mem, w_in.astype(BF16), mlstm_conv_w.astype(F32), row(mlstm_conv_b), wq, wk, wv, wg, bg,
        row(mlstm_norm_g), row(mlstm_skip), invf, w_mem_kv.astype(BF16), w_out.astype(BF16),
        row(ln_g), row(ln_b),
        ml_width=ml_width, ml_heads=ml_heads, moba_width=moba_width, mem_width=mem_width)
```

```python
import functools

import jax
import jax.numpy as jnp
from jax import lax
from jax.experimental import pallas as pl
from jax.experimental.pallas import tpu as pltpu

MOBA_HEADS = 4
MOBA_BLOCK = 256
MOBA_TOP_K = 3
MEM_HEADS = 4
ROPE_THETA = 10000.0
DEPTH = 1
DEEPNORM_ALPHA = (2 * DEPTH) ** 0.25
LN_EPS = 1e-5

V7X_LANES = 128
V7X_SUBLANES = 8
V7X_VMEM_BYTES = 64 * 1024 * 1024
V7X_VMEM_LIMIT_BYTES = V7X_VMEM_BYTES - 8 * 1024 * 1024

TOKEN_TILE = MOBA_BLOCK
MASK_VALUE = -1e30
GATE_ROWS = 2 * V7X_SUBLANES

F32 = jnp.float32
BF16 = jnp.bfloat16


def _dot(a, b):
    return jnp.dot(a, b, preferred_element_type=F32)


def _dot_nt(a, b):
    return lax.dot_general(a, b, (((1,), (1,)), ((), ())), preferred_element_type=F32)


def _lane_scan(v, op, identity, lane_idx):
    shift = 1
    while shift < v.shape[-1]:
        v = op(v, jnp.where(lane_idx >= shift, pltpu.roll(v, shift, v.ndim - 1), identity))
        shift *= 2
    return v


def _silu(v):
    h = 0.5 * v
    return h + h * jnp.tanh(h)


def _log_sigmoid(v):
    return jnp.minimum(v, 0.0) - jnp.log1p(jnp.exp(-jnp.abs(v)))


def _proj_stage(x_ref, pos_ref, win_ref, convw_ref, convb_ref, wq_ref, wk_ref, wv_ref,
                wfold_ref, bg_ref, normg_ref, skip_ref, invf_ref,
                qm_ref, kmt_ref, vm_ref, add_ref, mul_ref, gcol_ref, grow_ref,
                qa_ref, ka_ref, vat_ref, sza_ref, qc_ref, szc_ref,
                xpad_scr, kmean_scr,
                *, t, ml_width, ml_heads, moba_width, mem_width, n_sel, gate_slots):
    tm = x_ref.shape[0]
    hd = ml_width // ml_heads
    moba_hd = moba_width // MOBA_HEADS
    mem_hd = mem_width // MEM_HEADS
    c_zm = ml_width
    c_qa = 2 * ml_width
    c_ka = c_qa + moba_width
    c_va = c_ka + moba_width
    c_za = c_va + moba_width
    c_qc = c_za + moba_width
    c_zc = c_qc + mem_width

    xb = x_ref[...].astype(BF16)
    pad = xpad_scr.shape[0] - tm
    k_w = convw_ref.shape[0]
    nb_lanes = gate_slots

    def x_cols(c0, width):
        return _dot(xb, win_ref[:, c0:c0 + width])

    def rotary_pair(p, qp, kp):
        rots, means = [], []
        for i in range(2):
            h = 2 * p + i
            ls = slice(i * moba_hd, (i + 1) * moba_hd)
            q_rot = qp[:, ls] * cosf + pltpu.roll(qp[:, ls], moba_hd // 2, 1) * sins
            k_rot = kp[:, ls] * cosf + pltpu.roll(kp[:, ls], moba_hd // 2, 1) * sins
            qa_ref[:, 2 * h * moba_hd:(2 * h + 1) * moba_hd] = (q_rot * (moba_hd ** -0.5)).astype(BF16)
            ka_ref[:, h * moba_hd:(h + 1) * moba_hd] = k_rot.astype(BF16)
            rots.append(q_rot)
            means.append(jnp.mean(k_rot, axis=0, keepdims=True))
        return rots, means

    def conv_gate_head(h, xm_h, zm_h):
        hs = slice(h * hd, (h + 1) * hd)
        xpad_scr[pad:pad + tm, hs] = xm_h
        conv = convb_ref[:, hs] + xm_h * convw_ref[k_w - 1:k_w, hs]
        for j in range(k_w - 1):
            conv = conv + xpad_scr[pl.ds(pad - (k_w - 1) + j, tm), hs] * convw_ref[j:j + 1, hs]
        xc_h = _silu(conv)
        sz_h = _silu(zm_h)
        add_ref[:, hs] = (skip_ref[:, hs] * xc_h * sz_h).astype(BF16)
        mul_ref[:, hs] = (normg_ref[:, hs] * sz_h).astype(BF16)
        return xc_h.astype(BF16), xm_h.astype(BF16)

    def blockdiag_head(h, xc_b, xm_b):
        hs = slice(h * hd, (h + 1) * hd)
        q_h = _dot(xc_b, wq_ref[h]).astype(BF16)
        k_f = _dot(xc_b, wk_ref[h])
        v_h = _dot(xm_b, wv_ref[h]).astype(BF16)
        qm_ref[:, hs] = q_h
        vm_ref[:, hs] = v_h
        kmt_ref[hs, :] = (jnp.transpose(k_f) * (hd ** -0.5)).astype(BF16)

    def gate_terms(h, xc_b, xm_b):
        hs = slice(h * hd, (h + 1) * hd)
        return _dot(xc_b, wfold_ref[0, hs, :]) + _dot(xm_b, wfold_ref[1, hs, :])

    def select_blocks(gate_t):
        n_rows = MOBA_HEADS * nb_lanes
        gate = gate_t[0:n_rows, :]
        row = lax.broadcasted_iota(jnp.int32, gate.shape, 0)
        n_r = row & (nb_lanes - 1)
        valid = n_r < t
        gate = jnp.where(valid, gate, -jnp.inf)
        cnt = jnp.zeros(gate.shape, jnp.int32)
        for r in range(1, nb_lanes):
            up = pltpu.roll(gate, r, 0)
            cnt = cnt + ((n_r >= r) & (up >= gate)).astype(jnp.int32)
            dn = pltpu.roll(gate, n_rows - r, 0)
            cnt = cnt + ((n_r < nb_lanes - r) & (dn > gate)).astype(jnp.int32)
        keep = (valid & (cnt < n_sel)) | (n_r == t)
        selb_t = jnp.where(keep, 0.0, MASK_VALUE)
        selb = jnp.transpose(jnp.concatenate(
            [selb_t, jnp.full((V7X_LANES - n_rows, tm), MASK_VALUE, F32)], axis=0))
        for h in range(MOBA_HEADS):
            shift = (V7X_LANES - h * nb_lanes) % V7X_LANES
            sel_h = selb if shift == 0 else pltpu.roll(selb, shift, 1)
            qa_ref[:, (2 * h + 1) * moba_hd:(2 * h + 2) * moba_hd] = sel_h.astype(BF16)

    pw = 2 * moba_hd
    qa0, ka0 = x_cols(c_qa, pw), x_cols(c_ka, pw)
    ang_t = invf_ref[...] * pos_ref[...]
    cos_t = jnp.cos(ang_t)
    sin_t = jnp.sin(ang_t)
    cosf = jnp.transpose(jnp.concatenate([cos_t, cos_t], axis=0))
    sins = jnp.transpose(jnp.concatenate([-sin_t, sin_t], axis=0))
    qa1, ka1 = x_cols(c_qa + pw, pw), x_cols(c_ka + pw, pw)
    rots0, means0 = rotary_pair(0, qa0, ka0)
    xm0, zm0 = x_cols(0, hd), x_cols(c_zm, hd)
    rots1, means1 = rotary_pair(1, qa1, ka1)
    q_rots, kmean_rows = rots0 + rots1, means0 + means1
    q_all = jnp.concatenate(q_rots, axis=1)
    km = kmean_scr[...]
    q_hi = q_all.astype(BF16)
    q_lo = (q_all - q_hi.astype(F32)).astype(BF16)
    k_hi = km.astype(BF16)
    k_lo = (km - k_hi.astype(F32)).astype(BF16)
    xz = {0: (xm0, zm0), 1: (x_cols(hd, hd), x_cols(c_zm + hd, hd))}
    gate = _dot_nt(k_hi, q_hi) + _dot_nt(k_hi, q_lo) + _dot_nt(k_lo, q_hi)
    pending = [("xz", h) for h in range(2, ml_heads)] + [("va", None), ("za", None)]
    qkv = []
    va = za = None
    for h in range(ml_heads):
        xc_b, xm_b = conv_gate_head(h, *xz[h])
        if pending:
            kind, arg = pending.pop(0)
            if kind == "xz":
                xz[arg] = (x_cols(arg * hd, hd), x_cols(c_zm + arg * hd, hd))
            elif kind == "va":
                va = x_cols(c_va, moba_width)
            else:
                za = x_cols(c_za, moba_width)
        blockdiag_head(h, xc_b, xm_b)
        qkv.append((xc_b, xm_b))
    if va is None:
        va = x_cols(c_va, moba_width)
    if za is None:
        za = x_cols(c_za, moba_width)
    g = jnp.zeros((tm, V7X_LANES), F32) + bg_ref[...]
    for h in range(ml_heads):
        g = g + gate_terms(h, *qkv[h])
    sza_ref[...] = _silu(za).astype(BF16)
    qc = x_cols(c_qc, mem_width)
    zc = x_cols(c_zc, mem_width)

    nrow = grow_ref.shape[0]
    gt = jnp.transpose(g)[0:V7X_SUBLANES, :]
    sub = lax.broadcasted_iota(jnp.int32, gt.shape, 0)
    tok_i = lax.broadcasted_iota(jnp.int32, gt.shape, 1)
    b = _lane_scan(jnp.where(sub >= ml_heads, _log_sigmoid(gt), 0.0), jnp.add, 0.0, tok_i)
    a = _lane_scan(jnp.where(sub >= ml_heads, pltpu.roll(gt, ml_heads, 0) - b, -jnp.inf),
                   jnp.maximum, -jnp.inf, tok_i)
    rows = jnp.concatenate([jnp.where(sub < ml_heads, gt, b), pltpu.roll(a, ml_heads, 0)], axis=0)
    grow_ref[...] = rows
    gcol_ref[...] = jnp.transpose(
        jnp.concatenate([rows, jnp.zeros((V7X_LANES - nrow, tm), F32)], axis=0))

    km_row = lax.broadcasted_iota(jnp.int32, kmean_scr.shape, 0)
    km_head = lax.broadcasted_iota(jnp.int32, kmean_scr.shape, 1) // moba_hd
    kmean_scr[...] = jnp.where(km_row == km_head * nb_lanes + t,
                               jnp.concatenate(kmean_rows, axis=1), kmean_scr[...])
    select_blocks(gate)
    vat_ref[...] = jnp.transpose(va).astype(BF16)
    qc_ref[...] = (qc * (mem_hd ** -0.5)).astype(BF16)
    szc_ref[...] = _silu(zc).astype(BF16)


def _mlstm_stage(qm_ref, kmt_ref, vm_ref, gcol_ref, grow_ref, add_ref, mul_ref, out_ref,
                 cn_scr, m_scr, *, heads):
    L = qm_ref.shape[0]
    hd = qm_ref.shape[1] // heads
    row_i = lax.broadcasted_iota(jnp.int32, (L, L), 0)
    col_i = lax.broadcasted_iota(jnp.int32, (L, L), 1)
    causal = row_i >= col_i
    gcol = gcol_ref[...]
    grow = grow_ref[...]
    ones = jnp.ones((L, V7X_LANES), BF16)
    head_slices = [slice(h * hd, (h + 1) * hd) for h in range(heads)]

    qk, inter, qn, mm_l, r_l, w_inter_l, v_aug = [], [], [], [], [], [], []
    for h, hs in enumerate(head_slices):
        q = qm_ref[:, hs]
        kt = kmt_ref[hs, :]
        v_aug.append(jnp.concatenate([vm_ref[:, hs], ones], axis=1))
        i_row = grow[h:h + 1, :]
        b_row = grow[heads + h:heads + h + 1, :]
        a_col = gcol[:, 2 * heads + h:2 * heads + h + 1]
        m_prev = m_scr[h][0:1, 0:1]
        cn_prev = cn_scr[h]
        qk.append(_dot(q, kt))
        q_cn = _dot(q, cn_prev.astype(BF16))
        inter.append(q_cn[:, :hd])
        qn.append(q_cn[:, hd:])
        r_row = i_row - b_row
        mm = jnp.maximum(m_prev, a_col)
        r_l.append(r_row)
        mm_l.append(mm)
        w_inter_l.append(jnp.exp(m_prev - mm))
        b_end = b_row[:, L - 1:L]
        log_w = b_end + r_row
        m_new = jnp.maximum(b_end + m_prev, jnp.max(log_w, axis=1, keepdims=True))
        decay = jnp.exp(b_end + m_prev - m_new)
        kw = (kt.astype(F32) * jnp.exp(log_w - m_new)).astype(BF16)
        cn_scr[h] = decay * cn_prev + _dot(kw, v_aug[h])
        m_scr[h] = jnp.broadcast_to(m_new, m_scr.shape[1:])

    s_qk = [(qk[h] * jnp.exp(jnp.where(causal, r_l[h] - mm_l[h], -jnp.inf))).astype(BF16)
            for h in range(heads)]
    pv_rs = [_dot(s_qk[h], v_aug[h]) for h in range(heads)]

    for h, hs in enumerate(head_slices):
        b_col = gcol[:, heads + h:heads + h + 1]
        num = w_inter_l[h] * inter[h] + pv_rs[h][:, :hd]
        den = w_inter_l[h] * qn[h] + pv_rs[h][:, hd:]
        rec = 1.0 / jnp.maximum(jnp.abs(den), jnp.exp(-(b_col + mm_l[h])))
        hh = num * jnp.concatenate([rec] * (hd // V7X_LANES), axis=1)
        mu = jnp.mean(hh, axis=1, keepdims=True)
        hc = hh - mu
        var = jnp.mean(hc * hc, axis=1, keepdims=True)
        hn = hc * lax.rsqrt(var + LN_EPS)
        out_ref[:, hs] = (hn * mul_ref[:, hs].astype(F32) + add_ref[:, hs].astype(F32)).astype(BF16)


def _moba_stage(q_ref, k_ref, vt_ref, sz_ref, out_ref, s_scr, m_scr, *, j):
    tq = q_ref.shape[0]
    bs = MOBA_BLOCK
    nblk = k_ref.shape[0] // bs
    aw = q_ref.shape[1] // MOBA_HEADS
    hd = aw // 2
    ones_rows = 2 * V7X_SUBLANES
    heads = [slice(h * aw, (h + 1) * aw) for h in range(MOBA_HEADS)]
    kv_heads = [slice(h * hd, (h + 1) * hd) for h in range(MOBA_HEADS)]
    tag_lane = lax.broadcasted_iota(jnp.int32, (bs, hd), 1)
    ones_blk = jnp.ones((ones_rows, bs), BF16)

    n_slots = s_scr.shape[0]

    def attend(own):
        causal_t = (lax.broadcasted_iota(jnp.int32, (bs, tq), 0)
                    <= lax.broadcasted_iota(jnp.int32, (bs, tq), 1))

        def score_pass(hi):
            slot = hi % n_slots
            mx = None
            for n in range(own + 1):
                k_aug = jnp.concatenate([k_ref[n * bs:(n + 1) * bs, kv_heads[hi]],
                                         (tag_lane == n).astype(BF16)], axis=1)
                s = _dot_nt(k_aug, q_ref[:, heads[hi]])
                if n == own:
                    s = jnp.where(causal_t, s, MASK_VALUE)
                s_scr[slot, n] = s
                sm = jnp.max(s, axis=0, keepdims=True)
                mx = sm if mx is None else jnp.maximum(mx, sm)
            m_scr[slot] = jnp.broadcast_to(mx, m_scr.shape[1:])

        def value_pass(hi):
            slot = hi % n_slots
            m = m_scr[slot][0:1, :]
            acc = None
            for n in range(own + 1):
                p = jnp.exp(s_scr[slot, n] - m).astype(BF16)
                vt_aug = jnp.concatenate([vt_ref[n, kv_heads[hi], :], ones_blk], axis=0)
                pv = _dot(vt_aug, p)
                acc = pv if acc is None else acc + pv
            o_t = acc[0:hd, :] * (1.0 / acc[hd:hd + 1, :])
            out_ref[:, hi * hd:(hi + 1) * hd] = (
                jnp.transpose(o_t) * sz_ref[:, hi * hd:(hi + 1) * hd].astype(F32)).astype(BF16)

        ahead = min(n_slots, MOBA_HEADS)
        for hi in range(ahead):
            score_pass(hi)
        for hi in range(MOBA_HEADS):
            value_pass(hi)
            if hi + ahead < MOBA_HEADS:
                score_pass(hi + ahead)

    for own in range(nblk):
        pl.when(j == own)(functools.partial(attend, own))


def _mix_stage(x_ref, om_ref, oa_ref, qc_ref, szc_ref, kv_scr, wo_ref, lng_ref, lnb_ref, y_ref):
    width = qc_ref.shape[1]
    hd = width // MEM_HEADS
    w_m = om_ref.shape[1]
    w_a = oa_ref.shape[1]
    head_slices = [slice(h * hd, (h + 1) * hd) for h in range(MEM_HEADS)]
    scores = [_dot_nt(qc_ref[:, hs], kv_scr[:, hs]) for hs in head_slices]
    mixed = _dot(oa_ref[...], wo_ref[w_m:w_m + w_a, :])
    probs = [jnp.exp(s - jnp.max(s, axis=1, keepdims=True)) for s in scores]
    outs = [_dot(p.astype(BF16), kv_scr[:, width + h * hd:width + (h + 1) * hd])
            for h, p in enumerate(probs)]
    oc = jnp.concatenate(
        [(outs[h] * (1.0 / jnp.sum(probs[h], axis=1, keepdims=True))
          * szc_ref[:, hs].astype(F32)).astype(BF16) for h, hs in enumerate(head_slices)], axis=1)
    mixed = mixed + _dot(oc, wo_ref[w_m + w_a:, :])
    mixed = mixed + _dot(om_ref[...], wo_ref[0:w_m, :])
    y = DEEPNORM_ALPHA * x_ref[...] + mixed
    mu = jnp.mean(y, axis=1, keepdims=True)
    yc = y - mu
    var = jnp.mean(yc * yc, axis=1, keepdims=True)
    y_ref[...] = yc * lax.rsqrt(var + LN_EPS) * lng_ref[...] + lnb_ref[...]


def _layer_kernel(x_ref, pos_ref, mem_ref, win_ref, convw_ref, convb_ref, wq_ref, wk_ref, wv_ref,
                  wg_ref, bg_ref, normg_ref, skip_ref, invf_ref, wkv_ref, wo_ref, lng_ref, lnb_ref,
                  y_ref,
                  qm_s, kmt_s, vm_s, add_s, mul_s, gcol_s, grow_s, qa_s, sza_s, qc_s, szc_s,
                  ka_s, vat_s, om_s, oa_s, xpad_s, kmean_s, cn_s, m_s, score_s, rowmax_s, kv_s, wfold_s,
                  *, ml_width, ml_heads, moba_width, mem_width, n_sel, gate_slots):
    t = pl.program_id(1)
    tm = x_ref.shape[0]
    pad = xpad_s.shape[0] - tm
    hd = ml_width // ml_heads

    @pl.when((pl.program_id(0) == 0) & (t == 0))
    def _():
        for h in range(ml_heads):
            rq, rk, rv = [slice(i * ml_width + h * hd, i * ml_width + (h + 1) * hd) for i in range(3)]
            wfold_s[0, rq, :] = (_dot(wq_ref[h], wg_ref[rq, :]) + _dot(wk_ref[h], wg_ref[rk, :])).astype(BF16)
            wfold_s[1, rq, :] = _dot(wv_ref[h], wg_ref[rv, :]).astype(BF16)

    @pl.when(t == 0)
    def _():
        xpad_s[0:pad, :] = jnp.zeros((pad, ml_width), F32)
        kmean_s[...] = jnp.zeros(kmean_s.shape, F32)
        cn_s[...] = jnp.zeros(cn_s.shape, F32)
        m_s[...] = jnp.zeros(m_s.shape, F32)
        kv_s[...] = _dot(mem_ref[...].astype(BF16), wkv_ref[...]).astype(BF16)

    @pl.when(t > 0)
    def _():
        xpad_s[0:pad, :] = xpad_s[tm:tm + pad, :]

    rows = pl.ds(pl.multiple_of(t * tm, tm), tm)
    _proj_stage(x_ref, pos_ref, win_ref, convw_ref, convb_ref, wq_ref, wk_ref, wv_ref,
                wfold_s, bg_ref, normg_ref, skip_ref, invf_ref,
                qm_s, kmt_s, vm_s, add_s, mul_s, gcol_s, grow_s,
                qa_s, ka_s.at[rows], vat_s.at[t], sza_s, qc_s, szc_s, xpad_s, kmean_s,
                t=t, ml_width=ml_width, ml_heads=ml_heads, moba_width=moba_width,
                mem_width=mem_width, n_sel=n_sel, gate_slots=gate_slots)
    _mlstm_stage(qm_s, kmt_s, vm_s, gcol_s, grow_s, add_s, mul_s, om_s, cn_s, m_s, heads=ml_heads)
    _moba_stage(qa_s, ka_s, vat_s, sza_s, oa_s, score_s, rowmax_s, j=t)
    _mix_stage(x_ref, om_s, oa_s, qc_s, szc_s, kv_s, wo_ref, lng_ref, lnb_ref, y_ref)


def _layer_call(x, pos, mem, win, convw, convb, wq, wk, wv, wg, bg, normg, skip, invf, wkv, wo, lng, lnb,
                *, ml_width, ml_heads, moba_width, mem_width):
    bsz, s, d = x.shape
    m_tok = mem.shape[1]
    tm = TOKEN_TILE
    nt = s // tm
    n_sel = min(MOBA_TOP_K, nt - 1)
    gate_slots = max(V7X_SUBLANES, pl.next_power_of_2(nt))
    assert MOBA_HEADS * gate_slots <= V7X_LANES and s % tm == 0 and ml_heads >= 2
    hd = ml_width // ml_heads
    tok = lambda w: pl.BlockSpec((None, tm, w), lambda b, t: (b, t, 0))

    def resident(a):
        nd = a.ndim
        return pl.BlockSpec(a.shape, lambda b, t, _nd=nd: (0,) * _nd, pipeline_mode=pl.Buffered(1))

    weights = (win, convw, convb, wq, wk, wv, wg, bg, normg, skip, invf, wkv, wo, lng, lnb)
    in_specs = ([tok(d),
                 pl.BlockSpec((None, None, 1, tm), lambda b, t: (b, t, 0, 0)),
                 pl.BlockSpec((None, m_tok, d), lambda b, t: (b, 0, 0))]
                + [resident(a) for a in weights])
    vmem = pltpu.VMEM
    score_heads = MOBA_HEADS
    scratch = [
        vmem((tm, ml_width), BF16),
        vmem((ml_width, tm), BF16),
        vmem((tm, ml_width), BF16),
        vmem((tm, ml_width), BF16),
        vmem((tm, ml_width), BF16),
        vmem((tm, V7X_LANES), F32),
        vmem((GATE_ROWS, tm), F32),
        vmem((tm, 2 * moba_width), BF16),
        vmem((tm, moba_width), BF16),
        vmem((tm, mem_width), BF16),
        vmem((tm, mem_width), BF16),
        vmem((s, moba_width), BF16),
        vmem((nt, moba_width, tm), BF16),
        vmem((tm, ml_width), BF16),
        vmem((tm, moba_width), BF16),
        vmem((tm + V7X_SUBLANES, ml_width), F32),
        vmem((V7X_LANES, moba_width), F32),
        vmem((ml_heads, hd, hd + V7X_LANES), F32),
        vmem((ml_heads, V7X_SUBLANES, V7X_LANES), F32),
        vmem((score_heads, nt, tm, MOBA_BLOCK), F32),
        vmem((score_heads, V7X_SUBLANES, tm), F32),
        vmem((m_tok, 2 * mem_width), BF16),
        vmem((2, ml_width, V7X_LANES), BF16),
    ]
    kern = functools.partial(_layer_kernel, ml_width=ml_width, ml_heads=ml_heads, moba_width=moba_width,
                             mem_width=mem_width, n_sel=n_sel, gate_slots=gate_slots)
    return pl.pallas_call(
        kern,
        grid=(bsz, nt),
        in_specs=in_specs,
        out_specs=tok(d),
        out_shape=jax.ShapeDtypeStruct((bsz, s, d), x.dtype),
        scratch_shapes=scratch,
        compiler_params=pltpu.CompilerParams(dimension_semantics=("arbitrary", "arbitrary"),
                                             vmem_limit_bytes=V7X_VMEM_LIMIT_BYTES),
        name="layer",
    )(x, pos.reshape(bsz, nt, 1, tm), mem, *weights)


def _diag_tiles(w, heads):
    groups, blk, _ = w.shape
    hd = groups // heads * blk
    rows = w.reshape(heads, hd, blk)
    idx = jnp.arange(hd)
    same_group = (idx[:, None] // blk) == (idx[None, :] // blk)
    return jnp.where(same_group, jnp.tile(rows, (1, 1, hd // blk)), 0.0)


def kernel(x, mem, positions, w_in, mlstm_conv_w, mlstm_conv_b, mlstm_wq, mlstm_wk, mlstm_wv, mlstm_w_gates, mlstm_b_gates, mlstm_norm_g, mlstm_skip, w_mem_kv, w_out, ln_g, ln_b):
    bsz, s, d = x.shape
    ml_width = mlstm_conv_w.shape[1]
    ml_heads = mlstm_b_gates.shape[0] // 2
    mem_width = w_mem_kv.shape[1] // 2
    moba_width = (w_in.shape[1] - 2 * ml_width - 2 * mem_width) // 4
    moba_hd = moba_width // MOBA_HEADS
    assert moba_hd == V7X_LANES and mem_width // MEM_HEADS == V7X_LANES and MOBA_HEADS == 4
    assert s % TOKEN_TILE == 0

    row = lambda a: a.reshape(1, -1).astype(F32)
    wq = _diag_tiles(mlstm_wq, ml_heads).astype(BF16)
    wk = _diag_tiles(mlstm_wk, ml_heads).astype(BF16)
    wv = _diag_tiles(mlstm_wv, ml_heads).astype(BF16)
    n_gates = mlstm_w_gates.shape[1]
    wg = jnp.pad(mlstm_w_gates, ((0, 0), (0, V7X_LANES - n_gates))).astype(BF16)
    bg = jnp.pad(mlstm_b_gates, (0, V7X_LANES - n_gates)).reshape(1, -1).astype(F32)
    half = moba_hd // 2
    inv_freq = ROPE_THETA ** (-jnp.arange(half, dtype=F32) * 2.0 / moba_hd)
    invf = inv_freq.reshape(-1, 1)
    pos = positions.astype(F32)

    return _layer_call(
        x, pos, mem, w_in.astype(BF16), mlstm_conv_w.astype(F32), row(mlstm_conv_b), wq, wk, wv, wg, bg,
        row(mlstm_norm_g), row(mlstm_skip), invf, w_mem_kv.astype(BF16), w_out.astype(BF16),
        row(ln_g), row(ln_b),
        ml_width=ml_width, ml_heads=ml_heads, moba_width=moba_width, mem_width=mem_width)
```

```python
import functools

import jax
import jax.numpy as jnp
from jax import lax
from jax.experimental import pallas as pl
from jax.experimental.pallas import tpu as pltpu

MOBA_HEADS = 4
MOBA_BLOCK = 256
MOBA_TOP_K = 3
MEM_HEADS = 4
ROPE_THETA = 10000.0
DEPTH = 1
DEEPNORM_ALPHA = (2 * DEPTH) ** 0.25
LN_EPS = 1e-5

V7X_LANES = 128
V7X_SUBLANES = 8
V7X_VMEM_BYTES = 64 * 1024 * 1024
V7X_VMEM_LIMIT_BYTES = V7X_VMEM_BYTES - 8 * 1024 * 1024

TOKEN_TILE = MOBA_BLOCK
MASK_VALUE = -1e30
GATE_ROWS = 2 * V7X_SUBLANES

F32 = jnp.float32
BF16 = jnp.bfloat16


def _dot(a, b):
    return jnp.dot(a, b, preferred_element_type=F32)


def _dot_nt(a, b):
    return lax.dot_general(a, b, (((1,), (1,)), ((), ())), preferred_element_type=F32)


def _lane_scan(v, op, identity, lane_idx):
    shift = 1
    while shift < v.shape[-1]:
        v = op(v, jnp.where(lane_idx >= shift, pltpu.roll(v, shift, v.ndim - 1), identity))
        shift *= 2
    return v


def _silu(v):
    h = 0.5 * v
    return h + h * jnp.tanh(h)


def _log_sigmoid(v):
    return jnp.minimum(v, 0.0) - jnp.log1p(jnp.exp(-jnp.abs(v)))


def _proj_stage(x_ref, pos_ref, win_ref, convw_ref, convb_ref, wq_ref, wk_ref, wv_ref,
                wfold_ref, bg_ref, normg_ref, skip_ref, invf_ref,
                qm_ref, kmt_ref, vm_ref, add_ref, mul_ref, gcol_ref, grow_ref,
                qa_ref, ka_ref, vat_ref, sza_ref, qc_ref, szc_ref,
                xpad_scr, kmean_scr,
                *, t, ml_width, ml_heads, moba_width, mem_width, n_sel, gate_slots):
    tm = x_ref.shape[0]
    hd = ml_width // ml_heads
    moba_hd = moba_width // MOBA_HEADS
    mem_hd = mem_width // MEM_HEADS
    c_zm = ml_width
    c_qa = 2 * ml_width
    c_ka = c_qa + moba_width
    c_va = c_ka + moba_width
    c_za = c_va + moba_width
    c_qc = c_za + moba_width
    c_zc = c_qc + mem_width

    xb = x_ref[...].astype(BF16)
    pad = xpad_scr.shape[0] - tm
    k_w = convw_ref.shape[0]
    nb_lanes = gate_slots

    def x_cols(c0, width):
        return _dot(xb, win_ref[:, c0:c0 + width])

    def rotary_pair(p, qp, kp):
        rots, means = [], []
        for i in range(2):
            h = 2 * p + i
            ls = slice(i * moba_hd, (i + 1) * moba_hd)
            q_rot = qp[:, ls] * cosf + pltpu.roll(qp[:, ls], moba_hd // 2, 1) * sins
            k_rot = kp[:, ls] * cosf + pltpu.roll(kp[:, ls], moba_hd // 2, 1) * sins
            qa_ref[:, 2 * h * moba_hd:(2 * h + 1) * moba_hd] = (q_rot * (moba_hd ** -0.5)).astype(BF16)
            ka_ref[:, h * moba_hd:(h + 1) * moba_hd] = k_rot.astype(BF16)
            rots.append(q_rot)
            means.append(jnp.mean(k_rot, axis=0, keepdims=True))
        return rots, means

    def conv_gate_head(h, xm_h, zm_h):
        hs = slice(h * hd, (h + 1) * hd)
        xpad_scr[pad:pad + tm, hs] = xm_h
        conv = convb_ref[:, hs] + xm_h * convw_ref[k_w - 1:k_w, hs]
        for j in range(k_w - 1):
            conv = conv + xpad_scr[pl.ds(pad - (k_w - 1) + j, tm), hs] * convw_ref[j:j + 1, hs]
        xc_h = _silu(conv)
        sz_h = _silu(zm_h)
        add_ref[:, hs] = (skip_ref[:, hs] * xc_h * sz_h).astype(BF16)
        mul_ref[:, hs] = (normg_ref[:, hs] * sz_h).astype(BF16)
        return xc_h.astype(BF16), xm_h.astype(BF16)

    def blockdiag_head(h, xc_b, xm_b):
        hs = slice(h * hd, (h + 1) * hd)
        q_h = _dot(xc_b, wq_ref[h]).astype(BF16)
        k_f = _dot(xc_b, wk_ref[h])
        v_h = _dot(xm_b, wv_ref[h]).astype(BF16)
        qm_ref[:, hs] = q_h
        vm_ref[:, hs] = v_h
        kmt_ref[hs, :] = (jnp.transpose(k_f) * (hd ** -0.5)).astype(BF16)

    def gate_terms(h, xc_b, xm_b):
        hs = slice(h * hd, (h + 1) * hd)
        return _dot(xc_b, wfold_ref[0, hs, :]) + _dot(xm_b, wfold_ref[1, hs, :])

    def select_blocks(gate_t):
        n_rows = MOBA_HEADS * nb_lanes
        gate = gate_t[0:n_rows, :]
        row = lax.broadcasted_iota(jnp.int32, gate.shape, 0)
        n_r = row & (nb_lanes - 1)
        valid = n_r < t
        gate = jnp.where(valid, gate, -jnp.inf)
        cnt = jnp.zeros(gate.shape, jnp.int32)
        for r in range(1, nb_lanes):
            up = pltpu.roll(gate, r, 0)
            cnt = cnt + ((n_r >= r) & (up >= gate)).astype(jnp.int32)
            dn = pltpu.roll(gate, n_rows - r, 0)
            cnt = cnt + ((n_r < nb_lanes - r) & (dn > gate)).astype(jnp.int32)
        keep = (valid & (cnt < n_sel)) | (n_r == t)
        selb_t = jnp.where(keep, 0.0, MASK_VALUE)
        selb = jnp.transpose(jnp.concatenate(
            [selb_t, jnp.full((V7X_LANES - n_rows, tm), MASK_VALUE, F32)], axis=0))
        for h in range(MOBA_HEADS):
            shift = (V7X_LANES - h * nb_lanes) % V7X_LANES
            sel_h = selb if shift == 0 else pltpu.roll(selb, shift, 1)
            qa_ref[:, (2 * h + 1) * moba_hd:(2 * h + 2) * moba_hd] = sel_h.astype(BF16)

    pw = 2 * moba_hd
    qa0, ka0 = x_cols(c_qa, pw), x_cols(c_ka, pw)
    ang_t = invf_ref[...] * pos_ref[...]
    cos_t = jnp.cos(ang_t)
    sin_t = jnp.sin(ang_t)
    cosf = jnp.transpose(jnp.concatenate([cos_t, cos_t], axis=0))
    sins = jnp.transpose(jnp.concatenate([-sin_t, sin_t], axis=0))
    qa1, ka1 = x_cols(c_qa + pw, pw), x_cols(c_ka + pw, pw)
    rots0, means0 = rotary_pair(0, qa0, ka0)
    xm0, zm0 = x_cols(0, hd), x_cols(c_zm, hd)
    rots1, means1 = rotary_pair(1, qa1, ka1)
    q_rots, kmean_rows = rots0 + rots1, means0 + means1
    q_all = jnp.concatenate(q_rots, axis=1)
    km = kmean_scr[...]
    q_hi = q_all.astype(BF16)
    q_lo = (q_all - q_hi.astype(F32)).astype(BF16)
    k_hi = km.astype(BF16)
    k_lo = (km - k_hi.astype(F32)).astype(BF16)
    xz = {0: (xm0, zm0), 1: (x_cols(hd, hd), x_cols(c_zm + hd, hd))}
    gate = _dot_nt(k_hi, q_hi) + _dot_nt(k_hi, q_lo) + _dot_nt(k_lo, q_hi)
    pending = [("xz", h) for h in range(2, ml_heads)] + [("va", None), ("za", None)]
    qkv = []
    va = za = None
    for h in range(ml_heads):
        xc_b, xm_b = conv_gate_head(h, *xz[h])
        if pending:
            kind, arg = pending.pop(0)
            if kind == "xz":
                xz[arg] = (x_cols(arg * hd, hd), x_cols(c_zm + arg * hd, hd))
            elif kind == "va":
                va = x_cols(c_va, moba_width)
            else:
                za = x_cols(c_za, moba_width)
        blockdiag_head(h, xc_b, xm_b)
        qkv.append((xc_b, xm_b))
    if va is None:
        va = x_cols(c_va, moba_width)
    if za is None:
        za = x_cols(c_za, moba_width)
    g = jnp.zeros((tm, V7X_LANES), F32) + bg_ref[...]
    for h in range(ml_heads):
        g = g + gate_terms(h, *qkv[h])
    sza_ref[...] = _silu(za).astype(BF16)
    qc = x_cols(c_qc, mem_width)
    zc = x_cols(c_zc, mem_width)

    nrow = grow_ref.shape[0]
    gt = jnp.transpose(g)[0:V7X_SUBLANES, :]
    sub = lax.broadcasted_iota(jnp.int32, gt.shape, 0)
    tok_i = lax.broadcasted_iota(jnp.int32, gt.shape, 1)
    b = _lane_scan(jnp.where(sub >= ml_heads, _log_sigmoid(gt), 0.0), jnp.add, 0.0, tok_i)
    a = _lane_scan(jnp.where(sub >= ml_heads, pltpu.roll(gt, ml_heads, 0) - b, -jnp.inf),
                   jnp.maximum, -jnp.inf, tok_i)
    rows = jnp.concatenate([jnp.where(sub < ml_heads, gt, b), pltpu.roll(a, ml_heads, 0)], axis=0)
    grow_ref[...] = rows
    gcol_ref[...] = jnp.transpose(
        jnp.concatenate([rows, jnp.zeros((V7X_LANES - nrow, tm), F32)], axis=0))

    km_row = lax.broadcasted_iota(jnp.int32, kmean_scr.shape, 0)
    km_head = lax.broadcasted_iota(jnp.int32, kmean_scr.shape, 1) // moba_hd
    kmean_scr[...] = jnp.where(km_row == km_head * nb_lanes + t,
                               jnp.concatenate(kmean_rows, axis=1), kmean_scr[...])
    select_blocks(gate)
    vat_ref[...] = jnp.transpose(va).astype(BF16)
    qc_ref[...] = (qc * (mem_hd ** -0.5)).astype(BF16)
    szc_ref[...] = _silu(zc).astype(BF16)


def _mlstm_mem_stage(qm_ref, kmt_ref, vm_ref, gcol_ref, grow_ref, add_ref, mul_ref, om_ref,
                     qc_ref, szc_ref, kv_scr, wo_ref, mixed_ref, cn_scr, m_scr, *, heads):
    L = qm_ref.shape[0]
    ml_width = qm_ref.shape[1]
    mem_width = qc_ref.shape[1]
    mem_hd = mem_width // MEM_HEADS
    mem_slices = [slice(h * mem_hd, (h + 1) * mem_hd) for h in range(MEM_HEADS)]
    wo_mem_row0 = wo_ref.shape[0] - mem_width
    hd = qm_ref.shape[1] // heads
    row_i = lax.broadcasted_iota(jnp.int32, (L, L), 0)
    col_i = lax.broadcasted_iota(jnp.int32, (L, L), 1)
    causal = row_i >= col_i
    gcol = gcol_ref[...]
    grow = grow_ref[...]
    ones = jnp.ones((L, V7X_LANES), BF16)
    head_slices = [slice(h * hd, (h + 1) * hd) for h in range(heads)]

    qk, inter, qn, mm_l, r_l, w_inter_l, v_aug = [], [], [], [], [], [], []
    for h, hs in enumerate(head_slices):
        q = qm_ref[:, hs]
        kt = kmt_ref[hs, :]
        v_aug.append(jnp.concatenate([vm_ref[:, hs], ones], axis=1))
        i_row = grow[h:h + 1, :]
        b_row = grow[heads + h:heads + h + 1, :]
        a_col = gcol[:, 2 * heads + h:2 * heads + h + 1]
        m_prev = m_scr[h][0:1, 0:1]
        cn_prev = cn_scr[h]
        qk.append(_dot(q, kt))
        q_cn = _dot(q, cn_prev.astype(BF16))
        inter.append(q_cn[:, :hd])
        qn.append(q_cn[:, hd:])
        r_row = i_row - b_row
        mm = jnp.maximum(m_prev, a_col)
        r_l.append(r_row)
        mm_l.append(mm)
        w_inter_l.append(jnp.exp(m_prev - mm))
        b_end = b_row[:, L - 1:L]
        log_w = b_end + r_row
        m_new = jnp.maximum(b_end + m_prev, jnp.max(log_w, axis=1, keepdims=True))
        decay = jnp.exp(b_end + m_prev - m_new)
        kw = (kt.astype(F32) * jnp.exp(log_w - m_new)).astype(BF16)
        cn_scr[h] = decay * cn_prev + _dot(kw, v_aug[h])
        m_scr[h] = jnp.broadcast_to(m_new, m_scr.shape[1:])
    mem_scores = [_dot_nt(qc_ref[:, ms], kv_scr[:, ms]) for ms in mem_slices]

    s_qk = [(qk[h] * jnp.exp(jnp.where(causal, r_l[h] - mm_l[h], -jnp.inf))).astype(BF16)
            for h in range(heads)]
    mem_probs = [jnp.exp(s - jnp.max(s, axis=1, keepdims=True)) for s in mem_scores]
    pv_rs = [_dot(s_qk[h], v_aug[h]) for h in range(heads)]
    mem_outs = [_dot(p.astype(BF16), kv_scr[:, mem_width + h * mem_hd:mem_width + (h + 1) * mem_hd])
                for h, p in enumerate(mem_probs)]
    out_c = jnp.concatenate(
        [(mem_outs[h] * (1.0 / jnp.sum(mem_probs[h], axis=1, keepdims=True))
          * szc_ref[:, ms].astype(F32)).astype(BF16) for h, ms in enumerate(mem_slices)], axis=1)
    mixed = _dot(out_c, wo_ref[wo_mem_row0:, :])

    for h, hs in enumerate(head_slices):
        b_col = gcol[:, heads + h:heads + h + 1]
        num = w_inter_l[h] * inter[h] + pv_rs[h][:, :hd]
        den = w_inter_l[h] * qn[h] + pv_rs[h][:, hd:]
        rec = 1.0 / jnp.maximum(jnp.abs(den), jnp.exp(-(b_col + mm_l[h])))
        hh = num * jnp.concatenate([rec] * (hd // V7X_LANES), axis=1)
        mu = jnp.mean(hh, axis=1, keepdims=True)
        hc = hh - mu
        var = jnp.mean(hc * hc, axis=1, keepdims=True)
        hn = hc * lax.rsqrt(var + LN_EPS)
        om_ref[:, hs] = (hn * mul_ref[:, hs].astype(F32) + add_ref[:, hs].astype(F32)).astype(BF16)
    mixed_ref[...] = mixed


def _moba_stage(q_ref, k_ref, vt_ref, sz_ref, out_ref, s_scr, m_scr, *, j):
    tq = q_ref.shape[0]
    bs = MOBA_BLOCK
    nblk = k_ref.shape[0] // bs
    aw = q_ref.shape[1] // MOBA_HEADS
    hd = aw // 2
    ones_rows = 2 * V7X_SUBLANES
    heads = [slice(h * aw, (h + 1) * aw) for h in range(MOBA_HEADS)]
    kv_heads = [slice(h * hd, (h + 1) * hd) for h in range(MOBA_HEADS)]
    tag_lane = lax.broadcasted_iota(jnp.int32, (bs, hd), 1)
    ones_blk = jnp.ones((ones_rows, bs), BF16)

    n_slots = s_scr.shape[0]

    def attend(own):
        causal_t = (lax.broadcasted_iota(jnp.int32, (bs, tq), 0)
                    <= lax.broadcasted_iota(jnp.int32, (bs, tq), 1))

        def score_pass(hi):
            slot = hi % n_slots
            mx = None
            for n in range(own + 1):
                k_aug = jnp.concatenate([k_ref[n * bs:(n + 1) * bs, kv_heads[hi]],
                                         (tag_lane == n).astype(BF16)], axis=1)
                s = _dot_nt(k_aug, q_ref[:, heads[hi]])
                if n == own:
                    s = jnp.where(causal_t, s, MASK_VALUE)
                s_scr[slot, n] = s
                sm = jnp.max(s, axis=0, keepdims=True)
                mx = sm if mx is None else jnp.maximum(mx, sm)
            m_scr[slot] = jnp.broadcast_to(mx, m_scr.shape[1:])

        def value_pass(hi):
            slot = hi % n_slots
            m = m_scr[slot][0:1, :]
            acc = None
            for n in range(own + 1):
                p = jnp.exp(s_scr[slot, n] - m).astype(BF16)
                vt_aug = jnp.concatenate([vt_ref[n, kv_heads[hi], :], ones_blk], axis=0)
                pv = _dot(vt_aug, p)
                acc = pv if acc is None else acc + pv
            o_t = acc[0:hd, :] * (1.0 / acc[hd:hd + 1, :])
            out_ref[:, hi * hd:(hi + 1) * hd] = (
                jnp.transpose(o_t) * sz_ref[:, hi * hd:(hi + 1) * hd].astype(F32)).astype(BF16)

        ahead = min(n_slots, MOBA_HEADS)
        for hi in range(ahead):
            score_pass(hi)
        for hi in range(MOBA_HEADS):
            value_pass(hi)
            if hi + ahead < MOBA_HEADS:
                score_pass(hi + ahead)

    for own in range(nblk):
        pl.when(j == own)(functools.partial(attend, own))


def _out_stage(x_ref, mixed_ref, om_ref, oa_ref, wo_ref, lng_ref, lnb_ref, y_ref):
    w_m = om_ref.shape[1]
    w_a = oa_ref.shape[1]
    mixed = (mixed_ref[...] + _dot(om_ref[...], wo_ref[0:w_m, :])
             + _dot(oa_ref[...], wo_ref[w_m:w_m + w_a, :]))
    y = DEEPNORM_ALPHA * x_ref[...] + mixed
    mu = jnp.mean(y, axis=1, keepdims=True)
    yc = y - mu
    var = jnp.mean(yc * yc, axis=1, keepdims=True)
    y_ref[...] = yc * lax.rsqrt(var + LN_EPS) * lng_ref[...] + lnb_ref[...]


def _layer_kernel(x_ref, pos_ref, mem_ref, win_ref, convw_ref, convb_ref, wq_ref, wk_ref, wv_ref,
                  wg_ref, bg_ref, normg_ref, skip_ref, invf_ref, wkv_ref, wo_ref, lng_ref, lnb_ref,
                  y_ref,
                  qm_s, kmt_s, vm_s, add_s, mul_s, gcol_s, grow_s, qa_s, sza_s, qc_s, szc_s,
                  ka_s, vat_s, mixed_s, om_s, oa_s, xpad_s, kmean_s, cn_s, m_s, score_s, rowmax_s, kv_s, wfold_s,
                  *, ml_width, ml_heads, moba_width, mem_width, n_sel, gate_slots):
    t = pl.program_id(1)
    tm = x_ref.shape[0]
    pad = xpad_s.shape[0] - tm
    hd = ml_width // ml_heads

    @pl.when((pl.program_id(0) == 0) & (t == 0))
    def _():
        for h in range(ml_heads):
            rq, rk, rv = [slice(i * ml_width + h * hd, i * ml_width + (h + 1) * hd) for i in range(3)]
            wfold_s[0, rq, :] = (_dot(wq_ref[h], wg_ref[rq, :]) + _dot(wk_ref[h], wg_ref[rk, :])).astype(BF16)
            wfold_s[1, rq, :] = _dot(wv_ref[h], wg_ref[rv, :]).astype(BF16)

    @pl.when(t == 0)
    def _():
        xpad_s[0:pad, :] = jnp.zeros((pad, ml_width), F32)
        kmean_s[...] = jnp.zeros(kmean_s.shape, F32)
        cn_s[...] = jnp.zeros(cn_s.shape, F32)
        m_s[...] = jnp.zeros(m_s.shape, F32)
        kv_s[...] = _dot(mem_ref[...].astype(BF16), wkv_ref[...]).astype(BF16)

    @pl.when(t > 0)
    def _():
        xpad_s[0:pad, :] = xpad_s[tm:tm + pad, :]

    rows = pl.ds(pl.multiple_of(t * tm, tm), tm)
    _proj_stage(x_ref, pos_ref, win_ref, convw_ref, convb_ref, wq_ref, wk_ref, wv_ref,
                wfold_s, bg_ref, normg_ref, skip_ref, invf_ref,
                qm_s, kmt_s, vm_s, add_s, mul_s, gcol_s, grow_s,
                qa_s, ka_s.at[rows], vat_s.at[t], sza_s, qc_s, szc_s, xpad_s, kmean_s,
                t=t, ml_width=ml_width, ml_heads=ml_heads, moba_width=moba_width,
                mem_width=mem_width, n_sel=n_sel, gate_slots=gate_slots)
    _mlstm_mem_stage(qm_s, kmt_s, vm_s, gcol_s, grow_s, add_s, mul_s, om_s, qc_s, szc_s, kv_s, wo_ref,
                     mixed_s, cn_s, m_s, heads=ml_heads)
    _moba_stage(qa_s, ka_s, vat_s, sza_s, oa_s, score_s, rowmax_s, j=t)
    _out_stage(x_ref, mixed_s, om_s, oa_s, wo_ref, lng_ref, lnb_ref, y_ref)


def _layer_call(x, pos, mem, win, convw, convb, wq, wk, wv, wg, bg, normg, skip, invf, wkv, wo, lng, lnb,
                *, ml_width, ml_heads, moba_width, mem_width):
    bsz, s, d = x.shape
    m_tok = mem.shape[1]
    tm = TOKEN_TILE
    nt = s // tm
    n_sel = min(MOBA_TOP_K, nt - 1)
    gate_slots = max(V7X_SUBLANES, pl.next_power_of_2(nt))
    assert MOBA_HEADS * gate_slots <= V7X_LANES and s % tm == 0 and ml_heads >= 2
    hd = ml_width // ml_heads
    tok = lambda w: pl.BlockSpec((None, tm, w), lambda b, t: (b, t, 0))

    def resident(a):
        nd = a.ndim
        return pl.BlockSpec(a.shape, lambda b, t, _nd=nd: (0,) * _nd, pipeline_mode=pl.Buffered(1))

    weights = (win, convw, convb, wq, wk, wv, wg, bg, normg, skip, invf, wkv, wo, lng, lnb)
    in_specs = ([tok(d),
                 pl.BlockSpec((None, None, 1, tm), lambda b, t: (b, t, 0, 0)),
                 pl.BlockSpec((None, m_tok, d), lambda b, t: (b, 0, 0))]
                + [resident(a) for a in weights])
    vmem = pltpu.VMEM
    score_heads = MOBA_HEADS
    scratch = [
        vmem((tm, ml_width), BF16),
        vmem((ml_width, tm), BF16),
        vmem((tm, ml_width), BF16),
        vmem((tm, ml_width), BF16),
        vmem((tm, ml_width), BF16),
        vmem((tm, V7X_LANES), F32),
        vmem((GATE_ROWS, tm), F32),
        vmem((tm, 2 * moba_width), BF16),
        vmem((tm, moba_width), BF16),
        vmem((tm, mem_width), BF16),
        vmem((tm, mem_width), BF16),
        vmem((s, moba_width), BF16),
        vmem((nt, moba_width, tm), BF16),
        vmem((tm, d), F32),
        vmem((tm, ml_width), BF16),
        vmem((tm, moba_width), BF16),
        vmem((tm + V7X_SUBLANES, ml_width), F32),
        vmem((V7X_LANES, moba_width), F32),
        vmem((ml_heads, hd, hd + V7X_LANES), F32),
        vmem((ml_heads, V7X_SUBLANES, V7X_LANES), F32),
        vmem((score_heads, nt, tm, MOBA_BLOCK), F32),
        vmem((score_heads, V7X_SUBLANES, tm), F32),
        vmem((m_tok, 2 * mem_width), BF16),
        vmem((2, ml_width, V7X_LANES), BF16),
    ]
    kern = functools.partial(_layer_kernel, ml_width=ml_width, ml_heads=ml_heads, moba_width=moba_width,
                             mem_width=mem_width, n_sel=n_sel, gate_slots=gate_slots)
    return pl.pallas_call(
        kern,
        grid=(bsz, nt),
        in_specs=in_specs,
        out_specs=tok(d),
        out_shape=jax.ShapeDtypeStruct((bsz, s, d), x.dtype),
        scratch_shapes=scratch,
        compiler_params=pltpu.CompilerParams(dimension_semantics=("arbitrary", "arbitrary"),
                                             vmem_limit_bytes=V7X_VMEM_LIMIT_BYTES),
        name="layer",
    )(x, pos.reshape(bsz, nt, 1, tm), mem, *weights)


def _diag_tiles(w, heads):
    groups, blk, _ = w.shape
    hd = groups // heads * blk
    rows = w.reshape(heads, hd, blk)
    idx = jnp.arange(hd)
    same_group = (idx[:, None] // blk) == (idx[None, :] // blk)
    return jnp.where(same_group, jnp.tile(rows, (1, 1, hd // blk)), 0.0)


def kernel(x, mem, positions, w_in, mlstm_conv_w, mlstm_conv_b, mlstm_wq, mlstm_wk, mlstm_wv, mlstm_w_gates, mlstm_b_gates, mlstm_norm_g, mlstm_skip, w_mem_kv, w_out, ln_g, ln_b):
    bsz, s, d = x.shape
    ml_width = mlstm_conv_w.shape[1]
    ml_heads = mlstm_b_gates.shape[0] // 2
    mem_width = w_mem_kv.shape[1] // 2
    moba_width = (w_in.shape[1] - 2 * ml_width - 2 * mem_width) // 4
    moba_hd = moba_width // MOBA_HEADS
    assert moba_hd == V7X_LANES and mem_width // MEM_HEADS == V7X_LANES and MOBA_HEADS == 4
    assert s % TOKEN_TILE == 0

    row = lambda a: a.reshape(1, -1).astype(F32)
    wq = _diag_tiles(mlstm_wq, ml_heads).astype(BF16)
    wk = _diag_tiles(mlstm_wk, ml_heads).astype(BF16)
    wv = _diag_tiles(mlstm_wv, ml_heads).astype(BF16)
    n_gates = mlstm_w_gates.shape[1]
    wg = jnp.pad(mlstm_w_gates, ((0, 0), (0, V7X_LANES - n_gates))).astype(BF16)
    bg = jnp.pad(mlstm_b_gates, (0, V7X_LANES - n_gates)).reshape(1, -1).astype(F32)
    half = moba_hd // 2
    inv_freq = ROPE_THETA ** (-jnp.arange(half, dtype=F32) * 2.0 / moba_hd)
    invf = inv_freq.reshape(-1, 1)
    pos = positions.astype(F32)

    return _layer_call(
        x, pos, mem, w_in.astype(BF16), mlstm_conv_w.astype(F32), row(mlstm_conv_b), wq, wk, wv, wg, bg,
        row(mlstm_norm_g), row(mlstm_skip), invf, w_mem_kv.astype(BF16), w_out.astype(BF16),
        row(ln_g), row(ln_b),
        ml_width=ml_width, ml_heads=ml_heads, moba_width=moba_width, mem_width=mem_width)
```

```python
import functools

import jax
import jax.numpy as jnp
from jax import lax
from jax.experimental import pallas as pl
from jax.experimental.pallas import tpu as pltpu

MOBA_HEADS = 4
MOBA_BLOCK = 256
MOBA_TOP_K = 3
MEM_HEADS = 4
ROPE_THETA = 10000.0
DEPTH = 1
DEEPNORM_ALPHA = (2 * DEPTH) ** 0.25
LN_EPS = 1e-5

V7X_LANES = 128
V7X_SUBLANES = 8
V7X_VMEM_BYTES = 64 * 1024 * 1024
V7X_VMEM_LIMIT_BYTES = V7X_VMEM_BYTES - 8 * 1024 * 1024

TOKEN_TILE = MOBA_BLOCK
MASK_VALUE = -1e30
GATE_ROWS = 2 * V7X_SUBLANES

F32 = jnp.float32
BF16 = jnp.bfloat16


def _dot(a, b):
    return jnp.dot(a, b, preferred_element_type=F32)


def _dot_nt(a, b):
    return lax.dot_general(a, b, (((1,), (1,)), ((), ())), preferred_element_type=F32)


def _lane_scan(v, op, identity, lane_idx):
    shift = 1
    while shift < v.shape[-1]:
        v = op(v, jnp.where(lane_idx >= shift, pltpu.roll(v, shift, v.ndim - 1), identity))
        shift *= 2
    return v


def _silu(v):
    h = 0.5 * v
    return h + h * jnp.tanh(h)


def _log_sigmoid(v):
    return jnp.minimum(v, 0.0) - jnp.log1p(jnp.exp(-jnp.abs(v)))


def _proj_stage(x_ref, pos_ref, win_ref, convw_ref, convb_ref, wq_ref, wk_ref, wv_ref,
                wfold_ref, bg_ref, normg_ref, skip_ref, invf_ref,
                qm_ref, kmt_ref, vm_ref, add_ref, mul_ref, gcol_ref, grow_ref,
                qa_ref, ka_ref, vat_ref, sza_ref, qc_ref, szc_ref,
                xpad_scr, kmean_scr,
                *, t, ml_width, ml_heads, moba_width, mem_width, n_sel, gate_slots):
    tm = x_ref.shape[0]
    hd = ml_width // ml_heads
    moba_hd = moba_width // MOBA_HEADS
    mem_hd = mem_width // MEM_HEADS
    c_zm = ml_width
    c_qa = 2 * ml_width
    c_ka = c_qa + moba_width
    c_va = c_ka + moba_width
    c_za = c_va + moba_width
    c_qc = c_za + moba_width
    c_zc = c_qc + mem_width

    xb = x_ref[...].astype(BF16)
    pad = xpad_scr.shape[0] - tm
    k_w = convw_ref.shape[0]
    nb_lanes = gate_slots

    def x_cols(c0, width):
        return _dot(xb, win_ref[:, c0:c0 + width])

    def rotary_pair(p, qp, kp):
        rots, means = [], []
        for i in range(2):
            h = 2 * p + i
            ls = slice(i * moba_hd, (i + 1) * moba_hd)
            q_rot = qp[:, ls] * cosf + pltpu.roll(qp[:, ls], moba_hd // 2, 1) * sins
            k_rot = kp[:, ls] * cosf + pltpu.roll(kp[:, ls], moba_hd // 2, 1) * sins
            qa_ref[:, 2 * h * moba_hd:(2 * h + 1) * moba_hd] = (q_rot * (moba_hd ** -0.5)).astype(BF16)
            ka_ref[:, h * moba_hd:(h + 1) * moba_hd] = k_rot.astype(BF16)
            rots.append(q_rot)
            means.append(jnp.mean(k_rot, axis=0, keepdims=True))
        return rots, means

    def conv_gate_head(h, xm_h, zm_h):
        hs = slice(h * hd, (h + 1) * hd)
        xpad_scr[pad:pad + tm, hs] = xm_h
        conv = convb_ref[:, hs] + xm_h * convw_ref[k_w - 1:k_w, hs]
        for j in range(k_w - 1):
            conv = conv + xpad_scr[pl.ds(pad - (k_w - 1) + j, tm), hs] * convw_ref[j:j + 1, hs]
        xc_h = _silu(conv)
        sz_h = _silu(zm_h)
        add_ref[:, hs] = (skip_ref[:, hs] * xc_h * sz_h).astype(BF16)
        mul_ref[:, hs] = (normg_ref[:, hs] * sz_h).astype(BF16)
        return xc_h.astype(BF16), xm_h.astype(BF16)

    def blockdiag_head(h, xc_b, xm_b):
        hs = slice(h * hd, (h + 1) * hd)
        q_h = _dot(xc_b, wq_ref[h]).astype(BF16)
        k_f = _dot(xc_b, wk_ref[h])
        v_h = _dot(xm_b, wv_ref[h]).astype(BF16)
        qm_ref[:, hs] = q_h
        vm_ref[:, hs] = v_h
        kmt_ref[hs, :] = (jnp.transpose(k_f) * (hd ** -0.5)).astype(BF16)

    def gate_terms(h, xc_b, xm_b):
        hs = slice(h * hd, (h + 1) * hd)
        return _dot(xc_b, wfold_ref[0, hs, :]) + _dot(xm_b, wfold_ref[1, hs, :])

    def select_blocks(gate_t):
        n_rows = MOBA_HEADS * nb_lanes
        gate = gate_t[0:n_rows, :]
        row = lax.broadcasted_iota(jnp.int32, gate.shape, 0)
        n_r = row & (nb_lanes - 1)
        valid = n_r < t
        gate = jnp.where(valid, gate, -jnp.inf)
        cnt = jnp.zeros(gate.shape, jnp.int32)
        for r in range(1, nb_lanes):
            up = pltpu.roll(gate, r, 0)
            cnt = cnt + ((n_r >= r) & (up >= gate)).astype(jnp.int32)
            dn = pltpu.roll(gate, n_rows - r, 0)
            cnt = cnt + ((n_r < nb_lanes - r) & (dn > gate)).astype(jnp.int32)
        keep = (valid & (cnt < n_sel)) | (n_r == t)
        selb_t = jnp.where(keep, 0.0, MASK_VALUE)
        selb = jnp.transpose(jnp.concatenate(
            [selb_t, jnp.full((V7X_LANES - n_rows, tm), MASK_VALUE, F32)], axis=0))
        for h in range(MOBA_HEADS):
            shift = (V7X_LANES - h * nb_lanes) % V7X_LANES
            sel_h = selb if shift == 0 else pltpu.roll(selb, shift, 1)
            qa_ref[:, (2 * h + 1) * moba_hd:(2 * h + 2) * moba_hd] = sel_h.astype(BF16)

    pw = 2 * moba_hd
    qa0, ka0 = x_cols(c_qa, pw), x_cols(c_ka, pw)
    ang_t = invf_ref[...] * pos_ref[...]
    cos_t = jnp.cos(ang_t)
    sin_t = jnp.sin(ang_t)
    cosf = jnp.transpose(jnp.concatenate([cos_t, cos_t], axis=0))
    sins = jnp.transpose(jnp.concatenate([-sin_t, sin_t], axis=0))
    qa1, ka1 = x_cols(c_qa + pw, pw), x_cols(c_ka + pw, pw)
    rots0, means0 = rotary_pair(0, qa0, ka0)
    xm0, zm0 = x_cols(0, hd), x_cols(c_zm, hd)
    rots1, means1 = rotary_pair(1, qa1, ka1)
    q_rots, kmean_rows = rots0 + rots1, means0 + means1
    q_all = jnp.concatenate(q_rots, axis=1)
    km = kmean_scr[...]
    q_hi = q_all.astype(BF16)
    q_lo = (q_all - q_hi.astype(F32)).astype(BF16)
    k_hi = km.astype(BF16)
    k_lo = (km - k_hi.astype(F32)).astype(BF16)
    xz = {0: (xm0, zm0), 1: (x_cols(hd, hd), x_cols(c_zm + hd, hd))}
    gate = _dot_nt(k_hi, q_hi) + _dot_nt(k_hi, q_lo) + _dot_nt(k_lo, q_hi)
    pending = [("xz", h) for h in range(2, ml_heads)] + [("va", None), ("za", None)]
    qkv = []
    va = za = None
    for h in range(ml_heads):
        xc_b, xm_b = conv_gate_head(h, *xz[h])
        if pending:
            kind, arg = pending.pop(0)
            if kind == "xz":
                xz[arg] = (x_cols(arg * hd, hd), x_cols(c_zm + arg * hd, hd))
            elif kind == "va":
                va = x_cols(c_va, moba_width)
            else:
                za = x_cols(c_za, moba_width)
        blockdiag_head(h, xc_b, xm_b)
        qkv.append((xc_b, xm_b))
    if va is None:
        va = x_cols(c_va, moba_width)
    if za is None:
        za = x_cols(c_za, moba_width)
    g = jnp.zeros((tm, V7X_LANES), F32) + bg_ref[...]
    for h in range(ml_heads):
        g = g + gate_terms(h, *qkv[h])
    sza_ref[...] = _silu(za).astype(BF16)
    qc = x_cols(c_qc, mem_width)
    zc = x_cols(c_zc, mem_width)

    nrow = grow_ref.shape[0]
    gt = jnp.transpose(g)[0:V7X_SUBLANES, :]
    sub = lax.broadcasted_iota(jnp.int32, gt.shape, 0)
    tok_i = lax.broadcasted_iota(jnp.int32, gt.shape, 1)
    b = _lane_scan(jnp.where(sub >= ml_heads, _log_sigmoid(gt), 0.0), jnp.add, 0.0, tok_i)
    a = _lane_scan(jnp.where(sub >= ml_heads, pltpu.roll(gt, ml_heads, 0) - b, -jnp.inf),
                   jnp.maximum, -jnp.inf, tok_i)
    rows = jnp.concatenate([jnp.where(sub < ml_heads, gt, b), pltpu.roll(a, ml_heads, 0)], axis=0)
    grow_ref[...] = rows
    gcol_ref[...] = jnp.transpose(
        jnp.concatenate([rows, jnp.zeros((V7X_LANES - nrow, tm), F32)], axis=0))

    km_row = lax.broadcasted_iota(jnp.int32, kmean_scr.shape, 0)
    km_head = lax.broadcasted_iota(jnp.int32, kmean_scr.shape, 1) // moba_hd
    kmean_scr[...] = jnp.where(km_row == km_head * nb_lanes + t,
                               jnp.concatenate(kmean_rows, axis=1), kmean_scr[...])
    select_blocks(gate)
    vat_ref[...] = jnp.transpose(va).astype(BF16)
    qc_ref[...] = (qc * (mem_hd ** -0.5)).astype(BF16)
    szc_ref[...] = _silu(zc).astype(BF16)


def _mlstm_stage(qm_ref, kmt_ref, vm_ref, gcol_ref, grow_ref, add_ref, mul_ref, out_ref,
                 cn_scr, m_scr, *, heads):
    L = qm_ref.shape[0]
    hd = qm_ref.shape[1] // heads
    row_i = lax.broadcasted_iota(jnp.int32, (L, L), 0)
    col_i = lax.broadcasted_iota(jnp.int32, (L, L), 1)
    causal = row_i >= col_i
    gcol = gcol_ref[...]
    grow = grow_ref[...]
    ones = jnp.ones((L, V7X_LANES), BF16)
    head_slices = [slice(h * hd, (h + 1) * hd) for h in range(heads)]

    qk, inter, qn, mm_l, r_l, w_inter_l, v_aug, cn_prevs = [], [], [], [], [], [], [], []
    for h, hs in enumerate(head_slices):
        q = qm_ref[:, hs]
        cn_prevs.append(cn_scr[h])
        qk.append(_dot(q, kmt_ref[hs, :]))
        q_cn = _dot(q, cn_prevs[h].astype(BF16))
        inter.append(q_cn[:, :hd])
        qn.append(q_cn[:, hd:])
    for h, hs in enumerate(head_slices):
        kt = kmt_ref[hs, :]
        v_aug.append(jnp.concatenate([vm_ref[:, hs], ones], axis=1))
        i_row = grow[h:h + 1, :]
        b_row = grow[heads + h:heads + h + 1, :]
        a_rep = jnp.broadcast_to(gcol[:, 2 * heads + h:2 * heads + h + 1], (L, V7X_LANES))
        m_prev = m_scr[h][0:1, 0:1]
        cn_prev = cn_prevs[h]
        r_row = i_row - b_row
        mm = jnp.maximum(m_prev, a_rep)
        r_l.append(r_row)
        mm_l.append(mm)
        w_inter_l.append(jnp.exp(m_prev - mm))
        b_end = b_row[:, L - 1:L]
        log_w = b_end + r_row
        m_new = jnp.maximum(b_end + m_prev, jnp.max(log_w, axis=1, keepdims=True))
        decay = jnp.exp(b_end + m_prev - m_new)
        kw = (kt.astype(F32) * jnp.exp(log_w - m_new)).astype(BF16)
        cn_scr[h] = decay * cn_prev + _dot(kw, v_aug[h])
        m_scr[h] = jnp.broadcast_to(m_new, m_scr.shape[1:])

    wide = lambda rep, width: jnp.concatenate([rep] * (width // V7X_LANES), axis=1)
    s_qk = [(qk[h] * jnp.exp(jnp.where(causal, r_l[h] - wide(mm_l[h], L), -jnp.inf))).astype(BF16)
            for h in range(heads)]
    pv_rs = [_dot(s_qk[h], v_aug[h]) for h in range(heads)]

    for h, hs in enumerate(head_slices):
        b_rep = jnp.broadcast_to(gcol[:, heads + h:heads + h + 1], (L, V7X_LANES))
        num = wide(w_inter_l[h], hd) * inter[h] + pv_rs[h][:, :hd]
        den = w_inter_l[h] * qn[h] + pv_rs[h][:, hd:]
        rec = 1.0 / jnp.maximum(jnp.abs(den), jnp.exp(-(b_rep + mm_l[h])))
        hh = num * wide(rec, hd)
        mu = jnp.mean(hh, axis=1, keepdims=True)
        hc = hh - mu
        var = jnp.mean(hc * hc, axis=1, keepdims=True)
        hn = hc * lax.rsqrt(var + LN_EPS)
        out_ref[:, hs] = (hn * mul_ref[:, hs].astype(F32) + add_ref[:, hs].astype(F32)).astype(BF16)


def _moba_stage(q_ref, k_ref, vt_ref, sz_ref, out_ref, s_scr, m_scr, *, j):
    tq = q_ref.shape[0]
    bs = MOBA_BLOCK
    nblk = k_ref.shape[0] // bs
    aw = q_ref.shape[1] // MOBA_HEADS
    hd = aw // 2
    ones_rows = 2 * V7X_SUBLANES
    heads = [slice(h * aw, (h + 1) * aw) for h in range(MOBA_HEADS)]
    kv_heads = [slice(h * hd, (h + 1) * hd) for h in range(MOBA_HEADS)]
    tag_lane = lax.broadcasted_iota(jnp.int32, (bs, hd), 1)
    ones_blk = jnp.ones((ones_rows, bs), BF16)

    n_slots = s_scr.shape[0]

    def attend(own):
        causal_t = (lax.broadcasted_iota(jnp.int32, (bs, tq), 0)
                    <= lax.broadcasted_iota(jnp.int32, (bs, tq), 1))

        def score_pass(hi):
            slot = hi % n_slots
            mx = None
            for n in range(own + 1):
                k_aug = jnp.concatenate([k_ref[n * bs:(n + 1) * bs, kv_heads[hi]],
                                         (tag_lane == n).astype(BF16)], axis=1)
                s = _dot_nt(k_aug, q_ref[:, heads[hi]])
                if n == own:
                    s = jnp.where(causal_t, s, MASK_VALUE)
                s_scr[slot, n] = s
                sm = jnp.max(s, axis=0, keepdims=True)
                mx = sm if mx is None else jnp.maximum(mx, sm)
            m_scr[slot] = jnp.broadcast_to(mx, m_scr.shape[1:])

        def value_pass(hi):
            slot = hi % n_slots
            m = m_scr[slot][0:1, :]
            acc = None
            for n in range(own + 1):
                p = jnp.exp(s_scr[slot, n] - m).astype(BF16)
                vt_aug = jnp.concatenate([vt_ref[n, kv_heads[hi], :], ones_blk], axis=0)
                pv = _dot(vt_aug, p)
                acc = pv if acc is None else acc + pv
            o_t = acc[0:hd, :] * (1.0 / acc[hd:hd + 1, :])
            out_ref[:, hi * hd:(hi + 1) * hd] = (
                jnp.transpose(o_t) * sz_ref[:, hi * hd:(hi + 1) * hd].astype(F32)).astype(BF16)

        ahead = min(n_slots, MOBA_HEADS)
        for hi in range(ahead):
            score_pass(hi)
        for hi in range(MOBA_HEADS):
            value_pass(hi)
            if hi + ahead < MOBA_HEADS:
                score_pass(hi + ahead)

    for own in range(nblk):
        pl.when(j == own)(functools.partial(attend, own))


def _mix_stage(x_ref, om_ref, oa_ref, qc_ref, szc_ref, kv_scr, wo_ref, lng_ref, lnb_ref, y_ref):
    width = qc_ref.shape[1]
    hd = width // MEM_HEADS
    w_m = om_ref.shape[1]
    w_a = oa_ref.shape[1]
    head_slices = [slice(h * hd, (h + 1) * hd) for h in range(MEM_HEADS)]
    scores = [_dot_nt(qc_ref[:, hs], kv_scr[:, hs]) for hs in head_slices]
    mixed = _dot(oa_ref[...], wo_ref[w_m:w_m + w_a, :])
    probs = [jnp.exp(s - jnp.max(s, axis=1, keepdims=True)) for s in scores]
    outs = [_dot(p.astype(BF16), kv_scr[:, width + h * hd:width + (h + 1) * hd])
            for h, p in enumerate(probs)]
    oc = jnp.concatenate(
        [(outs[h] * (1.0 / jnp.sum(probs[h], axis=1, keepdims=True))
          * szc_ref[:, hs].astype(F32)).astype(BF16) for h, hs in enumerate(head_slices)], axis=1)
    mixed = mixed + _dot(oc, wo_ref[w_m + w_a:, :])
    mixed = mixed + _dot(om_ref[...], wo_ref[0:w_m, :])
    y = DEEPNORM_ALPHA * x_ref[...] + mixed
    mu = jnp.mean(y, axis=1, keepdims=True)
    yc = y - mu
    var = jnp.mean(yc * yc, axis=1, keepdims=True)
    y_ref[...] = yc * lax.rsqrt(var + LN_EPS) * lng_ref[...] + lnb_ref[...]


def _layer_kernel(x_ref, pos_ref, mem_ref, win_ref, convw_ref, convb_ref, wq_ref, wk_ref, wv_ref,
                  wg_ref, bg_ref, normg_ref, skip_ref, invf_ref, wkv_ref, wo_ref, lng_ref, lnb_ref,
                  y_ref,
                  qm_s, kmt_s, vm_s, add_s, mul_s, gcol_s, grow_s, qa_s, sza_s, qc_s, szc_s,
                  ka_s, vat_s, om_s, oa_s, xpad_s, kmean_s, cn_s, m_s, score_s, rowmax_s, kv_s, wfold_s,
                  *, ml_width, ml_heads, moba_width, mem_width, n_sel, gate_slots):
    t = pl.program_id(1)
    tm = x_ref.shape[0]
    pad = xpad_s.shape[0] - tm
    hd = ml_width // ml_heads

    @pl.when((pl.program_id(0) == 0) & (t == 0))
    def _():
        for h in range(ml_heads):
            rq, rk, rv = [slice(i * ml_width + h * hd, i * ml_width + (h + 1) * hd) for i in range(3)]
            wfold_s[0, rq, :] = (_dot(wq_ref[h], wg_ref[rq, :]) + _dot(wk_ref[h], wg_ref[rk, :])).astype(BF16)
            wfold_s[1, rq, :] = _dot(wv_ref[h], wg_ref[rv, :]).astype(BF16)

    @pl.when(t == 0)
    def _():
        xpad_s[0:pad, :] = jnp.zeros((pad, ml_width), F32)
        kmean_s[...] = jnp.zeros(kmean_s.shape, F32)
        cn_s[...] = jnp.zeros(cn_s.shape, F32)
        m_s[...] = jnp.zeros(m_s.shape, F32)
        kv_s[...] = _dot(mem_ref[...].astype(BF16), wkv_ref[...]).astype(BF16)

    @pl.when(t > 0)
    def _():
        xpad_s[0:pad, :] = xpad_s[tm:tm + pad, :]

    rows = pl.ds(pl.multiple_of(t * tm, tm), tm)
    _proj_stage(x_ref, pos_ref, win_ref, convw_ref, convb_ref, wq_ref, wk_ref, wv_ref,
                wfold_s, bg_ref, normg_ref, skip_ref, invf_ref,
                qm_s, kmt_s, vm_s, add_s, mul_s, gcol_s, grow_s,
                qa_s, ka_s.at[rows], vat_s.at[t], sza_s, qc_s, szc_s, xpad_s, kmean_s,
                t=t, ml_width=ml_width, ml_heads=ml_heads, moba_width=moba_width,
                mem_width=mem_width, n_sel=n_sel, gate_slots=gate_slots)
    _mlstm_stage(qm_s, kmt_s, vm_s, gcol_s, grow_s, add_s, mul_s, om_s, cn_s, m_s, heads=ml_heads)
    _moba_stage(qa_s, ka_s, vat_s, sza_s, oa_s, score_s, rowmax_s, j=t)
    _mix_stage(x_ref, om_s, oa_s, qc_s, szc_s, kv_s, wo_ref, lng_ref, lnb_ref, y_ref)


def _layer_call(x, pos, mem, win, convw, convb, wq, wk, wv, wg, bg, normg, skip, invf, wkv, wo, lng, lnb,
                *, ml_width, ml_heads, moba_width, mem_width):
    bsz, s, d = x.shape
    m_tok = mem.shape[1]
    tm = TOKEN_TILE
    nt = s // tm
    n_sel = min(MOBA_TOP_K, nt - 1)
    gate_slots = max(V7X_SUBLANES, pl.next_power_of_2(nt))
    assert MOBA_HEADS * gate_slots <= V7X_LANES and s % tm == 0 and ml_heads >= 2
    hd = ml_width // ml_heads
    tok = lambda w: pl.BlockSpec((None, tm, w), lambda b, t: (b, t, 0))

    def resident(a):
        nd = a.ndim
        return pl.BlockSpec(a.shape, lambda b, t, _nd=nd: (0,) * _nd, pipeline_mode=pl.Buffered(1))

    weights = (win, convw, convb, wq, wk, wv, wg, bg, normg, skip, invf, wkv, wo, lng, lnb)
    in_specs = ([tok(d),
                 pl.BlockSpec((None, None, 1, tm), lambda b, t: (b, t, 0, 0)),
                 pl.BlockSpec((None, m_tok, d), lambda b, t: (b, 0, 0))]
                + [resident(a) for a in weights])
    vmem = pltpu.VMEM
    score_heads = MOBA_HEADS
    scratch = [
        vmem((tm, ml_width), BF16),
        vmem((ml_width, tm), BF16),
        vmem((tm, ml_width), BF16),
        vmem((tm, ml_width), BF16),
        vmem((tm, ml_width), BF16),
        vmem((tm, V7X_LANES), F32),
        vmem((GATE_ROWS, tm), F32),
        vmem((tm, 2 * moba_width), BF16),
        vmem((tm, moba_width), BF16),
        vmem((tm, mem_width), BF16),
        vmem((tm, mem_width), BF16),
        vmem((s, moba_width), BF16),
        vmem((nt, moba_width, tm), BF16),
        vmem((tm, ml_width), BF16),
        vmem((tm, moba_width), BF16),
        vmem((tm + V7X_SUBLANES, ml_width), F32),
        vmem((V7X_LANES, moba_width), F32),
        vmem((ml_heads, hd, hd + V7X_LANES), F32),
        vmem((ml_heads, V7X_SUBLANES, V7X_LANES), F32),
        vmem((score_heads, nt, tm, MOBA_BLOCK), F32),
        vmem((score_heads, V7X_SUBLANES, tm), F32),
        vmem((m_tok, 2 * mem_width), BF16),
        vmem((2, ml_width, V7X_LANES), BF16),
    ]
    kern = functools.partial(_layer_kernel, ml_width=ml_width, ml_heads=ml_heads, moba_width=moba_width,
                             mem_width=mem_width, n_sel=n_sel, gate_slots=gate_slots)
    return pl.pallas_call(
        kern,
        grid=(bsz, nt),
        in_specs=in_specs,
        out_specs=tok(d),
        out_shape=jax.ShapeDtypeStruct((bsz, s, d), x.dtype),
        scratch_shapes=scratch,
        compiler_params=pltpu.CompilerParams(dimension_semantics=("arbitrary", "arbitrary"),
                                             vmem_limit_bytes=V7X_VMEM_LIMIT_BYTES),
        name="layer",
    )(x, pos.reshape(bsz, nt, 1, tm), mem, *weights)


def _diag_tiles(w, heads):
    groups, blk, _ = w.shape
    hd = groups // heads * blk
    rows = w.reshape(heads, hd, blk)
    idx = jnp.arange(hd)
    same_group = (idx[:, None] // blk) == (idx[None, :] // blk)
    return jnp.where(same_group, jnp.tile(rows, (1, 1, hd // blk)), 0.0)


def kernel(x, mem, positions, w_in, mlstm_conv_w, mlstm_conv_b, mlstm_wq, mlstm_wk, mlstm_wv, mlstm_w_gates, mlstm_b_gates, mlstm_norm_g, mlstm_skip, w_mem_kv, w_out, ln_g, ln_b):
    bsz, s, d = x.shape
    ml_width = mlstm_conv_w.shape[1]
    ml_heads = mlstm_b_gates.shape[0] // 2
    mem_width = w_mem_kv.shape[1] // 2
    moba_width = (w_in.shape[1] - 2 * ml_width - 2 * mem_width) // 4
    moba_hd = moba_width // MOBA_HEADS
    assert moba_hd == V7X_LANES and mem_width // MEM_HEADS == V7X_LANES and MOBA_HEADS == 4
    assert s % TOKEN_TILE == 0

    row = lambda a: a.reshape(1, -1).astype(F32)
    wq = _diag_tiles(mlstm_wq, ml_heads).astype(BF16)
    wk = _diag_tiles(mlstm_wk, ml_heads).astype(BF16)
    wv = _diag_tiles(mlstm_wv, ml_heads).astype(BF16)
    n_gates = mlstm_w_gates.shape[1]
    wg = jnp.pad(mlstm_w_gates, ((0, 0), (0, V7X_LANES - n_gates))).astype(BF16)
    bg = jnp.pad(mlstm_b_gates, (0, V7X_LANES - n_gates)).reshape(1, -1).astype(F32)
    half = moba_hd // 2
    inv_freq = ROPE_THETA ** (-jnp.arange(half, dtype=F32) * 2.0 / moba_hd)
    invf = inv_freq.reshape(-1, 1)
    pos = positions.astype(F32)

    return _layer_call(
        x, pos, mem, w_in.astype(BF16), mlstm_conv_w.astype(F32), row(mlstm_conv_b), wq, wk, wv, wg, bg,
        row(mlstm_norm_g), row(mlstm_skip), invf, w_mem_kv.astype(BF16), w_out.astype(BF16),
        row(ln_g), row(ln_b),
        ml_width=ml_width, ml_heads=ml_heads, moba_width=moba_width, mem_width=mem_width)
```

```python
import functools

import jax
import jax.numpy as jnp
from jax import lax
from jax.experimental import pallas as pl
from jax.experimental.pallas import tpu as pltpu

MOBA_HEADS = 4
MOBA_BLOCK = 256
MOBA_TOP_K = 3
MEM_HEADS = 4
ROPE_THETA = 10000.0
DEPTH = 1
DEEPNORM_ALPHA = (2 * DEPTH) ** 0.25
LN_EPS = 1e-5

V7X_LANES = 128
V7X_SUBLANES = 8
V7X_VMEM_BYTES = 64 * 1024 * 1024
V7X_VMEM_LIMIT_BYTES = V7X_VMEM_BYTES - 8 * 1024 * 1024

TOKEN_TILE = MOBA_BLOCK
MASK_VALUE = -1e30
GATE_ROWS = 2 * V7X_SUBLANES
STAGE_ROWS, STAGE_COLS = 256, 1024

F32 = jnp.float32
BF16 = jnp.bfloat16


def _dot(a, b):
    return jnp.dot(a, b, preferred_element_type=F32)


def _dot_nt(a, b):
    return lax.dot_general(a, b, (((1,), (1,)), ((), ())), preferred_element_type=F32)


def _lane_scan(v, op, identity, lane_idx):
    shift = 1
    while shift < v.shape[-1]:
        v = op(v, jnp.where(lane_idx >= shift, pltpu.roll(v, shift, v.ndim - 1), identity))
        shift *= 2
    return v


def _silu(v):
    h = 0.5 * v
    return h + h * jnp.tanh(h)


def _log_sigmoid(v):
    return jnp.minimum(v, 0.0) - jnp.log1p(jnp.exp(-jnp.abs(v)))


def _proj_stage(x_ref, pos_ref, win_ref, convw_ref, convb_ref, wq_ref, wk_ref, wv_ref,
                wfold_ref, bg_ref, normg_ref, skip_ref, invf_ref,
                qm_ref, kmt_ref, vm_ref, add_ref, mul_ref, gcol_ref, grow_ref,
                qa_ref, ka_ref, vat_ref, sza_ref, qc_ref, szc_ref,
                xpad_scr, kmean_scr,
                *, t, ml_width, ml_heads, moba_width, mem_width, n_sel, gate_slots):
    tm = x_ref.shape[0]
    hd = ml_width // ml_heads
    moba_hd = moba_width // MOBA_HEADS
    mem_hd = mem_width // MEM_HEADS
    c_zm = ml_width
    c_qa = 2 * ml_width
    c_ka = c_qa + moba_width
    c_va = c_ka + moba_width
    c_za = c_va + moba_width
    c_qc = c_za + moba_width
    c_zc = c_qc + mem_width

    xb = x_ref[...].astype(BF16)
    pad = xpad_scr.shape[0] - tm
    k_w = convw_ref.shape[0]
    nb_lanes = gate_slots

    def x_cols(c0, width):
        return _dot(xb, win_ref[:, c0:c0 + width])

    def rotary_pair(p, qp, kp):
        rots, means = [], []
        for i in range(2):
            h = 2 * p + i
            ls = slice(i * moba_hd, (i + 1) * moba_hd)
            q_rot = qp[:, ls] * cosf + pltpu.roll(qp[:, ls], moba_hd // 2, 1) * sins
            k_rot = kp[:, ls] * cosf + pltpu.roll(kp[:, ls], moba_hd // 2, 1) * sins
            qa_ref[:, 2 * h * moba_hd:(2 * h + 1) * moba_hd] = (q_rot * (moba_hd ** -0.5)).astype(BF16)
            ka_ref[:, h * moba_hd:(h + 1) * moba_hd] = k_rot.astype(BF16)
            rots.append(q_rot)
            means.append(jnp.mean(k_rot, axis=0, keepdims=True))
        return rots, means

    def conv_gate_head(h, xm_h, zm_h):
        hs = slice(h * hd, (h + 1) * hd)
        xpad_scr[pad:pad + tm, hs] = xm_h
        conv = convb_ref[:, hs] + xm_h * convw_ref[k_w - 1:k_w, hs]
        for j in range(k_w - 1):
            conv = conv + xpad_scr[pl.ds(pad - (k_w - 1) + j, tm), hs] * convw_ref[j:j + 1, hs]
        xc_h = _silu(conv)
        sz_h = _silu(zm_h)
        add_ref[:, hs] = (skip_ref[:, hs] * xc_h * sz_h).astype(BF16)
        mul_ref[:, hs] = (normg_ref[:, hs] * sz_h).astype(BF16)
        return xc_h.astype(BF16), xm_h.astype(BF16)

    def blockdiag_head(h, xc_b, xm_b):
        hs = slice(h * hd, (h + 1) * hd)
        q_h = _dot(xc_b, wq_ref[h]).astype(BF16)
        k_f = _dot(xc_b, wk_ref[h])
        v_h = _dot(xm_b, wv_ref[h]).astype(BF16)
        qm_ref[:, hs] = q_h
        vm_ref[:, hs] = v_h
        kmt_ref[hs, :] = (jnp.transpose(k_f) * (hd ** -0.5)).astype(BF16)

    def gate_terms(h, xc_b, xm_b):
        hs = slice(h * hd, (h + 1) * hd)
        return _dot(xc_b, wfold_ref[0, hs, :]) + _dot(xm_b, wfold_ref[1, hs, :])

    def select_blocks(gate_t):
        n_rows = MOBA_HEADS * nb_lanes
        gate = gate_t[0:n_rows, :]
        row = lax.broadcasted_iota(jnp.int32, gate.shape, 0)
        n_r = row & (nb_lanes - 1)
        valid = n_r < t
        gate = jnp.where(valid, gate, -jnp.inf)
        cnt = jnp.zeros(gate.shape, jnp.int32)
        for r in range(1, nb_lanes):
            up = pltpu.roll(gate, r, 0)
            cnt = cnt + ((n_r >= r) & (up >= gate)).astype(jnp.int32)
            dn = pltpu.roll(gate, n_rows - r, 0)
            cnt = cnt + ((n_r < nb_lanes - r) & (dn > gate)).astype(jnp.int32)
        keep = (valid & (cnt < n_sel)) | (n_r == t)
        selb_t = jnp.where(keep, 0.0, MASK_VALUE)
        selb = jnp.transpose(jnp.concatenate(
            [selb_t, jnp.full((V7X_LANES - n_rows, tm), MASK_VALUE, F32)], axis=0))
        for h in range(MOBA_HEADS):
            shift = (V7X_LANES - h * nb_lanes) % V7X_LANES
            sel_h = selb if shift == 0 else pltpu.roll(selb, shift, 1)
            qa_ref[:, (2 * h + 1) * moba_hd:(2 * h + 2) * moba_hd] = sel_h.astype(BF16)

    pw = 2 * moba_hd
    qa0, ka0 = x_cols(c_qa, pw), x_cols(c_ka, pw)
    ang_t = invf_ref[...] * pos_ref[...]
    cos_t = jnp.cos(ang_t)
    sin_t = jnp.sin(ang_t)
    cosf = jnp.transpose(jnp.concatenate([cos_t, cos_t], axis=0))
    sins = jnp.transpose(jnp.concatenate([-sin_t, sin_t], axis=0))
    qa1, ka1 = x_cols(c_qa + pw, pw), x_cols(c_ka + pw, pw)
    rots0, means0 = rotary_pair(0, qa0, ka0)
    xm0, zm0 = x_cols(0, hd), x_cols(c_zm, hd)
    rots1, means1 = rotary_pair(1, qa1, ka1)
    q_rots, kmean_rows = rots0 + rots1, means0 + means1
    q_all = jnp.concatenate(q_rots, axis=1)
    km = kmean_scr[...]
    q_hi = q_all.astype(BF16)
    q_lo = (q_all - q_hi.astype(F32)).astype(BF16)
    k_hi = km.astype(BF16)
    k_lo = (km - k_hi.astype(F32)).astype(BF16)
    xz = {0: (xm0, zm0), 1: (x_cols(hd, hd), x_cols(c_zm + hd, hd))}
    gate = _dot_nt(k_hi, q_hi) + _dot_nt(k_hi, q_lo) + _dot_nt(k_lo, q_hi)
    pending = [("xz", h) for h in range(2, ml_heads)] + [("va", None), ("za", None)]
    qkv = []
    va = za = None
    for h in range(ml_heads):
        xc_b, xm_b = conv_gate_head(h, *xz[h])
        if pending:
            kind, arg = pending.pop(0)
            if kind == "xz":
                xz[arg] = (x_cols(arg * hd, hd), x_cols(c_zm + arg * hd, hd))
            elif kind == "va":
                va = x_cols(c_va, moba_width)
            else:
                za = x_cols(c_za, moba_width)
        blockdiag_head(h, xc_b, xm_b)
        qkv.append((xc_b, xm_b))
    if va is None:
        va = x_cols(c_va, moba_width)
    if za is None:
        za = x_cols(c_za, moba_width)
    g = jnp.zeros((tm, V7X_LANES), F32) + bg_ref[...]
    for h in range(ml_heads):
        g = g + gate_terms(h, *qkv[h])
    sza_ref[...] = _silu(za).astype(BF16)
    qc = x_cols(c_qc, mem_width)
    zc = x_cols(c_zc, mem_width)

    nrow = grow_ref.shape[0]
    gt = jnp.transpose(g)[0:V7X_SUBLANES, :]
    sub = lax.broadcasted_iota(jnp.int32, gt.shape, 0)
    tok_i = lax.broadcasted_iota(jnp.int32, gt.shape, 1)
    b = _lane_scan(jnp.where(sub >= ml_heads, _log_sigmoid(gt), 0.0), jnp.add, 0.0, tok_i)
    a = _lane_scan(jnp.where(sub >= ml_heads, pltpu.roll(gt, ml_heads, 0) - b, -jnp.inf),
                   jnp.maximum, -jnp.inf, tok_i)
    rows = jnp.concatenate([jnp.where(sub < ml_heads, gt, b), pltpu.roll(a, ml_heads, 0)], axis=0)
    grow_ref[...] = rows
    gcol_ref[...] = jnp.transpose(
        jnp.concatenate([rows, jnp.zeros((V7X_LANES - nrow, tm), F32)], axis=0))

    km_row = lax.broadcasted_iota(jnp.int32, kmean_scr.shape, 0)
    km_head = lax.broadcasted_iota(jnp.int32, kmean_scr.shape, 1) // moba_hd
    kmean_scr[...] = jnp.where(km_row == km_head * nb_lanes + t,
                               jnp.concatenate(kmean_rows, axis=1), kmean_scr[...])
    select_blocks(gate)
    vat_ref[...] = jnp.transpose(va).astype(BF16)
    qc_ref[...] = (qc * (mem_hd ** -0.5)).astype(BF16)
    szc_ref[...] = _silu(zc).astype(BF16)


def _mlstm_stage(qm_ref, kmt_ref, vm_ref, gcol_ref, grow_ref, add_ref, mul_ref, out_ref,
                 cn_scr, m_scr, *, heads):
    L = qm_ref.shape[0]
    hd = qm_ref.shape[1] // heads
    row_i = lax.broadcasted_iota(jnp.int32, (L, L), 0)
    col_i = lax.broadcasted_iota(jnp.int32, (L, L), 1)
    causal = row_i >= col_i
    gcol = gcol_ref[...]
    grow = grow_ref[...]
    ones = jnp.ones((L, V7X_LANES), BF16)
    head_slices = [slice(h * hd, (h + 1) * hd) for h in range(heads)]

    qk, inter, qn, mm_l, r_l, w_inter_l, v_aug, cn_prevs = [], [], [], [], [], [], [], []
    for h, hs in enumerate(head_slices):
        q = qm_ref[:, hs]
        cn_prevs.append(cn_scr[h])
        qk.append(_dot(q, kmt_ref[hs, :]))
        q_cn = _dot(q, cn_prevs[h].astype(BF16))
        inter.append(q_cn[:, :hd])
        qn.append(q_cn[:, hd:])
    for h, hs in enumerate(head_slices):
        kt = kmt_ref[hs, :]
        v_aug.append(jnp.concatenate([vm_ref[:, hs], ones], axis=1))
        i_row = grow[h:h + 1, :]
        b_row = grow[heads + h:heads + h + 1, :]
        a_rep = jnp.broadcast_to(gcol[:, 2 * heads + h:2 * heads + h + 1], (L, V7X_LANES))
        m_prev = m_scr[h][0:1, 0:1]
        cn_prev = cn_prevs[h]
        r_row = i_row - b_row
        mm = jnp.maximum(m_prev, a_rep)
        r_l.append(r_row)
        mm_l.append(mm)
        w_inter_l.append(jnp.exp(m_prev - mm))
        b_end = b_row[:, L - 1:L]
        log_w = b_end + r_row
        m_new = jnp.maximum(b_end + m_prev, jnp.max(log_w, axis=1, keepdims=True))
        decay = jnp.exp(b_end + m_prev - m_new)
        kw = (kt.astype(F32) * jnp.exp(log_w - m_new)).astype(BF16)
        cn_scr[h] = decay * cn_prev + _dot(kw, v_aug[h])
        m_scr[h] = jnp.broadcast_to(m_new, m_scr.shape[1:])

    wide = lambda rep, width: jnp.concatenate([rep] * (width // V7X_LANES), axis=1)
    s_qk = [(qk[h] * jnp.exp(jnp.where(causal, r_l[h] - wide(mm_l[h], L), -jnp.inf))).astype(BF16)
            for h in range(heads)]
    pv_rs = [_dot(s_qk[h], v_aug[h]) for h in range(heads)]

    for h, hs in enumerate(head_slices):
        b_rep = jnp.broadcast_to(gcol[:, heads + h:heads + h + 1], (L, V7X_LANES))
        num = wide(w_inter_l[h], hd) * inter[h] + pv_rs[h][:, :hd]
        den = w_inter_l[h] * qn[h] + pv_rs[h][:, hd:]
        rec = 1.0 / jnp.maximum(jnp.abs(den), jnp.exp(-(b_rep + mm_l[h])))
        hh = num * wide(rec, hd)
        mu = jnp.mean(hh, axis=1, keepdims=True)
        hc = hh - mu
        var = jnp.mean(hc * hc, axis=1, keepdims=True)
        hn = hc * lax.rsqrt(var + LN_EPS)
        out_ref[:, hs] = (hn * mul_ref[:, hs].astype(F32) + add_ref[:, hs].astype(F32)).astype(BF16)


def _moba_stage(q_ref, k_ref, vt_ref, sz_ref, out_ref, s_scr, m_scr, *, j):
    tq = q_ref.shape[0]
    bs = MOBA_BLOCK
    nblk = k_ref.shape[0] // bs
    aw = q_ref.shape[1] // MOBA_HEADS
    hd = aw // 2
    ones_rows = 2 * V7X_SUBLANES
    heads = [slice(h * aw, (h + 1) * aw) for h in range(MOBA_HEADS)]
    kv_heads = [slice(h * hd, (h + 1) * hd) for h in range(MOBA_HEADS)]
    tag_lane = lax.broadcasted_iota(jnp.int32, (bs, hd), 1)
    ones_blk = jnp.ones((ones_rows, bs), BF16)

    n_slots = s_scr.shape[0]

    def attend(own):
        causal_t = (lax.broadcasted_iota(jnp.int32, (bs, tq), 0)
                    <= lax.broadcasted_iota(jnp.int32, (bs, tq), 1))

        def score_pass(hi):
            slot = hi % n_slots
            mx = None
            for n in range(own + 1):
                k_aug = jnp.concatenate([k_ref[n * bs:(n + 1) * bs, kv_heads[hi]],
                                         (tag_lane == n).astype(BF16)], axis=1)
                s = _dot_nt(k_aug, q_ref[:, heads[hi]])
                if n == own:
                    s = jnp.where(causal_t, s, MASK_VALUE)
                s_scr[slot, n] = s
                sm = jnp.max(s, axis=0, keepdims=True)
                mx = sm if mx is None else jnp.maximum(mx, sm)
            m_scr[slot] = jnp.broadcast_to(mx, m_scr.shape[1:])

        def value_pass(hi):
            slot = hi % n_slots
            m = m_scr[slot][0:1, :]
            acc = None
            for n in range(own + 1):
                p = jnp.exp(s_scr[slot, n] - m).astype(BF16)
                vt_aug = jnp.concatenate([vt_ref[n, kv_heads[hi], :], ones_blk], axis=0)
                pv = _dot(vt_aug, p)
                acc = pv if acc is None else acc + pv
            o_t = acc[0:hd, :] * (1.0 / acc[hd:hd + 1, :])
            out_ref[:, hi * hd:(hi + 1) * hd] = (
                jnp.transpose(o_t) * sz_ref[:, hi * hd:(hi + 1) * hd].astype(F32)).astype(BF16)

        ahead = min(n_slots, MOBA_HEADS)
        for hi in range(ahead):
            score_pass(hi)
        for hi in range(MOBA_HEADS):
            value_pass(hi)
            if hi + ahead < MOBA_HEADS:
                score_pass(hi + ahead)

    for own in range(nblk):
        pl.when(j == own)(functools.partial(attend, own))


def _mix_stage(x_ref, om_ref, oa_ref, qc_ref, szc_ref, kv_scr, wo_ref, lng_ref, lnb_ref, y_ref):
    width = qc_ref.shape[1]
    hd = width // MEM_HEADS
    w_m = om_ref.shape[1]
    w_a = oa_ref.shape[1]
    head_slices = [slice(h * hd, (h + 1) * hd) for h in range(MEM_HEADS)]
    scores = [_dot_nt(qc_ref[:, hs], kv_scr[:, hs]) for hs in head_slices]
    mixed = _dot(oa_ref[...], wo_ref[w_m:w_m + w_a, :])
    probs = [jnp.exp(s - jnp.max(s, axis=1, keepdims=True)) for s in scores]
    outs = [_dot(p.astype(BF16), kv_scr[:, width + h * hd:width + (h + 1) * hd])
            for h, p in enumerate(probs)]
    oc = jnp.concatenate(
        [(outs[h] * (1.0 / jnp.sum(probs[h], axis=1, keepdims=True))
          * szc_ref[:, hs].astype(F32)).astype(BF16) for h, hs in enumerate(head_slices)], axis=1)
    mixed = mixed + _dot(oc, wo_ref[w_m + w_a:, :])
    mixed = mixed + _dot(om_ref[...], wo_ref[0:w_m, :])
    y = DEEPNORM_ALPHA * x_ref[...] + mixed
    mu = jnp.mean(y, axis=1, keepdims=True)
    yc = y - mu
    var = jnp.mean(yc * yc, axis=1, keepdims=True)
    y_ref[...] = yc * lax.rsqrt(var + LN_EPS) * lng_ref[...] + lnb_ref[...]


def _stage_weights_bf16(pairs, stage_s, sem):
    cr, cc = stage_s.shape[1:]
    chunks = [(src, dst, r0, c0) for src, dst in pairs
              for r0 in range(0, src.shape[0], cr) for c0 in range(0, src.shape[1], cc)]

    def chunk_copy(i):
        src, _, r0, c0 = chunks[i]
        return pltpu.make_async_copy(src.at[pl.ds(r0, cr), pl.ds(c0, cc)], stage_s.at[i % 2], sem.at[i % 2])

    chunk_copy(0).start()
    for i, (_, dst, r0, c0) in enumerate(chunks):
        if i + 1 < len(chunks):
            chunk_copy(i + 1).start()
        chunk_copy(i).wait()
        dst[r0:r0 + cr, c0:c0 + cc] = stage_s[i % 2].astype(BF16)


def _layer_kernel(x_ref, pos_ref, mem_ref, win_hbm, convw_ref, convb_ref, wq_ref, wk_ref, wv_ref,
                  wg_ref, bg_ref, normg_ref, skip_ref, invf_ref, wkv_hbm, wo_hbm, lng_ref, lnb_ref,
                  y_ref,
                  qm_s, kmt_s, vm_s, add_s, mul_s, gcol_s, grow_s, qa_s, sza_s, qc_s, szc_s,
                  ka_s, vat_s, om_s, oa_s, xpad_s, kmean_s, cn_s, m_s, score_s, rowmax_s, kv_s, wfold_s,
                  win_ref, wkv_ref, wo_ref, stage_s, stage_sem,
                  *, ml_width, ml_heads, moba_width, mem_width, n_sel, gate_slots):
    t = pl.program_id(1)
    tm = x_ref.shape[0]
    pad = xpad_s.shape[0] - tm
    hd = ml_width // ml_heads

    @pl.when((pl.program_id(0) == 0) & (t == 0))
    def _():
        _stage_weights_bf16([(win_hbm, win_ref), (wkv_hbm, wkv_ref), (wo_hbm, wo_ref)], stage_s, stage_sem)
        for h in range(ml_heads):
            rq, rk, rv = [slice(i * ml_width + h * hd, i * ml_width + (h + 1) * hd) for i in range(3)]
            wfold_s[0, rq, :] = (_dot(wq_ref[h], wg_ref[rq, :]) + _dot(wk_ref[h], wg_ref[rk, :])).astype(BF16)
            wfold_s[1, rq, :] = _dot(wv_ref[h], wg_ref[rv, :]).astype(BF16)

    @pl.when(t == 0)
    def _():
        xpad_s[0:pad, :] = jnp.zeros((pad, ml_width), F32)
        kmean_s[...] = jnp.zeros(kmean_s.shape, F32)
        cn_s[...] = jnp.zeros(cn_s.shape, F32)
        m_s[...] = jnp.zeros(m_s.shape, F32)
        kv_s[...] = _dot(mem_ref[...].astype(BF16), wkv_ref[...]).astype(BF16)

    @pl.when(t > 0)
    def _():
        xpad_s[0:pad, :] = xpad_s[tm:tm + pad, :]

    rows = pl.ds(pl.multiple_of(t * tm, tm), tm)
    _proj_stage(x_ref, pos_ref, win_ref, convw_ref, convb_ref, wq_ref, wk_ref, wv_ref,
                wfold_s, bg_ref, normg_ref, skip_ref, invf_ref,
                qm_s, kmt_s, vm_s, add_s, mul_s, gcol_s, grow_s,
                qa_s, ka_s.at[rows], vat_s.at[t], sza_s, qc_s, szc_s, xpad_s, kmean_s,
                t=t, ml_width=ml_width, ml_heads=ml_heads, moba_width=moba_width,
                mem_width=mem_width, n_sel=n_sel, gate_slots=gate_slots)
    _mlstm_stage(qm_s, kmt_s, vm_s, gcol_s, grow_s, add_s, mul_s, om_s, cn_s, m_s, heads=ml_heads)
    _moba_stage(qa_s, ka_s, vat_s, sza_s, oa_s, score_s, rowmax_s, j=t)
    _mix_stage(x_ref, om_s, oa_s, qc_s, szc_s, kv_s, wo_ref, lng_ref, lnb_ref, y_ref)


def _layer_call(x, pos, mem, win, convw, convb, wq, wk, wv, wg, bg, normg, skip, invf, wkv, wo, lng, lnb,
                *, ml_width, ml_heads, moba_width, mem_width):
    bsz, s, d = x.shape
    m_tok = mem.shape[1]
    tm = TOKEN_TILE
    nt = s // tm
    n_sel = min(MOBA_TOP_K, nt - 1)
    gate_slots = max(V7X_SUBLANES, pl.next_power_of_2(nt))
    assert MOBA_HEADS * gate_slots <= V7X_LANES and s % tm == 0 and ml_heads >= 2
    hd = ml_width // ml_heads
    tok = lambda w: pl.BlockSpec((None, tm, w), lambda b, t: (b, t, 0))

    def resident(a):
        nd = a.ndim
        return pl.BlockSpec(a.shape, lambda b, t, _nd=nd: (0,) * _nd, pipeline_mode=pl.Buffered(1))

    weights = (win, convw, convb, wq, wk, wv, wg, bg, normg, skip, invf, wkv, wo, lng, lnb)
    staged = (win, wkv, wo)
    assert all(a.shape[0] % STAGE_ROWS == 0 and a.shape[1] % STAGE_COLS == 0 for a in staged)
    in_specs = ([tok(d),
                 pl.BlockSpec((None, None, 1, tm), lambda b, t: (b, t, 0, 0)),
                 pl.BlockSpec((None, m_tok, d), lambda b, t: (b, 0, 0))]
                + [pl.BlockSpec(memory_space=pl.ANY) if any(a is w for w in staged) else resident(a)
                   for a in weights])
    vmem = pltpu.VMEM
    score_heads = MOBA_HEADS
    scratch = [
        vmem((tm, ml_width), BF16),
        vmem((ml_width, tm), BF16),
        vmem((tm, ml_width), BF16),
        vmem((tm, ml_width), BF16),
        vmem((tm, ml_width), BF16),
        vmem((tm, V7X_LANES), F32),
        vmem((GATE_ROWS, tm), F32),
        vmem((tm, 2 * moba_width), BF16),
        vmem((tm, moba_width), BF16),
        vmem((tm, mem_width), BF16),
        vmem((tm, mem_width), BF16),
        vmem((s, moba_width), BF16),
        vmem((nt, moba_width, tm), BF16),
        vmem((tm, ml_width), BF16),
        vmem((tm, moba_width), BF16),
        vmem((tm + V7X_SUBLANES, ml_width), F32),
        vmem((V7X_LANES, moba_width), F32),
        vmem((ml_heads, hd, hd + V7X_LANES), F32),
        vmem((ml_heads, V7X_SUBLANES, V7X_LANES), F32),
        vmem((score_heads, nt, tm, MOBA_BLOCK), F32),
        vmem((score_heads, V7X_SUBLANES, tm), F32),
        vmem((m_tok, 2 * mem_width), BF16),
        vmem((2, ml_width, V7X_LANES), BF16),
        vmem(win.shape, BF16),
        vmem(wkv.shape, BF16),
        vmem(wo.shape, BF16),
        vmem((2, STAGE_ROWS, STAGE_COLS), F32),
        pltpu.SemaphoreType.DMA((2,)),
    ]
    kern = functools.partial(_layer_kernel, ml_width=ml_width, ml_heads=ml_heads, moba_width=moba_width,
                             mem_width=mem_width, n_sel=n_sel, gate_slots=gate_slots)
    return pl.pallas_call(
        kern,
        grid=(bsz, nt),
        in_specs=in_specs,
        out_specs=tok(d),
        out_shape=jax.ShapeDtypeStruct((bsz, s, d), x.dtype),
        scratch_shapes=scratch,
        compiler_params=pltpu.CompilerParams(dimension_semantics=("arbitrary", "arbitrary"),
                                             vmem_limit_bytes=V7X_VMEM_LIMIT_BYTES),
        name="layer",
    )(x, pos.reshape(bsz, nt, 1, tm), mem, *weights)


def _diag_tiles(w, heads):
    groups, blk, _ = w.shape
    hd = groups // heads * blk
    rows = w.reshape(heads, hd, blk)
    idx = jnp.arange(hd)
    same_group = (idx[:, None] // blk) == (idx[None, :] // blk)
    return jnp.where(same_group, jnp.tile(rows, (1, 1, hd // blk)), 0.0)


def kernel(x, mem, positions, w_in, mlstm_conv_w, mlstm_conv_b, mlstm_wq, mlstm_wk, mlstm_wv, mlstm_w_gates, mlstm_b_gates, mlstm_norm_g, mlstm_skip, w_mem_kv, w_out, ln_g, ln_b):
    bsz, s, d = x.shape
    ml_width = mlstm_conv_w.shape[1]
    ml_heads = mlstm_b_gates.shape[0] // 2
    mem_width = w_mem_kv.shape[1] // 2
    moba_width = (w_in.shape[1] - 2 * ml_width - 2 * mem_width) // 4
    moba_hd = moba_width // MOBA_HEADS
    assert moba_hd == V7X_LANES and mem_width // MEM_HEADS == V7X_LANES and MOBA_HEADS == 4
    assert s % TOKEN_TILE == 0

    row = lambda a: a.reshape(1, -1).astype(F32)
    wq = _diag_tiles(mlstm_wq, ml_heads).astype(BF16)
    wk = _diag_tiles(mlstm_wk, ml_heads).astype(BF16)
    wv = _diag_tiles(mlstm_wv, ml_heads).astype(BF16)
    n_gates = mlstm_w_gates.shape[1]
    wg = jnp.pad(mlstm_w_gates, ((0, 0), (0, V7X_LANES - n_gates))).astype(BF16)
    bg = jnp.pad(mlstm_b_gates, (0, V7X_LANES - n_gates)).reshape(1, -1).astype(F32)
    half = moba_hd // 2
    inv_freq = ROPE_THETA ** (-jnp.arange(half, dtype=F32) * 2.0 / moba_hd)
    invf = inv_freq.reshape(-1, 1)
    pos = positions.astype(F32)

    return _layer_call(
        x, pos, mem, w_in.astype(F32), mlstm_conv_w.astype(F32), row(mlstm_conv_b), wq, wk, wv, wg, bg,
        row(mlstm_norm_g), row(mlstm_skip), invf, w_mem_kv.astype(F32), w_out.astype(F32),
        row(ln_g), row(ln_b),
        ml_width=ml_width, ml_heads=ml_heads, moba_width=moba_width, mem_width=mem_width)
```

```python
import functools

import jax
import jax.numpy as jnp
from jax import lax
from jax.experimental import pallas as pl
from jax.experimental.pallas import tpu as pltpu

MOBA_HEADS = 4
MOBA_BLOCK = 256
MOBA_TOP_K = 3
MEM_HEADS = 4
ROPE_THETA = 10000.0
DEPTH = 1
DEEPNORM_ALPHA = (2 * DEPTH) ** 0.25
LN_EPS = 1e-5

V7X_LANES = 128
V7X_SUBLANES = 8
V7X_VMEM_BYTES = 64 * 1024 * 1024
V7X_VMEM_LIMIT_BYTES = V7X_VMEM_BYTES - 8 * 1024 * 1024

TOKEN_TILE = MOBA_BLOCK
MASK_VALUE = -1e30
GATE_ROWS = 2 * V7X_SUBLANES
STAGE_ROWS, STAGE_COLS = 512, 1024

F32 = jnp.float32
BF16 = jnp.bfloat16


def _dot(a, b):
    return jnp.dot(a, b, preferred_element_type=F32)


def _dot_nt(a, b):
    return lax.dot_general(a, b, (((1,), (1,)), ((), ())), preferred_element_type=F32)


def _lane_scan(v, op, identity, lane_idx):
    shift = 1
    while shift < v.shape[-1]:
        v = op(v, jnp.where(lane_idx >= shift, pltpu.roll(v, shift, v.ndim - 1), identity))
        shift *= 2
    return v


def _silu(v):
    h = 0.5 * v
    return h + h * jnp.tanh(h)


def _log_sigmoid(v):
    return jnp.minimum(v, 0.0) - jnp.log1p(jnp.exp(-jnp.abs(v)))


def _proj_stage(x_ref, pos_ref, win_ref, convw_ref, convb_ref, wq_ref, wk_ref, wv_ref,
                wfold_ref, bg_ref, normg_ref, skip_ref, invf_ref,
                qm_ref, kmt_ref, vm_ref, add_ref, mul_ref, gcol_ref, grow_ref,
                qa_ref, ka_ref, vat_ref, sza_ref, qc_ref, szc_ref,
                xpad_scr, kmean_scr,
                *, t, ml_width, ml_heads, moba_width, mem_width, n_sel, gate_slots):
    tm = x_ref.shape[0]
    hd = ml_width // ml_heads
    moba_hd = moba_width // MOBA_HEADS
    mem_hd = mem_width // MEM_HEADS
    c_zm = ml_width
    c_qa = 2 * ml_width
    c_ka = c_qa + moba_width
    c_va = c_ka + moba_width
    c_za = c_va + moba_width
    c_qc = c_za + moba_width
    c_zc = c_qc + mem_width

    xb = x_ref[...].astype(BF16)
    pad = xpad_scr.shape[0] - tm
    k_w = convw_ref.shape[0]
    nb_lanes = gate_slots

    def x_cols(c0, width):
        return _dot(xb, win_ref[:, c0:c0 + width])

    def rotary_pair(p, qp, kp):
        rots, means = [], []
        for i in range(2):
            h = 2 * p + i
            ls = slice(i * moba_hd, (i + 1) * moba_hd)
            q_rot = qp[:, ls] * cosf + pltpu.roll(qp[:, ls], moba_hd // 2, 1) * sins
            k_rot = kp[:, ls] * cosf + pltpu.roll(kp[:, ls], moba_hd // 2, 1) * sins
            qa_ref[:, 2 * h * moba_hd:(2 * h + 1) * moba_hd] = (q_rot * (moba_hd ** -0.5)).astype(BF16)
            ka_ref[:, h * moba_hd:(h + 1) * moba_hd] = k_rot.astype(BF16)
            rots.append(q_rot)
            means.append(jnp.mean(k_rot, axis=0, keepdims=True))
        return rots, means

    def conv_gate_head(h, xm_h, zm_h):
        hs = slice(h * hd, (h + 1) * hd)
        xpad_scr[pad:pad + tm, hs] = xm_h
        conv = convb_ref[:, hs] + xm_h * convw_ref[k_w - 1:k_w, hs]
        for j in range(k_w - 1):
            conv = conv + xpad_scr[pl.ds(pad - (k_w - 1) + j, tm), hs] * convw_ref[j:j + 1, hs]
        xc_h = _silu(conv)
        sz_h = _silu(zm_h)
        add_ref[:, hs] = (skip_ref[:, hs] * xc_h * sz_h).astype(BF16)
        mul_ref[:, hs] = (normg_ref[:, hs] * sz_h).astype(BF16)
        return xc_h.astype(BF16), xm_h.astype(BF16)

    def blockdiag_head(h, xc_b, xm_b):
        hs = slice(h * hd, (h + 1) * hd)
        q_h = _dot(xc_b, wq_ref[h]).astype(BF16)
        k_f = _dot(xc_b, wk_ref[h])
        v_h = _dot(xm_b, wv_ref[h]).astype(BF16)
        qm_ref[:, hs] = q_h
        vm_ref[:, hs] = v_h
        kmt_ref[hs, :] = (jnp.transpose(k_f) * (hd ** -0.5)).astype(BF16)

    def gate_terms(h, xc_b, xm_b):
        hs = slice(h * hd, (h + 1) * hd)
        return _dot(xc_b, wfold_ref[0, hs, :]) + _dot(xm_b, wfold_ref[1, hs, :])

    def select_blocks(gate_t):
        n_rows = MOBA_HEADS * nb_lanes
        gate = gate_t[0:n_rows, :]
        row = lax.broadcasted_iota(jnp.int32, gate.shape, 0)
        n_r = row & (nb_lanes - 1)
        valid = n_r < t
        gate = jnp.where(valid, gate, -jnp.inf)
        cnt = jnp.zeros(gate.shape, jnp.int32)
        for r in range(1, nb_lanes):
            up = pltpu.roll(gate, r, 0)
            cnt = cnt + ((n_r >= r) & (up >= gate)).astype(jnp.int32)
            dn = pltpu.roll(gate, n_rows - r, 0)
            cnt = cnt + ((n_r < nb_lanes - r) & (dn > gate)).astype(jnp.int32)
        keep = (valid & (cnt < n_sel)) | (n_r == t)
        selb_t = jnp.where(keep, 0.0, MASK_VALUE)
        selb = jnp.transpose(jnp.concatenate(
            [selb_t, jnp.full((V7X_LANES - n_rows, tm), MASK_VALUE, F32)], axis=0))
        for h in range(MOBA_HEADS):
            shift = (V7X_LANES - h * nb_lanes) % V7X_LANES
            sel_h = selb if shift == 0 else pltpu.roll(selb, shift, 1)
            qa_ref[:, (2 * h + 1) * moba_hd:(2 * h + 2) * moba_hd] = sel_h.astype(BF16)

    pw = 2 * moba_hd
    qa0, ka0 = x_cols(c_qa, pw), x_cols(c_ka, pw)
    ang_t = invf_ref[...] * pos_ref[...]
    cos_t = jnp.cos(ang_t)
    sin_t = jnp.sin(ang_t)
    cosf = jnp.transpose(jnp.concatenate([cos_t, cos_t], axis=0))
    sins = jnp.transpose(jnp.concatenate([-sin_t, sin_t], axis=0))
    qa1, ka1 = x_cols(c_qa + pw, pw), x_cols(c_ka + pw, pw)
    rots0, means0 = rotary_pair(0, qa0, ka0)
    xm0, zm0 = x_cols(0, hd), x_cols(c_zm, hd)
    rots1, means1 = rotary_pair(1, qa1, ka1)
    q_rots, kmean_rows = rots0 + rots1, means0 + means1
    q_all = jnp.concatenate(q_rots, axis=1)
    km = kmean_scr[...]
    q_hi = q_all.astype(BF16)
    q_lo = (q_all - q_hi.astype(F32)).astype(BF16)
    k_hi = km.astype(BF16)
    k_lo = (km - k_hi.astype(F32)).astype(BF16)
    xz = {0: (xm0, zm0), 1: (x_cols(hd, hd), x_cols(c_zm + hd, hd))}
    gate = _dot_nt(k_hi, q_hi) + _dot_nt(k_hi, q_lo) + _dot_nt(k_lo, q_hi)
    pending = [("xz", h) for h in range(2, ml_heads)] + [("va", None), ("za", None)]
    qkv = []
    va = za = None
    for h in range(ml_heads):
        xc_b, xm_b = conv_gate_head(h, *xz[h])
        if pending:
            kind, arg = pending.pop(0)
            if kind == "xz":
                xz[arg] = (x_cols(arg * hd, hd), x_cols(c_zm + arg * hd, hd))
            elif kind == "va":
                va = x_cols(c_va, moba_width)
            else:
                za = x_cols(c_za, moba_width)
        blockdiag_head(h, xc_b, xm_b)
        qkv.append((xc_b, xm_b))
    if va is None:
        va = x_cols(c_va, moba_width)
    if za is None:
        za = x_cols(c_za, moba_width)
    g = jnp.zeros((tm, V7X_LANES), F32) + bg_ref[...]
    for h in range(ml_heads):
        g = g + gate_terms(h, *qkv[h])
    sza_ref[...] = _silu(za).astype(BF16)
    qc = x_cols(c_qc, mem_width)
    zc = x_cols(c_zc, mem_width)

    nrow = grow_ref.shape[0]
    gt = jnp.transpose(g)[0:V7X_SUBLANES, :]
    sub = lax.broadcasted_iota(jnp.int32, gt.shape, 0)
    tok_i = lax.broadcasted_iota(jnp.int32, gt.shape, 1)
    b = _lane_scan(jnp.where(sub >= ml_heads, _log_sigmoid(gt), 0.0), jnp.add, 0.0, tok_i)
    a = _lane_scan(jnp.where(sub >= ml_heads, pltpu.roll(gt, ml_heads, 0) - b, -jnp.inf),
                   jnp.maximum, -jnp.inf, tok_i)
    rows = jnp.concatenate([jnp.where(sub < ml_heads, gt, b), pltpu.roll(a, ml_heads, 0)], axis=0)
    grow_ref[...] = rows
    gcol_ref[...] = jnp.transpose(
        jnp.concatenate([rows, jnp.zeros((V7X_LANES - nrow, tm), F32)], axis=0))

    km_row = lax.broadcasted_iota(jnp.int32, kmean_scr.shape, 0)
    km_head = lax.broadcasted_iota(jnp.int32, kmean_scr.shape, 1) // moba_hd
    kmean_scr[...] = jnp.where(km_row == km_head * nb_lanes + t,
                               jnp.concatenate(kmean_rows, axis=1), kmean_scr[...])
    select_blocks(gate)
    vat_ref[...] = jnp.transpose(va).astype(BF16)
    qc_ref[...] = (qc * (mem_hd ** -0.5)).astype(BF16)
    szc_ref[...] = _silu(zc).astype(BF16)


def _mlstm_stage(qm_ref, kmt_ref, vm_ref, gcol_ref, grow_ref, add_ref, mul_ref, out_ref,
                 cn_scr, m_scr, *, heads):
    L = qm_ref.shape[0]
    hd = qm_ref.shape[1] // heads
    row_i = lax.broadcasted_iota(jnp.int32, (L, L), 0)
    col_i = lax.broadcasted_iota(jnp.int32, (L, L), 1)
    causal = row_i >= col_i
    gcol = gcol_ref[...]
    grow = grow_ref[...]
    ones = jnp.ones((L, V7X_LANES), BF16)
    head_slices = [slice(h * hd, (h + 1) * hd) for h in range(heads)]

    qk, inter, qn, mm_l, r_l, w_inter_l, v_aug, cn_prevs = [], [], [], [], [], [], [], []
    for h, hs in enumerate(head_slices):
        q = qm_ref[:, hs]
        cn_prevs.append(cn_scr[h])
        qk.append(_dot(q, kmt_ref[hs, :]))
        q_cn = _dot(q, cn_prevs[h].astype(BF16))
        inter.append(q_cn[:, :hd])
        qn.append(q_cn[:, hd:])
    for h, hs in enumerate(head_slices):
        kt = kmt_ref[hs, :]
        v_aug.append(jnp.concatenate([vm_ref[:, hs], ones], axis=1))
        i_row = grow[h:h + 1, :]
        b_row = grow[heads + h:heads + h + 1, :]
        a_rep = jnp.broadcast_to(gcol[:, 2 * heads + h:2 * heads + h + 1], (L, V7X_LANES))
        m_prev = m_scr[h][0:1, 0:1]
        cn_prev = cn_prevs[h]
        r_row = i_row - b_row
        mm = jnp.maximum(m_prev, a_rep)
        r_l.append(r_row)
        mm_l.append(mm)
        w_inter_l.append(jnp.exp(m_prev - mm))
        b_end = b_row[:, L - 1:L]
        log_w = b_end + r_row
        m_new = jnp.maximum(b_end + m_prev, jnp.max(log_w, axis=1, keepdims=True))
        decay = jnp.exp(b_end + m_prev - m_new)
        kw = (kt.astype(F32) * jnp.exp(log_w - m_new)).astype(BF16)
        cn_scr[h] = decay * cn_prev + _dot(kw, v_aug[h])
        m_scr[h] = jnp.broadcast_to(m_new, m_scr.shape[1:])

    wide = lambda rep, width: jnp.concatenate([rep] * (width // V7X_LANES), axis=1)
    s_qk = [(qk[h] * jnp.exp(jnp.where(causal, r_l[h] - wide(mm_l[h], L), -jnp.inf))).astype(BF16)
            for h in range(heads)]
    pv_rs = [_dot(s_qk[h], v_aug[h]) for h in range(heads)]

    for h, hs in enumerate(head_slices):
        b_rep = jnp.broadcast_to(gcol[:, heads + h:heads + h + 1], (L, V7X_LANES))
        num = wide(w_inter_l[h], hd) * inter[h] + pv_rs[h][:, :hd]
        den = w_inter_l[h] * qn[h] + pv_rs[h][:, hd:]
        rec = 1.0 / jnp.maximum(jnp.abs(den), jnp.exp(-(b_rep + mm_l[h])))
        hh = num * wide(rec, hd)
        mu = jnp.mean(hh, axis=1, keepdims=True)
        hc = hh - mu
        var = jnp.mean(hc * hc, axis=1, keepdims=True)
        hn = hc * lax.rsqrt(var + LN_EPS)
        out_ref[:, hs] = (hn * mul_ref[:, hs].astype(F32) + add_ref[:, hs].astype(F32)).astype(BF16)


def _moba_stage(q_ref, k_ref, vt_ref, sz_ref, out_ref, s_scr, m_scr, *, j):
    tq = q_ref.shape[0]
    bs = MOBA_BLOCK
    nblk = k_ref.shape[0] // bs
    aw = q_ref.shape[1] // MOBA_HEADS
    hd = aw // 2
    ones_rows = 2 * V7X_SUBLANES
    heads = [slice(h * aw, (h + 1) * aw) for h in range(MOBA_HEADS)]
    kv_heads = [slice(h * hd, (h + 1) * hd) for h in range(MOBA_HEADS)]
    tag_lane = lax.broadcasted_iota(jnp.int32, (bs, hd), 1)
    ones_blk = jnp.ones((ones_rows, bs), BF16)

    n_slots = s_scr.shape[0]

    def attend(own):
        causal_t = (lax.broadcasted_iota(jnp.int32, (bs, tq), 0)
                    <= lax.broadcasted_iota(jnp.int32, (bs, tq), 1))

        def score_pass(hi):
            slot = hi % n_slots
            mx = None
            for n in range(own + 1):
                k_aug = jnp.concatenate([k_ref[n * bs:(n + 1) * bs, kv_heads[hi]],
                                         (tag_lane == n).astype(BF16)], axis=1)
                s = _dot_nt(k_aug, q_ref[:, heads[hi]])
                if n == own:
                    s = jnp.where(causal_t, s, MASK_VALUE)
                s_scr[slot, n] = s
                sm = jnp.max(s, axis=0, keepdims=True)
                mx = sm if mx is None else jnp.maximum(mx, sm)
            m_scr[slot] = jnp.broadcast_to(mx, m_scr.shape[1:])

        def value_pass(hi):
            slot = hi % n_slots
            m = m_scr[slot][0:1, :]
            acc = None
            for n in range(own + 1):
                p = jnp.exp(s_scr[slot, n] - m).astype(BF16)
                vt_aug = jnp.concatenate([vt_ref[n, kv_heads[hi], :], ones_blk], axis=0)
                pv = _dot(vt_aug, p)
                acc = pv if acc is None else acc + pv
            o_t = acc[0:hd, :] * (1.0 / acc[hd:hd + 1, :])
            out_ref[:, hi * hd:(hi + 1) * hd] = (
                jnp.transpose(o_t) * sz_ref[:, hi * hd:(hi + 1) * hd].astype(F32)).astype(BF16)

        ahead = min(n_slots, MOBA_HEADS)
        for hi in range(ahead):
            score_pass(hi)
        for hi in range(MOBA_HEADS):
            value_pass(hi)
            if hi + ahead < MOBA_HEADS:
                score_pass(hi + ahead)

    for own in range(nblk):
        pl.when(j == own)(functools.partial(attend, own))


def _mix_stage(x_ref, om_ref, oa_ref, qc_ref, szc_ref, kv_scr, wo_ref, lng_ref, lnb_ref, y_ref):
    width = qc_ref.shape[1]
    hd = width // MEM_HEADS
    w_m = om_ref.shape[1]
    w_a = oa_ref.shape[1]
    head_slices = [slice(h * hd, (h + 1) * hd) for h in range(MEM_HEADS)]
    scores = [_dot_nt(qc_ref[:, hs], kv_scr[:, hs]) for hs in head_slices]
    mixed = _dot(oa_ref[...], wo_ref[w_m:w_m + w_a, :])
    probs = [jnp.exp(s - jnp.max(s, axis=1, keepdims=True)) for s in scores]
    outs = [_dot(p.astype(BF16), kv_scr[:, width + h * hd:width + (h + 1) * hd])
            for h, p in enumerate(probs)]
    oc = jnp.concatenate(
        [(outs[h] * (1.0 / jnp.sum(probs[h], axis=1, keepdims=True))
          * szc_ref[:, hs].astype(F32)).astype(BF16) for h, hs in enumerate(head_slices)], axis=1)
    mixed = mixed + _dot(oc, wo_ref[w_m + w_a:, :])
    mixed = mixed + _dot(om_ref[...], wo_ref[0:w_m, :])
    y = DEEPNORM_ALPHA * x_ref[...] + mixed
    mu = jnp.mean(y, axis=1, keepdims=True)
    yc = y - mu
    var = jnp.mean(yc * yc, axis=1, keepdims=True)
    y_ref[...] = yc * lax.rsqrt(var + LN_EPS) * lng_ref[...] + lnb_ref[...]


def _stage_weights_bf16(pairs, stage_s, sem):
    cr, cc = stage_s.shape[1:]
    chunks = [(src, dst, r0, c0) for src, dst in pairs
              for r0 in range(0, src.shape[0], cr) for c0 in range(0, src.shape[1], cc)]

    def chunk_copy(i):
        src, _, r0, c0 = chunks[i]
        return pltpu.make_async_copy(src.at[pl.ds(r0, cr), pl.ds(c0, cc)], stage_s.at[i % 2], sem.at[i % 2])

    chunk_copy(0).start()
    for i, (_, dst, r0, c0) in enumerate(chunks):
        if i + 1 < len(chunks):
            chunk_copy(i + 1).start()
        chunk_copy(i).wait()
        dst[r0:r0 + cr, c0:c0 + cc] = stage_s[i % 2].astype(BF16)


def _layer_kernel(x_ref, pos_ref, mem_ref, win_hbm, convw_ref, convb_ref, wq_ref, wk_ref, wv_ref,
                  wg_ref, bg_ref, normg_ref, skip_ref, invf_ref, wkv_hbm, wo_hbm, lng_ref, lnb_ref,
                  y_ref,
                  qm_s, kmt_s, vm_s, add_s, mul_s, gcol_s, grow_s, qa_s, sza_s, qc_s, szc_s,
                  ka_s, vat_s, om_s, oa_s, xpad_s, kmean_s, cn_s, m_s, score_s, rowmax_s, kv_s, wfold_s,
                  win_ref, wkv_ref, wo_ref, stage_s, stage_sem,
                  *, ml_width, ml_heads, moba_width, mem_width, n_sel, gate_slots):
    t = pl.program_id(1)
    tm = x_ref.shape[0]
    pad = xpad_s.shape[0] - tm
    hd = ml_width // ml_heads

    @pl.when((pl.program_id(0) == 0) & (t == 0))
    def _():
        _stage_weights_bf16([(win_hbm, win_ref), (wkv_hbm, wkv_ref), (wo_hbm, wo_ref)], stage_s, stage_sem)
        for h in range(ml_heads):
            rq, rk, rv = [slice(i * ml_width + h * hd, i * ml_width + (h + 1) * hd) for i in range(3)]
            wfold_s[0, rq, :] = (_dot(wq_ref[h], wg_ref[rq, :]) + _dot(wk_ref[h], wg_ref[rk, :])).astype(BF16)
            wfold_s[1, rq, :] = _dot(wv_ref[h], wg_ref[rv, :]).astype(BF16)

    @pl.when(t == 0)
    def _():
        xpad_s[0:pad, :] = jnp.zeros((pad, ml_width), F32)
        kmean_s[...] = jnp.zeros(kmean_s.shape, F32)
        cn_s[...] = jnp.zeros(cn_s.shape, F32)
        m_s[...] = jnp.zeros(m_s.shape, F32)
        kv_s[...] = _dot(mem_ref[...].astype(BF16), wkv_ref[...]).astype(BF16)

    @pl.when(t > 0)
    def _():
        xpad_s[0:pad, :] = xpad_s[tm:tm + pad, :]

    rows = pl.ds(pl.multiple_of(t * tm, tm), tm)
    _proj_stage(x_ref, pos_ref, win_ref, convw_ref, convb_ref, wq_ref, wk_ref, wv_ref,
                wfold_s, bg_ref, normg_ref, skip_ref, invf_ref,
                qm_s, kmt_s, vm_s, add_s, mul_s, gcol_s, grow_s,
                qa_s, ka_s.at[rows], vat_s.at[t], sza_s, qc_s, szc_s, xpad_s, kmean_s,
                t=t, ml_width=ml_width, ml_heads=ml_heads, moba_width=moba_width,
                mem_width=mem_width, n_sel=n_sel, gate_slots=gate_slots)
    _mlstm_stage(qm_s, kmt_s, vm_s, gcol_s, grow_s, add_s, mul_s, om_s, cn_s, m_s, heads=ml_heads)
    _moba_stage(qa_s, ka_s, vat_s, sza_s, oa_s, score_s, rowmax_s, j=t)
    _mix_stage(x_ref, om_s, oa_s, qc_s, szc_s, kv_s, wo_ref, lng_ref, lnb_ref, y_ref)


def _layer_call(x, pos, mem, win, convw, convb, wq, wk, wv, wg, bg, normg, skip, invf, wkv, wo, lng, lnb,
                *, ml_width, ml_heads, moba_width, mem_width):
    bsz, s, d = x.shape
    m_tok = mem.shape[1]
    tm = TOKEN_TILE
    nt = s // tm
    n_sel = min(MOBA_TOP_K, nt - 1)
    gate_slots = max(V7X_SUBLANES, pl.next_power_of_2(nt))
    assert MOBA_HEADS * gate_slots <= V7X_LANES and s % tm == 0 and ml_heads >= 2
    hd = ml_width // ml_heads
    tok = lambda w: pl.BlockSpec((None, tm, w), lambda b, t: (b, t, 0))

    def resident(a):
        nd = a.ndim
        return pl.BlockSpec(a.shape, lambda b, t, _nd=nd: (0,) * _nd, pipeline_mode=pl.Buffered(1))

    weights = (win, convw, convb, wq, wk, wv, wg, bg, normg, skip, invf, wkv, wo, lng, lnb)
    staged = (win, wkv, wo)
    assert all(a.shape[0] % STAGE_ROWS == 0 and a.shape[1] % STAGE_COLS == 0 for a in staged)
    in_specs = ([tok(d),
                 pl.BlockSpec((None, None, 1, tm), lambda b, t: (b, t, 0, 0)),
                 pl.BlockSpec((None, m_tok, d), lambda b, t: (b, 0, 0))]
                + [pl.BlockSpec(memory_space=pl.ANY) if any(a is w for w in staged) else resident(a)
                   for a in weights])
    vmem = pltpu.VMEM
    score_heads = MOBA_HEADS
    scratch = [
        vmem((tm, ml_width), BF16),
        vmem((ml_width, tm), BF16),
        vmem((tm, ml_width), BF16),
        vmem((tm, ml_width), BF16),
        vmem((tm, ml_width), BF16),
        vmem((tm, V7X_LANES), F32),
        vmem((GATE_ROWS, tm), F32),
        vmem((tm, 2 * moba_width), BF16),
        vmem((tm, moba_width), BF16),
        vmem((tm, mem_width), BF16),
        vmem((tm, mem_width), BF16),
        vmem((s, moba_width), BF16),
        vmem((nt, moba_width, tm), BF16),
        vmem((tm, ml_width), BF16),
        vmem((tm, moba_width), BF16),
        vmem((tm + V7X_SUBLANES, ml_width), F32),
        vmem((V7X_LANES, moba_width), F32),
        vmem((ml_heads, hd, hd + V7X_LANES), F32),
        vmem((ml_heads, V7X_SUBLANES, V7X_LANES), F32),
        vmem((score_heads, nt, tm, MOBA_BLOCK), F32),
        vmem((score_heads, V7X_SUBLANES, tm), F32),
        vmem((m_tok, 2 * mem_width), BF16),
        vmem((2, ml_width, V7X_LANES), BF16),
        vmem(win.shape, BF16),
        vmem(wkv.shape, BF16),
        vmem(wo.shape, BF16),
        vmem((2, STAGE_ROWS, STAGE_COLS), F32),
        pltpu.SemaphoreType.DMA((2,)),
    ]
    kern = functools.partial(_layer_kernel, ml_width=ml_width, ml_heads=ml_heads, moba_width=moba_width,
                             mem_width=mem_width, n_sel=n_sel, gate_slots=gate_slots)
    return pl.pallas_call(
        kern,
        grid=(bsz, nt),
        in_specs=in_specs,
        out_specs=tok(d),
        out_shape=jax.ShapeDtypeStruct((bsz, s, d), x.dtype),
        scratch_shapes=scratch,
        compiler_params=pltpu.CompilerParams(dimension_semantics=("arbitrary", "arbitrary"),
                                             vmem_limit_bytes=V7X_VMEM_LIMIT_BYTES),
        name="layer",
    )(x, pos.reshape(bsz, nt, 1, tm), mem, *weights)


def _diag_tiles(w, heads):
    groups, blk, _ = w.shape
    hd = groups // heads * blk
    rows = w.reshape(heads, hd, blk)
    idx = jnp.arange(hd)
    same_group = (idx[:, None] // blk) == (idx[None, :] // blk)
    return jnp.where(same_group, jnp.tile(rows, (1, 1, hd // blk)), 0.0)


def kernel(x, mem, positions, w_in, mlstm_conv_w, mlstm_conv_b, mlstm_wq, mlstm_wk, mlstm_wv, mlstm_w_gates, mlstm_b_gates, mlstm_norm_g, mlstm_skip, w_mem_kv, w_out, ln_g, ln_b):
    bsz, s, d = x.shape
    ml_width = mlstm_conv_w.shape[1]
    ml_heads = mlstm_b_gates.shape[0] // 2
    mem_width = w_mem_kv.shape[1] // 2
    moba_width = (w_in.shape[1] - 2 * ml_width - 2 * mem_width) // 4
    moba_hd = moba_width // MOBA_HEADS
    assert moba_hd == V7X_LANES and mem_width // MEM_HEADS == V7X_LANES and MOBA_HEADS == 4
    assert s % TOKEN_TILE == 0

    row = lambda a: a.reshape(1, -1).astype(F32)
    wq = _diag_tiles(mlstm_wq, ml_heads).astype(BF16)
    wk = _diag_tiles(mlstm_wk, ml_heads).astype(BF16)
    wv = _diag_tiles(mlstm_wv, ml_heads).astype(BF16)
    n_gates = mlstm_w_gates.shape[1]
    wg = jnp.pad(mlstm_w_gates, ((0, 0), (0, V7X_LANES - n_gates))).astype(BF16)
    bg = jnp.pad(mlstm_b_gates, (0, V7X_LANES - n_gates)).reshape(1, -1).astype(F32)
    half = moba_hd // 2
    inv_freq = ROPE_THETA ** (-jnp.arange(half, dtype=F32) * 2.0 / moba_hd)
    invf = inv_freq.reshape(-1, 1)
    pos = positions.astype(F32)

    return _layer_call(
        x, pos, mem, w_in.astype(F32), mlstm_conv_w.astype(F32), row(mlstm_conv_b), wq, wk, wv, wg, bg,
        row(mlstm_norm_g), row(mlstm_skip), invf, w_mem_kv.astype(F32), w_out.astype(F32),
        row(ln_g), row(ln_b),
        ml_width=ml_width, ml_heads=ml_heads, moba_width=moba_width, mem_width=mem_width)
```

```python
import functools

import jax
import jax.numpy as jnp
from jax import lax
from jax.experimental import pallas as pl
from jax.experimental.pallas import tpu as pltpu

MOBA_HEADS = 4
MOBA_BLOCK = 256
MOBA_TOP_K = 3
MEM_HEADS = 4
ROPE_THETA = 10000.0
DEPTH = 1
DEEPNORM_ALPHA = (2 * DEPTH) ** 0.25
LN_EPS = 1e-5

V7X_LANES = 128
V7X_SUBLANES = 8
V7X_VMEM_BYTES = 64 * 1024 * 1024
V7X_VMEM_LIMIT_BYTES = V7X_VMEM_BYTES - 8 * 1024 * 1024

TOKEN_TILE = MOBA_BLOCK
MASK_VALUE = -1e30
GATE_ROWS = 2 * V7X_SUBLANES
STAGE_ROWS, STAGE_COLS = 512, 1024

F32 = jnp.float32
BF16 = jnp.bfloat16


def _dot(a, b):
    return jnp.dot(a, b, preferred_element_type=F32)


def _dot_nt(a, b):
    return lax.dot_general(a, b, (((1,), (1,)), ((), ())), preferred_element_type=F32)


def _lane_scan(v, op, identity, lane_idx):
    shift = 1
    while shift < v.shape[-1]:
        v = op(v, jnp.where(lane_idx >= shift, pltpu.roll(v, shift, v.ndim - 1), identity))
        shift *= 2
    return v


def _silu(v):
    h = 0.5 * v
    return h + h * jnp.tanh(h)


def _log_sigmoid(v):
    return jnp.minimum(v, 0.0) - jnp.log1p(jnp.exp(-jnp.abs(v)))


def _proj_stage(x_ref, pos_ref, win_ref, convw_ref, convb_ref, wqkv_ref,
                wfold_ref, bg_ref, normg_ref, skip_ref, invf_ref,
                qm_ref, kmt_ref, vm_ref, add_ref, mul_ref, gcol_ref, grow_ref,
                qa_ref, ka_ref, vat_ref, sza_ref, qc_ref, szc_ref,
                xpad_scr, kmean_scr,
                *, t, ml_width, ml_heads, moba_width, mem_width, n_sel, gate_slots):
    tm = x_ref.shape[0]
    hd = ml_width // ml_heads
    moba_hd = moba_width // MOBA_HEADS
    mem_hd = mem_width // MEM_HEADS
    c_zm = ml_width
    c_qa = 2 * ml_width
    c_ka = c_qa + moba_width
    c_va = c_ka + moba_width
    c_za = c_va + moba_width
    c_qc = c_za + moba_width
    c_zc = c_qc + mem_width

    xb = x_ref[...].astype(BF16)
    pad = xpad_scr.shape[0] - tm
    k_w = convw_ref.shape[0]
    nb_lanes = gate_slots

    def x_cols(c0, width):
        return _dot(xb, win_ref[:, c0:c0 + width])

    def rotary_pair(p, qp, kp):
        rots, means = [], []
        for i in range(2):
            h = 2 * p + i
            ls = slice(i * moba_hd, (i + 1) * moba_hd)
            q_rot = qp[:, ls] * cosf + pltpu.roll(qp[:, ls], moba_hd // 2, 1) * sins
            k_rot = kp[:, ls] * cosf + pltpu.roll(kp[:, ls], moba_hd // 2, 1) * sins
            qa_ref[:, 2 * h * moba_hd:(2 * h + 1) * moba_hd] = (q_rot * (moba_hd ** -0.5)).astype(BF16)
            ka_ref[:, h * moba_hd:(h + 1) * moba_hd] = k_rot.astype(BF16)
            rots.append(q_rot)
            means.append(jnp.mean(k_rot, axis=0, keepdims=True))
        return rots, means

    def conv_gate_head(h, xm_h, zm_h):
        hs = slice(h * hd, (h + 1) * hd)
        xpad_scr[pad:pad + tm, hs] = xm_h
        conv = convb_ref[:, hs] + xm_h * convw_ref[k_w - 1:k_w, hs]
        for j in range(k_w - 1):
            conv = conv + xpad_scr[pl.ds(pad - (k_w - 1) + j, tm), hs] * convw_ref[j:j + 1, hs]
        xc_h = _silu(conv)
        sz_h = _silu(zm_h)
        add_ref[:, hs] = (skip_ref[:, hs] * xc_h * sz_h).astype(BF16)
        mul_ref[:, hs] = (normg_ref[:, hs] * sz_h).astype(BF16)
        return xc_h.astype(BF16), xm_h.astype(BF16)

    def blockdiag_head(h, xc_b, xm_b):
        hs = slice(h * hd, (h + 1) * hd)
        q_h = _dot(xc_b, wqkv_ref[h]).astype(BF16)
        k_f = _dot(xc_b, wqkv_ref[ml_heads + h])
        v_h = _dot(xm_b, wqkv_ref[2 * ml_heads + h]).astype(BF16)
        qm_ref[:, hs] = q_h
        vm_ref[:, hs] = v_h
        kmt_ref[hs, :] = (jnp.transpose(k_f) * (hd ** -0.5)).astype(BF16)

    def gate_terms(h, xc_b, xm_b):
        hs = slice(h * hd, (h + 1) * hd)
        return _dot(xc_b, wfold_ref[0, hs, :]) + _dot(xm_b, wfold_ref[1, hs, :])

    def select_blocks(gate_t):
        n_rows = MOBA_HEADS * nb_lanes
        gate = gate_t[0:n_rows, :]
        row = lax.broadcasted_iota(jnp.int32, gate.shape, 0)
        n_r = row & (nb_lanes - 1)
        valid = n_r < t
        gate = jnp.where(valid, gate, -jnp.inf)
        cnt = jnp.zeros(gate.shape, jnp.int32)
        for r in range(1, nb_lanes):
            up = pltpu.roll(gate, r, 0)
            cnt = cnt + ((n_r >= r) & (up >= gate)).astype(jnp.int32)
            dn = pltpu.roll(gate, n_rows - r, 0)
            cnt = cnt + ((n_r < nb_lanes - r) & (dn > gate)).astype(jnp.int32)
        keep = (valid & (cnt < n_sel)) | (n_r == t)
        selb_t = jnp.where(keep, 0.0, MASK_VALUE)
        selb = jnp.transpose(jnp.concatenate(
            [selb_t, jnp.full((V7X_LANES - n_rows, tm), MASK_VALUE, F32)], axis=0))
        for h in range(MOBA_HEADS):
            shift = (V7X_LANES - h * nb_lanes) % V7X_LANES
            sel_h = selb if shift == 0 else pltpu.roll(selb, shift, 1)
            qa_ref[:, (2 * h + 1) * moba_hd:(2 * h + 2) * moba_hd] = sel_h.astype(BF16)

    pw = 2 * moba_hd
    qa0, ka0 = x_cols(c_qa, pw), x_cols(c_ka, pw)
    ang_t = invf_ref[...] * pos_ref[...].astype(F32)
    cos_t = jnp.cos(ang_t)
    sin_t = jnp.sin(ang_t)
    cosf = jnp.transpose(jnp.concatenate([cos_t, cos_t], axis=0))
    sins = jnp.transpose(jnp.concatenate([-sin_t, sin_t], axis=0))
    qa1, ka1 = x_cols(c_qa + pw, pw), x_cols(c_ka + pw, pw)
    rots0, means0 = rotary_pair(0, qa0, ka0)
    xm0, zm0 = x_cols(0, hd), x_cols(c_zm, hd)
    rots1, means1 = rotary_pair(1, qa1, ka1)
    q_rots, kmean_rows = rots0 + rots1, means0 + means1
    q_all = jnp.concatenate(q_rots, axis=1)
    km = kmean_scr[...]
    q_hi = q_all.astype(BF16)
    q_lo = (q_all - q_hi.astype(F32)).astype(BF16)
    k_hi = km.astype(BF16)
    k_lo = (km - k_hi.astype(F32)).astype(BF16)
    xz = {0: (xm0, zm0), 1: (x_cols(hd, hd), x_cols(c_zm + hd, hd))}
    gate = _dot_nt(k_hi, q_hi) + _dot_nt(k_hi, q_lo) + _dot_nt(k_lo, q_hi)
    pending = [("xz", h) for h in range(2, ml_heads)] + [("va", None), ("za", None)]
    qkv = []
    va = za = None
    for h in range(ml_heads):
        xc_b, xm_b = conv_gate_head(h, *xz[h])
        if pending:
            kind, arg = pending.pop(0)
            if kind == "xz":
                xz[arg] = (x_cols(arg * hd, hd), x_cols(c_zm + arg * hd, hd))
            elif kind == "va":
                va = x_cols(c_va, moba_width)
            else:
                za = x_cols(c_za, moba_width)
        blockdiag_head(h, xc_b, xm_b)
        qkv.append((xc_b, xm_b))
    if va is None:
        va = x_cols(c_va, moba_width)
    if za is None:
        za = x_cols(c_za, moba_width)
    g = jnp.zeros((tm, V7X_LANES), F32) + bg_ref[...]
    for h in range(ml_heads):
        g = g + gate_terms(h, *qkv[h])
    sza_ref[...] = _silu(za).astype(BF16)
    qc = x_cols(c_qc, mem_width)
    zc = x_cols(c_zc, mem_width)

    nrow = grow_ref.shape[0]
    gt = jnp.transpose(g)[0:V7X_SUBLANES, :]
    sub = lax.broadcasted_iota(jnp.int32, gt.shape, 0)
    tok_i = lax.broadcasted_iota(jnp.int32, gt.shape, 1)
    b = _lane_scan(jnp.where(sub >= ml_heads, _log_sigmoid(gt), 0.0), jnp.add, 0.0, tok_i)
    a = _lane_scan(jnp.where(sub >= ml_heads, pltpu.roll(gt, ml_heads, 0) - b, -jnp.inf),
                   jnp.maximum, -jnp.inf, tok_i)
    rows = jnp.concatenate([jnp.where(sub < ml_heads, gt, b), pltpu.roll(a, ml_heads, 0)], axis=0)
    grow_ref[...] = rows
    gcol_ref[...] = jnp.transpose(
        jnp.concatenate([rows, jnp.zeros((V7X_LANES - nrow, tm), F32)], axis=0))

    km_row = lax.broadcasted_iota(jnp.int32, kmean_scr.shape, 0)
    km_head = lax.broadcasted_iota(jnp.int32, kmean_scr.shape, 1) // moba_hd
    kmean_scr[...] = jnp.where(km_row == km_head * nb_lanes + t,
                               jnp.concatenate(kmean_rows, axis=1), kmean_scr[...])
    select_blocks(gate)
    vat_ref[...] = jnp.transpose(va).astype(BF16)
    qc_ref[...] = (qc * (mem_hd ** -0.5)).astype(BF16)
    szc_ref[...] = _silu(zc).astype(BF16)


def _mlstm_stage(qm_ref, kmt_ref, vm_ref, gcol_ref, grow_ref, add_ref, mul_ref, out_ref,
                 cn_scr, m_scr, *, heads):
    L = qm_ref.shape[0]
    hd = qm_ref.shape[1] // heads
    row_i = lax.broadcasted_iota(jnp.int32, (L, L), 0)
    col_i = lax.broadcasted_iota(jnp.int32, (L, L), 1)
    causal = row_i >= col_i
    gcol = gcol_ref[...]
    grow = grow_ref[...]
    ones = jnp.ones((L, V7X_LANES), BF16)
    head_slices = [slice(h * hd, (h + 1) * hd) for h in range(heads)]

    qk, inter, qn, mm_l, r_l, w_inter_l, v_aug, cn_prevs = [], [], [], [], [], [], [], []
    for h, hs in enumerate(head_slices):
        q = qm_ref[:, hs]
        cn_prevs.append(cn_scr[h])
        qk.append(_dot(q, kmt_ref[hs, :]))
        q_cn = _dot(q, cn_prevs[h].astype(BF16))
        inter.append(q_cn[:, :hd])
        qn.append(q_cn[:, hd:])
    for h, hs in enumerate(head_slices):
        kt = kmt_ref[hs, :]
        v_aug.append(jnp.concatenate([vm_ref[:, hs], ones], axis=1))
        i_row = grow[h:h + 1, :]
        b_row = grow[heads + h:heads + h + 1, :]
        a_rep = jnp.broadcast_to(gcol[:, 2 * heads + h:2 * heads + h + 1], (L, V7X_LANES))
        m_prev = m_scr[h][0:1, 0:1]
        cn_prev = cn_prevs[h]
        r_row = i_row - b_row
        mm = jnp.maximum(m_prev, a_rep)
        r_l.append(r_row)
        mm_l.append(mm)
        w_inter_l.append(jnp.exp(m_prev - mm))
        b_end = b_row[:, L - 1:L]
        log_w = b_end + r_row
        m_new = jnp.maximum(b_end + m_prev, jnp.max(log_w, axis=1, keepdims=True))
        decay = jnp.exp(b_end + m_prev - m_new)
        kw = (kt.astype(F32) * jnp.exp(log_w - m_new)).astype(BF16)
        cn_scr[h] = decay * cn_prev + _dot(kw, v_aug[h])
        m_scr[h] = jnp.broadcast_to(m_new, m_scr.shape[1:])

    wide = lambda rep, width: jnp.concatenate([rep] * (width // V7X_LANES), axis=1)
    s_qk = [(qk[h] * jnp.exp(jnp.where(causal, r_l[h] - wide(mm_l[h], L), -jnp.inf))).astype(BF16)
            for h in range(heads)]
    pv_rs = [_dot(s_qk[h], v_aug[h]) for h in range(heads)]

    for h, hs in enumerate(head_slices):
        b_rep = jnp.broadcast_to(gcol[:, heads + h:heads + h + 1], (L, V7X_LANES))
        num = wide(w_inter_l[h], hd) * inter[h] + pv_rs[h][:, :hd]
        den = w_inter_l[h] * qn[h] + pv_rs[h][:, hd:]
        rec = 1.0 / jnp.maximum(jnp.abs(den), jnp.exp(-(b_rep + mm_l[h])))
        hh = num * wide(rec, hd)
        mu = jnp.mean(hh, axis=1, keepdims=True)
        hc = hh - mu
        var = jnp.mean(hc * hc, axis=1, keepdims=True)
        hn = hc * lax.rsqrt(var + LN_EPS)
        out_ref[:, hs] = (hn * mul_ref[:, hs].astype(F32) + add_ref[:, hs].astype(F32)).astype(BF16)


def _moba_stage(q_ref, k_ref, vt_ref, sz_ref, out_ref, s_scr, m_scr, *, j):
    tq = q_ref.shape[0]
    bs = MOBA_BLOCK
    nblk = k_ref.shape[0] // bs
    aw = q_ref.shape[1] // MOBA_HEADS
    hd = aw // 2
    ones_rows = 2 * V7X_SUBLANES
    heads = [slice(h * aw, (h + 1) * aw) for h in range(MOBA_HEADS)]
    kv_heads = [slice(h * hd, (h + 1) * hd) for h in range(MOBA_HEADS)]
    tag_lane = lax.broadcasted_iota(jnp.int32, (bs, hd), 1)
    ones_blk = jnp.ones((ones_rows, bs), BF16)

    n_slots = s_scr.shape[0]

    def attend(own):
        causal_t = (lax.broadcasted_iota(jnp.int32, (bs, tq), 0)
                    <= lax.broadcasted_iota(jnp.int32, (bs, tq), 1))

        def score_pass(hi):
            slot = hi % n_slots
            mx = None
            for n in range(own + 1):
                k_aug = jnp.concatenate([k_ref[n * bs:(n + 1) * bs, kv_heads[hi]],
                                         (tag_lane == n).astype(BF16)], axis=1)
                s = _dot_nt(k_aug, q_ref[:, heads[hi]])
                if n == own:
                    s = jnp.where(causal_t, s, MASK_VALUE)
                s_scr[slot, n] = s
                sm = jnp.max(s, axis=0, keepdims=True)
                mx = sm if mx is None else jnp.maximum(mx, sm)
            m_scr[slot] = jnp.broadcast_to(mx, m_scr.shape[1:])

        def value_pass(hi):
            slot = hi % n_slots
            m = m_scr[slot][0:1, :]
            acc = None
            for n in range(own + 1):
                p = jnp.exp(s_scr[slot, n] - m).astype(BF16)
                vt_aug = jnp.concatenate([vt_ref[n, kv_heads[hi], :], ones_blk], axis=0)
                pv = _dot(vt_aug, p)
                acc = pv if acc is None else acc + pv
            o_t = acc[0:hd, :] * (1.0 / acc[hd:hd + 1, :])
            out_ref[:, hi * hd:(hi + 1) * hd] = (
                jnp.transpose(o_t) * sz_ref[:, hi * hd:(hi + 1) * hd].astype(F32)).astype(BF16)

        ahead = min(n_slots, MOBA_HEADS)
        for hi in range(ahead):
            score_pass(hi)
        for hi in range(MOBA_HEADS):
            value_pass(hi)
            if hi + ahead < MOBA_HEADS:
                score_pass(hi + ahead)

    for own in range(nblk):
        pl.when(j == own)(functools.partial(attend, own))


def _mix_stage(x_ref, om_ref, oa_ref, qc_ref, szc_ref, kv_scr, wo_ref, lng_ref, lnb_ref, y_ref):
    width = qc_ref.shape[1]
    hd = width // MEM_HEADS
    w_m = om_ref.shape[1]
    w_a = oa_ref.shape[1]
    head_slices = [slice(h * hd, (h + 1) * hd) for h in range(MEM_HEADS)]
    scores = [_dot_nt(qc_ref[:, hs], kv_scr[:, hs]) for hs in head_slices]
    mixed = _dot(oa_ref[...], wo_ref[w_m:w_m + w_a, :])
    probs = [jnp.exp(s - jnp.max(s, axis=1, keepdims=True)) for s in scores]
    outs = [_dot(p.astype(BF16), kv_scr[:, width + h * hd:width + (h + 1) * hd])
            for h, p in enumerate(probs)]
    oc = jnp.concatenate(
        [(outs[h] * (1.0 / jnp.sum(probs[h], axis=1, keepdims=True))
          * szc_ref[:, hs].astype(F32)).astype(BF16) for h, hs in enumerate(head_slices)], axis=1)
    mixed = mixed + _dot(oc, wo_ref[w_m + w_a:, :])
    mixed = mixed + _dot(om_ref[...], wo_ref[0:w_m, :])
    y = DEEPNORM_ALPHA * x_ref[...] + mixed
    mu = jnp.mean(y, axis=1, keepdims=True)
    yc = y - mu
    var = jnp.mean(yc * yc, axis=1, keepdims=True)
    y_ref[...] = yc * lax.rsqrt(var + LN_EPS) * lng_ref[...] + lnb_ref[...]


def _stage_weights_bf16(pairs, stage_s, sem):
    cr, cc = stage_s.shape[1:]
    chunks = [(src, dst, r0, c0) for src, dst in pairs
              for r0 in range(0, src.shape[0], cr) for c0 in range(0, src.shape[1], cc)]

    def chunk_copy(i):
        src, _, r0, c0 = chunks[i]
        return pltpu.make_async_copy(src.at[pl.ds(r0, cr), pl.ds(c0, cc)], stage_s.at[i % 2], sem.at[i % 2])

    chunk_copy(0).start()
    for i, (_, dst, r0, c0) in enumerate(chunks):
        if i + 1 < len(chunks):
            chunk_copy(i + 1).start()
        chunk_copy(i).wait()
        dst[r0:r0 + cr, c0:c0 + cc] = stage_s[i % 2].astype(BF16)


def _layer_kernel(x_ref, pos_ref, mem_ref, win_hbm, convw_ref, convb_ref, wqkv_ref,
                  wg_ref, bg_ref, normg_ref, skip_ref, invf_ref, wkv_hbm, wo_hbm, lng_ref, lnb_ref,
                  y_ref,
                  qm_s, kmt_s, vm_s, add_s, mul_s, gcol_s, grow_s, qa_s, sza_s, qc_s, szc_s,
                  ka_s, vat_s, om_s, oa_s, xpad_s, kmean_s, cn_s, m_s, score_s, rowmax_s, kv_s, wfold_s,
                  win_ref, wkv_ref, wo_ref, stage_s, stage_sem,
                  *, ml_width, ml_heads, moba_width, mem_width, n_sel, gate_slots):
    t = pl.program_id(1)
    tm = x_ref.shape[0]
    pad = xpad_s.shape[0] - tm
    hd = ml_width // ml_heads

    @pl.when((pl.program_id(0) == 0) & (t == 0))
    def _():
        _stage_weights_bf16([(win_hbm, win_ref), (wkv_hbm, wkv_ref), (wo_hbm, wo_ref)], stage_s, stage_sem)
        for h in range(ml_heads):
            rq, rk, rv = [slice(i * ml_width + h * hd, i * ml_width + (h + 1) * hd) for i in range(3)]
            wq_h, wk_h, wv_h = [wqkv_ref[i * ml_heads + h] for i in range(3)]
            wfold_s[0, rq, :] = (_dot(wq_h, wg_ref[rq, :]) + _dot(wk_h, wg_ref[rk, :])).astype(BF16)
            wfold_s[1, rq, :] = _dot(wv_h, wg_ref[rv, :]).astype(BF16)

    @pl.when(t == 0)
    def _():
        xpad_s[0:pad, :] = jnp.zeros((pad, ml_width), F32)
        kmean_s[...] = jnp.zeros(kmean_s.shape, F32)
        cn_s[...] = jnp.zeros(cn_s.shape, F32)
        m_s[...] = jnp.zeros(m_s.shape, F32)
        kv_s[...] = _dot(mem_ref[...].astype(BF16), wkv_ref[...]).astype(BF16)

    @pl.when(t > 0)
    def _():
        xpad_s[0:pad, :] = xpad_s[tm:tm + pad, :]

    rows = pl.ds(pl.multiple_of(t * tm, tm), tm)
    _proj_stage(x_ref, pos_ref, win_ref, convw_ref, convb_ref, wqkv_ref,
                wfold_s, bg_ref, normg_ref, skip_ref, invf_ref,
                qm_s, kmt_s, vm_s, add_s, mul_s, gcol_s, grow_s,
                qa_s, ka_s.at[rows], vat_s.at[t], sza_s, qc_s, szc_s, xpad_s, kmean_s,
                t=t, ml_width=ml_width, ml_heads=ml_heads, moba_width=moba_width,
                mem_width=mem_width, n_sel=n_sel, gate_slots=gate_slots)
    _mlstm_stage(qm_s, kmt_s, vm_s, gcol_s, grow_s, add_s, mul_s, om_s, cn_s, m_s, heads=ml_heads)
    _moba_stage(qa_s, ka_s, vat_s, sza_s, oa_s, score_s, rowmax_s, j=t)
    _mix_stage(x_ref, om_s, oa_s, qc_s, szc_s, kv_s, wo_ref, lng_ref, lnb_ref, y_ref)


def _layer_call(x, pos, mem, win, convw, convb, wqkv, wg, bg, normg, skip, invf, wkv, wo, lng, lnb,
                *, ml_width, ml_heads, moba_width, mem_width):
    bsz, s, d = x.shape
    m_tok = mem.shape[1]
    tm = TOKEN_TILE
    nt = s // tm
    n_sel = min(MOBA_TOP_K, nt - 1)
    gate_slots = max(V7X_SUBLANES, pl.next_power_of_2(nt))
    assert MOBA_HEADS * gate_slots <= V7X_LANES and s % tm == 0 and ml_heads >= 2
    hd = ml_width // ml_heads
    tok = lambda w: pl.BlockSpec((None, tm, w), lambda b, t: (b, t, 0))

    def resident(a):
        nd = a.ndim
        return pl.BlockSpec(a.shape, lambda b, t, _nd=nd: (0,) * _nd, pipeline_mode=pl.Buffered(1))

    weights = (win, convw, convb, wqkv, wg, bg, normg, skip, invf, wkv, wo, lng, lnb)
    staged = (win, wkv, wo)
    assert all(a.shape[0] % STAGE_ROWS == 0 and a.shape[1] % STAGE_COLS == 0 for a in staged)
    in_specs = ([tok(d),
                 pl.BlockSpec((None, None, 1, tm), lambda b, t: (b, t, 0, 0)),
                 pl.BlockSpec((None, m_tok, d), lambda b, t: (b, 0, 0))]
                + [pl.BlockSpec(memory_space=pl.ANY) if any(a is w for w in staged) else resident(a)
                   for a in weights])
    vmem = pltpu.VMEM
    score_heads = MOBA_HEADS
    scratch = [
        vmem((tm, ml_width), BF16),
        vmem((ml_width, tm), BF16),
        vmem((tm, ml_width), BF16),
        vmem((tm, ml_width), BF16),
        vmem((tm, ml_width), BF16),
        vmem((tm, V7X_LANES), F32),
        vmem((GATE_ROWS, tm), F32),
        vmem((tm, 2 * moba_width), BF16),
        vmem((tm, moba_width), BF16),
        vmem((tm, mem_width), BF16),
        vmem((tm, mem_width), BF16),
        vmem((s, moba_width), BF16),
        vmem((nt, moba_width, tm), BF16),
        vmem((tm, ml_width), BF16),
        vmem((tm, moba_width), BF16),
        vmem((tm + V7X_SUBLANES, ml_width), F32),
        vmem((V7X_LANES, moba_width), F32),
        vmem((ml_heads, hd, hd + V7X_LANES), F32),
        vmem((ml_heads, V7X_SUBLANES, V7X_LANES), F32),
        vmem((score_heads, nt, tm, MOBA_BLOCK), F32),
        vmem((score_heads, V7X_SUBLANES, tm), F32),
        vmem((m_tok, 2 * mem_width), BF16),
        vmem((2, ml_width, V7X_LANES), BF16),
        vmem(win.shape, BF16),
        vmem(wkv.shape, BF16),
        vmem(wo.shape, BF16),
        vmem((2, STAGE_ROWS, STAGE_COLS), F32),
        pltpu.SemaphoreType.DMA((2,)),
    ]
    kern = functools.partial(_layer_kernel, ml_width=ml_width, ml_heads=ml_heads, moba_width=moba_width,
                             mem_width=mem_width, n_sel=n_sel, gate_slots=gate_slots)
    return pl.pallas_call(
        kern,
        grid=(bsz, nt),
        in_specs=in_specs,
        out_specs=tok(d),
        out_shape=jax.ShapeDtypeStruct((bsz, s, d), x.dtype),
        scratch_shapes=scratch,
        compiler_params=pltpu.CompilerParams(dimension_semantics=("arbitrary", "arbitrary"),
                                             vmem_limit_bytes=V7X_VMEM_LIMIT_BYTES),
        name="layer",
    )(x, pos.reshape(bsz, nt, 1, tm), mem, *weights)


def _diag_tiles(w, heads):
    groups, blk, _ = w.shape
    hd = groups // heads * blk
    rows = w.reshape(heads, hd, blk)
    idx = jnp.arange(hd)
    same_group = (idx[:, None] // blk) == (idx[None, :] // blk)
    return jnp.where(same_group, jnp.tile(rows, (1, 1, hd // blk)), 0.0)


def kernel(x, mem, positions, w_in, mlstm_conv_w, mlstm_conv_b, mlstm_wq, mlstm_wk, mlstm_wv, mlstm_w_gates, mlstm_b_gates, mlstm_norm_g, mlstm_skip, w_mem_kv, w_out, ln_g, ln_b):
    bsz, s, d = x.shape
    ml_width = mlstm_conv_w.shape[1]
    ml_heads = mlstm_b_gates.shape[0] // 2
    mem_width = w_mem_kv.shape[1] // 2
    moba_width = (w_in.shape[1] - 2 * ml_width - 2 * mem_width) // 4
    moba_hd = moba_width // MOBA_HEADS
    assert moba_hd == V7X_LANES and mem_width // MEM_HEADS == V7X_LANES and MOBA_HEADS == 4
    assert s % TOKEN_TILE == 0

    row = lambda a: a.reshape(1, -1).astype(F32)
    wqkv = _diag_tiles(jnp.concatenate([mlstm_wq, mlstm_wk, mlstm_wv]), 3 * ml_heads).astype(BF16)
    n_gates = mlstm_w_gates.shape[1]
    wg = jnp.pad(mlstm_w_gates, ((0, 0), (0, V7X_LANES - n_gates))).astype(BF16)
    bg = jnp.pad(mlstm_b_gates, (0, V7X_LANES - n_gates)).reshape(1, -1).astype(F32)
    half = moba_hd // 2
    inv_freq = ROPE_THETA ** (-jnp.arange(half, dtype=F32) * 2.0 / moba_hd)
    invf = inv_freq.reshape(-1, 1)
    pos = positions

    return _layer_call(
        x, pos, mem, w_in.astype(F32), mlstm_conv_w.astype(F32), row(mlstm_conv_b), wqkv, wg, bg,
        row(mlstm_norm_g), row(mlstm_skip), invf, w_mem_kv.astype(F32), w_out.astype(F32),
        row(ln_g), row(ln_b),
        ml_width=ml_width, ml_heads=ml_heads, moba_width=moba_width, mem_width=mem_width)
```

```python
import functools

import jax
import jax.numpy as jnp
from jax import lax
from jax.experimental import pallas as pl
from jax.experimental.pallas import tpu as pltpu

MOBA_HEADS = 4
MOBA_BLOCK = 256
MOBA_TOP_K = 3
MEM_HEADS = 4
ROPE_THETA = 10000.0
DEPTH = 1
DEEPNORM_ALPHA = (2 * DEPTH) ** 0.25
LN_EPS = 1e-5

V7X_LANES = 128
V7X_SUBLANES = 8
V7X_VMEM_BYTES = 64 * 1024 * 1024
V7X_VMEM_LIMIT_BYTES = V7X_VMEM_BYTES - 8 * 1024 * 1024

TOKEN_TILE = MOBA_BLOCK
MASK_VALUE = -1e30
GATE_ROWS = 2 * V7X_SUBLANES
STAGE_ROWS, STAGE_COLS = 256, 1024

F32 = jnp.float32
BF16 = jnp.bfloat16


def _dot(a, b):
    return jnp.dot(a, b, preferred_element_type=F32)


def _dot_nt(a, b):
    return lax.dot_general(a, b, (((1,), (1,)), ((), ())), preferred_element_type=F32)


def _lane_scan(v, op, identity, lane_idx):
    shift = 1
    while shift < v.shape[-1]:
        v = op(v, jnp.where(lane_idx >= shift, pltpu.roll(v, shift, v.ndim - 1), identity))
        shift *= 2
    return v


def _silu(v):
    h = 0.5 * v
    return h + h * jnp.tanh(h)


def _log_sigmoid(v):
    return jnp.minimum(v, 0.0) - jnp.log1p(jnp.exp(-jnp.abs(v)))


def _proj_stage(x_ref, pos_ref, win_ref, convw_ref, convb_ref, wqkv_ref,
                wfold_ref, bg_ref, normg_ref, skip_ref, invf_ref,
                qm_ref, kmt_ref, vm_ref, add_ref, mul_ref, gcol_ref, grow_ref,
                qa_ref, ka_ref, vat_ref, sza_ref, qc_ref, szc_ref,
                xpad_scr, kmean_scr,
                *, t, ml_width, ml_heads, moba_width, mem_width, n_sel, gate_slots):
    tm = x_ref.shape[0]
    hd = ml_width // ml_heads
    moba_hd = moba_width // MOBA_HEADS
    mem_hd = mem_width // MEM_HEADS
    c_zm = ml_width
    c_qa = 2 * ml_width
    c_ka = c_qa + moba_width
    c_va = c_ka + moba_width
    c_za = c_va + moba_width
    c_qc = c_za + moba_width
    c_zc = c_qc + mem_width

    xb = x_ref[...].astype(BF16)
    pad = xpad_scr.shape[0] - tm
    k_w = convw_ref.shape[0]
    nb_lanes = gate_slots

    def x_cols(c0, width):
        return _dot(xb, win_ref[:, c0:c0 + width])

    def rotary_pair(p, qp, kp):
        rots, means = [], []
        for i in range(2):
            h = 2 * p + i
            ls = slice(i * moba_hd, (i + 1) * moba_hd)
            q_rot = qp[:, ls] * cosf + pltpu.roll(qp[:, ls], moba_hd // 2, 1) * sins
            k_rot = kp[:, ls] * cosf + pltpu.roll(kp[:, ls], moba_hd // 2, 1) * sins
            qa_ref[:, 2 * h * moba_hd:(2 * h + 1) * moba_hd] = (q_rot * (moba_hd ** -0.5)).astype(BF16)
            ka_ref[:, h * moba_hd:(h + 1) * moba_hd] = k_rot.astype(BF16)
            rots.append(q_rot)
            means.append(jnp.mean(k_rot, axis=0, keepdims=True))
        return rots, means

    def conv_gate_head(h, xm_h, zm_h):
        hs = slice(h * hd, (h + 1) * hd)
        xpad_scr[pad:pad + tm, hs] = xm_h
        conv = convb_ref[:, hs] + xm_h * convw_ref[k_w - 1:k_w, hs]
        for j in range(k_w - 1):
            conv = conv + xpad_scr[pl.ds(pad - (k_w - 1) + j, tm), hs] * convw_ref[j:j + 1, hs]
        xc_h = _silu(conv)
        sz_h = _silu(zm_h)
        add_ref[:, hs] = (skip_ref[:, hs] * xc_h * sz_h).astype(BF16)
        mul_ref[:, hs] = (normg_ref[:, hs] * sz_h).astype(BF16)
        return xc_h.astype(BF16), xm_h.astype(BF16)

    def blockdiag_head(h, xc_b, xm_b):
        hs = slice(h * hd, (h + 1) * hd)
        q_h = _dot(xc_b, wqkv_ref[h]).astype(BF16)
        k_f = _dot(xc_b, wqkv_ref[ml_heads + h])
        v_h = _dot(xm_b, wqkv_ref[2 * ml_heads + h]).astype(BF16)
        qm_ref[:, hs] = q_h
        vm_ref[:, hs] = v_h
        kmt_ref[hs, :] = (jnp.transpose(k_f) * (hd ** -0.5)).astype(BF16)

    def gate_terms(h, xc_b, xm_b):
        hs = slice(h * hd, (h + 1) * hd)
        return _dot(xc_b, wfold_ref[0, hs, :]) + _dot(xm_b, wfold_ref[1, hs, :])

    def select_blocks(gate_t):
        n_rows = MOBA_HEADS * nb_lanes
        gate = gate_t[0:n_rows, :]
        row = lax.broadcasted_iota(jnp.int32, gate.shape, 0)
        n_r = row & (nb_lanes - 1)
        valid = n_r < t
        gate = jnp.where(valid, gate, -jnp.inf)
        cnt = jnp.zeros(gate.shape, jnp.int32)
        for r in range(1, nb_lanes):
            up = pltpu.roll(gate, r, 0)
            cnt = cnt + ((n_r >= r) & (up >= gate)).astype(jnp.int32)
            dn = pltpu.roll(gate, n_rows - r, 0)
            cnt = cnt + ((n_r < nb_lanes - r) & (dn > gate)).astype(jnp.int32)
        keep = (valid & (cnt < n_sel)) | (n_r == t)
        selb_t = jnp.where(keep, 0.0, MASK_VALUE)
        selb = jnp.transpose(jnp.concatenate(
            [selb_t, jnp.full((V7X_LANES - n_rows, tm), MASK_VALUE, F32)], axis=0))
        for h in range(MOBA_HEADS):
            shift = (V7X_LANES - h * nb_lanes) % V7X_LANES
            sel_h = selb if shift == 0 else pltpu.roll(selb, shift, 1)
            qa_ref[:, (2 * h + 1) * moba_hd:(2 * h + 2) * moba_hd] = sel_h.astype(BF16)

    pw = 2 * moba_hd
    qa0, ka0 = x_cols(c_qa, pw), x_cols(c_ka, pw)
    ang_t = invf_ref[...] * pos_ref[...].astype(F32)
    cos_t = jnp.cos(ang_t)
    sin_t = jnp.sin(ang_t)
    cosf = jnp.transpose(jnp.concatenate([cos_t, cos_t], axis=0))
    sins = jnp.transpose(jnp.concatenate([-sin_t, sin_t], axis=0))
    qa1, ka1 = x_cols(c_qa + pw, pw), x_cols(c_ka + pw, pw)
    rots0, means0 = rotary_pair(0, qa0, ka0)
    xm0, zm0 = x_cols(0, hd), x_cols(c_zm, hd)
    rots1, means1 = rotary_pair(1, qa1, ka1)
    q_rots, kmean_rows = rots0 + rots1, means0 + means1
    q_all = jnp.concatenate(q_rots, axis=1)
    km = kmean_scr[...]
    q_hi = q_all.astype(BF16)
    q_lo = (q_all - q_hi.astype(F32)).astype(BF16)
    k_hi = km.astype(BF16)
    k_lo = (km - k_hi.astype(F32)).astype(BF16)
    xz = {0: (xm0, zm0), 1: (x_cols(hd, hd), x_cols(c_zm + hd, hd))}
    gate = _dot_nt(k_hi, q_hi) + _dot_nt(k_hi, q_lo) + _dot_nt(k_lo, q_hi)
    pending = [("xz", h) for h in range(2, ml_heads)] + [("va", None), ("za", None)]
    qkv = []
    va = za = None
    for h in range(ml_heads):
        xc_b, xm_b = conv_gate_head(h, *xz[h])
        if pending:
            kind, arg = pending.pop(0)
            if kind == "xz":
                xz[arg] = (x_cols(arg * hd, hd), x_cols(c_zm + arg * hd, hd))
            elif kind == "va":
                va = x_cols(c_va, moba_width)
            else:
                za = x_cols(c_za, moba_width)
        blockdiag_head(h, xc_b, xm_b)
        qkv.append((xc_b, xm_b))
    if va is None:
        va = x_cols(c_va, moba_width)
    if za is None:
        za = x_cols(c_za, moba_width)
    g = jnp.zeros((tm, V7X_LANES), F32) + bg_ref[...]
    for h in range(ml_heads):
        g = g + gate_terms(h, *qkv[h])
    sza_ref[...] = _silu(za).astype(BF16)
    qc = x_cols(c_qc, mem_width)
    zc = x_cols(c_zc, mem_width)

    nrow = grow_ref.shape[0]
    gt = jnp.transpose(g)[0:V7X_SUBLANES, :]
    sub = lax.broadcasted_iota(jnp.int32, gt.shape, 0)
    tok_i = lax.broadcasted_iota(jnp.int32, gt.shape, 1)
    b = _lane_scan(jnp.where(sub >= ml_heads, _log_sigmoid(gt), 0.0), jnp.add, 0.0, tok_i)
    a = _lane_scan(jnp.where(sub >= ml_heads, pltpu.roll(gt, ml_heads, 0) - b, -jnp.inf),
                   jnp.maximum, -jnp.inf, tok_i)
    rows = jnp.concatenate([jnp.where(sub < ml_heads, gt, b), pltpu.roll(a, ml_heads, 0)], axis=0)
    grow_ref[...] = rows
    gcol_ref[...] = jnp.transpose(
        jnp.concatenate([rows, jnp.zeros((V7X_LANES - nrow, tm), F32)], axis=0))

    km_row = lax.broadcasted_iota(jnp.int32, kmean_scr.shape, 0)
    km_head = lax.broadcasted_iota(jnp.int32, kmean_scr.shape, 1) // moba_hd
    kmean_scr[...] = jnp.where(km_row == km_head * nb_lanes + t,
                               jnp.concatenate(kmean_rows, axis=1), kmean_scr[...])
    select_blocks(gate)
    vat_ref[...] = jnp.transpose(va).astype(BF16)
    qc_ref[...] = (qc * (mem_hd ** -0.5)).astype(BF16)
    szc_ref[...] = _silu(zc).astype(BF16)


def _mlstm_stage(qm_ref, kmt_ref, vm_ref, gcol_ref, grow_ref, add_ref, mul_ref, out_ref,
                 cn_scr, m_scr, *, heads):
    L = qm_ref.shape[0]
    hd = qm_ref.shape[1] // heads
    row_i = lax.broadcasted_iota(jnp.int32, (L, L), 0)
    col_i = lax.broadcasted_iota(jnp.int32, (L, L), 1)
    causal = row_i >= col_i
    gcol = gcol_ref[...]
    grow = grow_ref[...]
    ones = jnp.ones((L, V7X_LANES), BF16)
    head_slices = [slice(h * hd, (h + 1) * hd) for h in range(heads)]

    qk, inter, qn, mm_l, r_l, w_inter_l, v_aug, cn_prevs = [], [], [], [], [], [], [], []
    for h, hs in enumerate(head_slices):
        q = qm_ref[:, hs]
        cn_prevs.append(cn_scr[h])
        qk.append(_dot(q, kmt_ref[hs, :]))
        q_cn = _dot(q, cn_prevs[h].astype(BF16))
        inter.append(q_cn[:, :hd])
        qn.append(q_cn[:, hd:])
    for h, hs in enumerate(head_slices):
        kt = kmt_ref[hs, :]
        v_aug.append(jnp.concatenate([vm_ref[:, hs], ones], axis=1))
        i_row = grow[h:h + 1, :]
        b_row = grow[heads + h:heads + h + 1, :]
        a_rep = jnp.broadcast_to(gcol[:, 2 * heads + h:2 * heads + h + 1], (L, V7X_LANES))
        m_prev = m_scr[h][0:1, 0:1]
        cn_prev = cn_prevs[h]
        r_row = i_row - b_row
        mm = jnp.maximum(m_prev, a_rep)
        r_l.append(r_row)
        mm_l.append(mm)
        w_inter_l.append(jnp.exp(m_prev - mm))
        b_end = b_row[:, L - 1:L]
        log_w = b_end + r_row
        m_new = jnp.maximum(b_end + m_prev, jnp.max(log_w, axis=1, keepdims=True))
        decay = jnp.exp(b_end + m_prev - m_new)
        kw = (kt.astype(F32) * jnp.exp(log_w - m_new)).astype(BF16)
        cn_scr[h] = decay * cn_prev + _dot(kw, v_aug[h])
        m_scr[h] = jnp.broadcast_to(m_new, m_scr.shape[1:])

    wide = lambda rep, width: jnp.concatenate([rep] * (width // V7X_LANES), axis=1)
    s_qk = [(qk[h] * jnp.exp(jnp.where(causal, r_l[h] - wide(mm_l[h], L), -jnp.inf))).astype(BF16)
            for h in range(heads)]
    pv_rs = [_dot(s_qk[h], v_aug[h]) for h in range(heads)]

    for h, hs in enumerate(head_slices):
        b_rep = jnp.broadcast_to(gcol[:, heads + h:heads + h + 1], (L, V7X_LANES))
        num = wide(w_inter_l[h], hd) * inter[h] + pv_rs[h][:, :hd]
        den = w_inter_l[h] * qn[h] + pv_rs[h][:, hd:]
        rec = 1.0 / jnp.maximum(jnp.abs(den), jnp.exp(-(b_rep + mm_l[h])))
        hh = num * wide(rec, hd)
        mu = jnp.mean(hh, axis=1, keepdims=True)
        hc = hh - mu
        var = jnp.mean(hc * hc, axis=1, keepdims=True)
        hn = hc * lax.rsqrt(var + LN_EPS)
        out_ref[:, hs] = (hn * mul_ref[:, hs].astype(F32) + add_ref[:, hs].astype(F32)).astype(BF16)


def _moba_stage(q_ref, k_ref, vt_ref, sz_ref, out_ref, s_scr, m_scr, *, j, before):
    tq = q_ref.shape[0]
    bs = MOBA_BLOCK
    nblk = k_ref.shape[0] // bs
    aw = q_ref.shape[1] // MOBA_HEADS
    hd = aw // 2
    ones_rows = 2 * V7X_SUBLANES
    heads = [slice(h * aw, (h + 1) * aw) for h in range(MOBA_HEADS)]
    kv_heads = [slice(h * hd, (h + 1) * hd) for h in range(MOBA_HEADS)]
    tag_lane = lax.broadcasted_iota(jnp.int32, (bs, hd), 1)
    ones_blk = jnp.ones((ones_rows, bs), BF16)

    n_slots = s_scr.shape[0]

    def attend(own):
        before()
        causal_t = (lax.broadcasted_iota(jnp.int32, (bs, tq), 0)
                    <= lax.broadcasted_iota(jnp.int32, (bs, tq), 1))

        def score_pass(hi):
            slot = hi % n_slots
            mx = None
            for n in range(own + 1):
                k_aug = jnp.concatenate([k_ref[n * bs:(n + 1) * bs, kv_heads[hi]],
                                         (tag_lane == n).astype(BF16)], axis=1)
                s = _dot_nt(k_aug, q_ref[:, heads[hi]])
                if n >= own - 1:
                    s = jnp.where(causal_t | (j != n), s, MASK_VALUE)
                s_scr[slot, n] = s
                sm = jnp.max(s, axis=0, keepdims=True)
                mx = sm if mx is None else jnp.maximum(mx, sm)
            m_scr[slot] = jnp.broadcast_to(mx, m_scr.shape[1:])

        def value_pass(hi):
            slot = hi % n_slots
            m = m_scr[slot][0:1, :]
            acc = None
            for n in range(own + 1):
                p = jnp.exp(s_scr[slot, n] - m).astype(BF16)
                vt_aug = jnp.concatenate([vt_ref[n, kv_heads[hi], :], ones_blk], axis=0)
                pv = _dot(vt_aug, p)
                acc = pv if acc is None else acc + pv
            o_t = acc[0:hd, :] * (1.0 / acc[hd:hd + 1, :])
            out_ref[:, hi * hd:(hi + 1) * hd] = (
                jnp.transpose(o_t) * sz_ref[:, hi * hd:(hi + 1) * hd].astype(F32)).astype(BF16)

        ahead = min(n_slots, MOBA_HEADS)
        for hi in range(ahead):
            score_pass(hi)
        for hi in range(MOBA_HEADS):
            value_pass(hi)
            if hi + ahead < MOBA_HEADS:
                score_pass(hi + ahead)

    assert nblk % 2 == 0
    for own in range(1, nblk, 2):
        pl.when((j | 1) == own)(functools.partial(attend, own))


def _mix_stage(x_ref, om_ref, oa_ref, qc_ref, szc_ref, kv_scr, wo_ref, lng_ref, lnb_ref, y_ref):
    width = qc_ref.shape[1]
    hd = width // MEM_HEADS
    w_m = om_ref.shape[1]
    w_a = oa_ref.shape[1]
    head_slices = [slice(h * hd, (h + 1) * hd) for h in range(MEM_HEADS)]
    scores = [_dot_nt(qc_ref[:, hs], kv_scr[:, hs]) for hs in head_slices]
    mixed = _dot(oa_ref[...], wo_ref[w_m:w_m + w_a, :])
    probs = [jnp.exp(s - jnp.max(s, axis=1, keepdims=True)) for s in scores]
    outs = [_dot(p.astype(BF16), kv_scr[:, width + h * hd:width + (h + 1) * hd])
            for h, p in enumerate(probs)]
    oc = jnp.concatenate(
        [(outs[h] * (1.0 / jnp.sum(probs[h], axis=1, keepdims=True))
          * szc_ref[:, hs].astype(F32)).astype(BF16) for h, hs in enumerate(head_slices)], axis=1)
    mixed = mixed + _dot(oc, wo_ref[w_m + w_a:, :])
    mixed = mixed + _dot(om_ref[...], wo_ref[0:w_m, :])
    y = DEEPNORM_ALPHA * x_ref[...] + mixed
    mu = jnp.mean(y, axis=1, keepdims=True)
    yc = y - mu
    var = jnp.mean(yc * yc, axis=1, keepdims=True)
    y_ref[...] = yc * lax.rsqrt(var + LN_EPS) * lng_ref[...] + lnb_ref[...]


def _stage_weights_bf16(pairs, stage_s, sem):
    cr, cc = stage_s.shape[1:]
    chunks = [(src, dst, r0, c0) for src, dst in pairs
              for r0 in range(0, src.shape[0], cr) for c0 in range(0, src.shape[1], cc)]

    def chunk_copy(i):
        src, _, r0, c0 = chunks[i]
        return pltpu.make_async_copy(src.at[pl.ds(r0, cr), pl.ds(c0, cc)], stage_s.at[i % 2], sem.at[i % 2])

    chunk_copy(0).start()
    for i, (_, dst, r0, c0) in enumerate(chunks):
        if i + 1 < len(chunks):
            chunk_copy(i + 1).start()
        chunk_copy(i).wait()
        dst[r0:r0 + cr, c0:c0 + cc] = stage_s[i % 2].astype(BF16)


def _layer_kernel(x_ref, pos_ref, mem_ref, win_hbm, convw_ref, convb_ref, wqkv_ref,
                  wg_ref, bg_ref, normg_ref, skip_ref, invf_ref, wkv_hbm, wo_hbm, lng_ref, lnb_ref,
                  y_ref,
                  qm_s, kmt_s, vm_s, add_s, mul_s, gcol_s, grow_s, qa_s, sza_s, qc_s, szc_s,
                  ka_s, vat_s, om_s, oa_s, xpad_s, kmean_s, cn_s, m_s, score_s, rowmax_s, kv_s, wfold_s,
                  win_ref, wkv_ref, wo_ref, stage_s, stage_sem,
                  *, ml_width, ml_heads, moba_width, mem_width, n_sel, gate_slots):
    t = pl.program_id(1)
    tm = x_ref.shape[0]
    pad = xpad_s.shape[0] - tm
    hd = ml_width // ml_heads

    @pl.when((pl.program_id(0) == 0) & (t == 0))
    def _():
        _stage_weights_bf16([(win_hbm, win_ref), (wkv_hbm, wkv_ref), (wo_hbm, wo_ref)], stage_s, stage_sem)
        for h in range(ml_heads):
            rq, rk, rv = [slice(i * ml_width + h * hd, i * ml_width + (h + 1) * hd) for i in range(3)]
            wq_h, wk_h, wv_h = [wqkv_ref[i * ml_heads + h] for i in range(3)]
            wfold_s[0, rq, :] = (_dot(wq_h, wg_ref[rq, :]) + _dot(wk_h, wg_ref[rk, :])).astype(BF16)
            wfold_s[1, rq, :] = _dot(wv_h, wg_ref[rv, :]).astype(BF16)

    @pl.when(t == 0)
    def _():
        xpad_s[0:pad, :] = jnp.zeros((pad, ml_width), F32)
        kmean_s[...] = jnp.zeros(kmean_s.shape, F32)
        cn_s[...] = jnp.zeros(cn_s.shape, F32)
        m_s[...] = jnp.zeros(m_s.shape, F32)
        ka_s[...] = jnp.zeros(ka_s.shape, BF16)
        vat_s[...] = jnp.zeros(vat_s.shape, BF16)
        kv_s[...] = _dot(mem_ref[...].astype(BF16), wkv_ref[...]).astype(BF16)

    @pl.when(t > 0)
    def _():
        xpad_s[0:pad, :] = xpad_s[tm:tm + pad, :]

    rows = pl.ds(pl.multiple_of(t * tm, tm), tm)
    _proj_stage(x_ref, pos_ref, win_ref, convw_ref, convb_ref, wqkv_ref,
                wfold_s, bg_ref, normg_ref, skip_ref, invf_ref,
                qm_s, kmt_s, vm_s, add_s, mul_s, gcol_s, grow_s,
                qa_s, ka_s.at[rows], vat_s.at[t], sza_s, qc_s, szc_s, xpad_s, kmean_s,
                t=t, ml_width=ml_width, ml_heads=ml_heads, moba_width=moba_width,
                mem_width=mem_width, n_sel=n_sel, gate_slots=gate_slots)
    _moba_stage(qa_s, ka_s, vat_s, sza_s, oa_s, score_s, rowmax_s, j=t,
                before=functools.partial(_mlstm_stage, qm_s, kmt_s, vm_s, gcol_s, grow_s, add_s, mul_s, om_s,
                                         cn_s, m_s, heads=ml_heads))
    _mix_stage(x_ref, om_s, oa_s, qc_s, szc_s, kv_s, wo_ref, lng_ref, lnb_ref, y_ref)


def _layer_call(x, pos, mem, win, convw, convb, wqkv, wg, bg, normg, skip, invf, wkv, wo, lng, lnb,
                *, ml_width, ml_heads, moba_width, mem_width):
    bsz, s, d = x.shape
    m_tok = mem.shape[1]
    tm = TOKEN_TILE
    nt = s // tm
    n_sel = min(MOBA_TOP_K, nt - 1)
    gate_slots = max(V7X_SUBLANES, pl.next_power_of_2(nt))
    assert MOBA_HEADS * gate_slots <= V7X_LANES and s % tm == 0 and ml_heads >= 2
    hd = ml_width // ml_heads
    tok = lambda w: pl.BlockSpec((None, tm, w), lambda b, t: (b, t, 0))

    def resident(a):
        nd = a.ndim
        return pl.BlockSpec(a.shape, lambda b, t, _nd=nd: (0,) * _nd, pipeline_mode=pl.Buffered(1))

    weights = (win, convw, convb, wqkv, wg, bg, normg, skip, invf, wkv, wo, lng, lnb)
    staged = (win, wkv, wo)
    assert all(a.shape[0] % STAGE_ROWS == 0 and a.shape[1] % STAGE_COLS == 0 for a in staged)
    in_specs = ([tok(d),
                 pl.BlockSpec((None, None, 1, tm), lambda b, t: (b, t, 0, 0)),
                 pl.BlockSpec((None, m_tok, d), lambda b, t: (b, 0, 0))]
                + [pl.BlockSpec(memory_space=pl.ANY) if any(a is w for w in staged) else resident(a)
                   for a in weights])
    vmem = pltpu.VMEM
    score_heads = MOBA_HEADS
    scratch = [
        vmem((tm, ml_width), BF16),
        vmem((ml_width, tm), BF16),
        vmem((tm, ml_width), BF16),
        vmem((tm, ml_width), BF16),
        vmem((tm, ml_width), BF16),
        vmem((tm, V7X_LANES), F32),
        vmem((GATE_ROWS, tm), F32),
        vmem((tm, 2 * moba_width), BF16),
        vmem((tm, moba_width), BF16),
        vmem((tm, mem_width), BF16),
        vmem((tm, mem_width), BF16),
        vmem((s, moba_width), BF16),
        vmem((nt, moba_width, tm), BF16),
        vmem((tm, ml_width), BF16),
        vmem((tm, moba_width), BF16),
        vmem((tm + V7X_SUBLANES, ml_width), F32),
        vmem((V7X_LANES, moba_width), F32),
        vmem((ml_heads, hd, hd + V7X_LANES), F32),
        vmem((ml_heads, V7X_SUBLANES, V7X_LANES), F32),
        vmem((score_heads, nt, tm, MOBA_BLOCK), F32),
        vmem((score_heads, V7X_SUBLANES, tm), F32),
        vmem((m_tok, 2 * mem_width), BF16),
        vmem((2, ml_width, V7X_LANES), BF16),
        vmem(win.shape, BF16),
        vmem(wkv.shape, BF16),
        vmem(wo.shape, BF16),
        vmem((2, STAGE_ROWS, STAGE_COLS), F32),
        pltpu.SemaphoreType.DMA((2,)),
    ]
    kern = functools.partial(_layer_kernel, ml_width=ml_width, ml_heads=ml_heads, moba_width=moba_width,
                             mem_width=mem_width, n_sel=n_sel, gate_slots=gate_slots)
    return pl.pallas_call(
        kern,
        grid=(bsz, nt),
        in_specs=in_specs,
        out_specs=tok(d),
        out_shape=jax.ShapeDtypeStruct((bsz, s, d), x.dtype),
        scratch_shapes=scratch,
        compiler_params=pltpu.CompilerParams(dimension_semantics=("arbitrary", "arbitrary"),
                                             vmem_limit_bytes=V7X_VMEM_LIMIT_BYTES),
        name="layer",
    )(x, pos.reshape(bsz, nt, 1, tm), mem, *weights)


def _diag_tiles(w, heads):
    groups, blk, _ = w.shape
    hd = groups // heads * blk
    rows = w.reshape(heads, hd, blk)
    idx = jnp.arange(hd)
    same_group = (idx[:, None] // blk) == (idx[None, :] // blk)
    return jnp.where(same_group, jnp.tile(rows, (1, 1, hd // blk)), 0.0)


def kernel(x, mem, positions, w_in, mlstm_conv_w, mlstm_conv_b, mlstm_wq, mlstm_wk, mlstm_wv, mlstm_w_gates, mlstm_b_gates, mlstm_norm_g, mlstm_skip, w_mem_kv, w_out, ln_g, ln_b):
    bsz, s, d = x.shape
    ml_width = mlstm_conv_w.shape[1]
    ml_heads = mlstm_b_gates.shape[0] // 2
    mem_width = w_mem_kv.shape[1] // 2
    moba_width = (w_in.shape[1] - 2 * ml_width - 2 * mem_width) // 4
    moba_hd = moba_width // MOBA_HEADS
    assert moba_hd == V7X_LANES and mem_width // MEM_HEADS == V7X_LANES and MOBA_HEADS == 4
    assert s % TOKEN_TILE == 0

    row = lambda a: a.reshape(1, -1).astype(F32)
    wqkv = _diag_tiles(jnp.concatenate([mlstm_wq, mlstm_wk, mlstm_wv]), 3 * ml_heads).astype(BF16)
    n_gates = mlstm_w_gates.shape[1]
    wg = jnp.pad(mlstm_w_gates, ((0, 0), (0, V7X_LANES - n_gates))).astype(BF16)
    bg = jnp.pad(mlstm_b_gates, (0, V7X_LANES - n_gates)).reshape(1, -1).astype(F32)
    half = moba_hd // 2
    inv_freq = ROPE_THETA ** (-jnp.arange(half, dtype=F32) * 2.0 / moba_hd)
    invf = inv_freq.reshape(-1, 1)
    pos = positions

    return _layer_call(
        x, pos, mem, w_in.astype(F32), mlstm_conv_w.astype(F32), row(mlstm_conv_b), wqkv, wg, bg,
        row(mlstm_norm_g), row(mlstm_skip), invf, w_mem_kv.astype(F32), w_out.astype(F32),
        row(ln_g), row(ln_b),
        ml_width=ml_width, ml_heads=ml_heads, moba_width=moba_width, mem_width=mem_width)
```

```python
import functools

import jax
import jax.numpy as jnp
from jax import lax
from jax.experimental import pallas as pl
from jax.experimental.pallas import tpu as pltpu

MOBA_HEADS = 4
MOBA_BLOCK = 256
MOBA_TOP_K = 3
MEM_HEADS = 4
ROPE_THETA = 10000.0
DEPTH = 1
DEEPNORM_ALPHA = (2 * DEPTH) ** 0.25
LN_EPS = 1e-5

V7X_LANES = 128
V7X_SUBLANES = 8
V7X_VMEM_BYTES = 64 * 1024 * 1024
V7X_VMEM_LIMIT_BYTES = V7X_VMEM_BYTES - 8 * 1024 * 1024

TOKEN_TILE = MOBA_BLOCK
MASK_VALUE = -1e30
GATE_ROWS = 2 * V7X_SUBLANES
STAGE_ROWS, STAGE_COLS = 512, 1024

F32 = jnp.float32
BF16 = jnp.bfloat16


def _dot(a, b):
    return jnp.dot(a, b, preferred_element_type=F32)


def _dot_nt(a, b):
    return lax.dot_general(a, b, (((1,), (1,)), ((), ())), preferred_element_type=F32)


def _lane_scan(v, op, identity, lane_idx):
    shift = 1
    while shift < v.shape[-1]:
        v = op(v, jnp.where(lane_idx >= shift, pltpu.roll(v, shift, v.ndim - 1), identity))
        shift *= 2
    return v


def _silu(v):
    h = 0.5 * v
    return h + h * jnp.tanh(h)


def _log_sigmoid(v):
    return jnp.minimum(v, 0.0) - jnp.log1p(jnp.exp(-jnp.abs(v)))


def _proj_stage(x_ref, pos_ref, win_ref, convw_ref, convb_ref, wqkv_ref,
                wfold_ref, bg_ref, normg_ref, skip_ref, invf_ref,
                qm_ref, kmt_ref, vm_ref, add_ref, mul_ref, gcol_ref, grow_ref,
                qa_ref, ka_ref, vat_ref, sza_ref, qc_ref, szc_ref,
                xpad_scr, kmean_scr,
                *, t, ml_width, ml_heads, moba_width, mem_width, n_sel, gate_slots):
    tm = x_ref.shape[0]
    hd = ml_width // ml_heads
    moba_hd = moba_width // MOBA_HEADS
    mem_hd = mem_width // MEM_HEADS
    c_zm = ml_width
    c_qa = 2 * ml_width
    c_ka = c_qa + moba_width
    c_va = c_ka + moba_width
    c_za = c_va + moba_width
    c_qc = c_za + moba_width
    c_zc = c_qc + mem_width

    xb = x_ref[...].astype(BF16)
    pad = xpad_scr.shape[0] - tm
    k_w = convw_ref.shape[0]
    nb_lanes = gate_slots

    def x_cols(c0, width):
        return _dot(xb, win_ref[:, c0:c0 + width])

    def rotary_pair(p, qp, kp):
        rots, means = [], []
        for i in range(2):
            h = 2 * p + i
            ls = slice(i * moba_hd, (i + 1) * moba_hd)
            q_rot = qp[:, ls] * cosf + pltpu.roll(qp[:, ls], moba_hd // 2, 1) * sins
            k_rot = kp[:, ls] * cosf + pltpu.roll(kp[:, ls], moba_hd // 2, 1) * sins
            qa_ref[:, 2 * h * moba_hd:(2 * h + 1) * moba_hd] = (q_rot * (moba_hd ** -0.5)).astype(BF16)
            ka_ref[:, h * moba_hd:(h + 1) * moba_hd] = k_rot.astype(BF16)
            rots.append(q_rot)
            means.append(jnp.mean(k_rot, axis=0, keepdims=True))
        return rots, means

    def conv_gate_head(h, xm_h, zm_h):
        hs = slice(h * hd, (h + 1) * hd)
        xpad_scr[pad:pad + tm, hs] = xm_h
        conv = convb_ref[:, hs] + xm_h * convw_ref[k_w - 1:k_w, hs]
        for j in range(k_w - 1):
            conv = conv + xpad_scr[pl.ds(pad - (k_w - 1) + j, tm), hs] * convw_ref[j:j + 1, hs]
        xc_h = _silu(conv)
        sz_h = _silu(zm_h)
        add_ref[:, hs] = (skip_ref[:, hs] * xc_h * sz_h).astype(BF16)
        mul_ref[:, hs] = (normg_ref[:, hs] * sz_h).astype(BF16)
        return xc_h.astype(BF16), xm_h.astype(BF16)

    def blockdiag_head(h, xc_b, xm_b):
        hs = slice(h * hd, (h + 1) * hd)
        q_h = _dot(xc_b, wqkv_ref[h]).astype(BF16)
        k_f = _dot(xc_b, wqkv_ref[ml_heads + h])
        v_h = _dot(xm_b, wqkv_ref[2 * ml_heads + h]).astype(BF16)
        qm_ref[:, hs] = q_h
        vm_ref[:, hs] = v_h
        kmt_ref[hs, :] = (jnp.transpose(k_f) * (hd ** -0.5)).astype(BF16)

    def gate_terms(h, xc_b, xm_b):
        hs = slice(h * hd, (h + 1) * hd)
        return _dot(xc_b, wfold_ref[0, hs, :]) + _dot(xm_b, wfold_ref[1, hs, :])

    def select_blocks(gate_t):
        n_rows = MOBA_HEADS * nb_lanes
        gate = gate_t[0:n_rows, :]
        row = lax.broadcasted_iota(jnp.int32, gate.shape, 0)
        n_r = row & (nb_lanes - 1)
        valid = n_r < t
        gate = jnp.where(valid, gate, -jnp.inf)
        cnt = jnp.zeros(gate.shape, jnp.int32)
        for r in range(1, nb_lanes):
            up = pltpu.roll(gate, r, 0)
            cnt = cnt + ((n_r >= r) & (up >= gate)).astype(jnp.int32)
            dn = pltpu.roll(gate, n_rows - r, 0)
            cnt = cnt + ((n_r < nb_lanes - r) & (dn > gate)).astype(jnp.int32)
        keep = (valid & (cnt < n_sel)) | (n_r == t)
        selb_t = jnp.where(keep, 0.0, MASK_VALUE)
        selb = jnp.transpose(jnp.concatenate(
            [selb_t, jnp.full((V7X_LANES - n_rows, tm), MASK_VALUE, F32)], axis=0))
        for h in range(MOBA_HEADS):
            shift = (V7X_LANES - h * nb_lanes) % V7X_LANES
            sel_h = selb if shift == 0 else pltpu.roll(selb, shift, 1)
            qa_ref[:, (2 * h + 1) * moba_hd:(2 * h + 2) * moba_hd] = sel_h.astype(BF16)

    pw = 2 * moba_hd
    qa0, ka0 = x_cols(c_qa, pw), x_cols(c_ka, pw)
    ang_t = invf_ref[...] * pos_ref[...].astype(F32)
    cos_t = jnp.cos(ang_t)
    sin_t = jnp.sin(ang_t)
    cosf = jnp.transpose(jnp.concatenate([cos_t, cos_t], axis=0))
    sins = jnp.transpose(jnp.concatenate([-sin_t, sin_t], axis=0))
    qa1, ka1 = x_cols(c_qa + pw, pw), x_cols(c_ka + pw, pw)
    rots0, means0 = rotary_pair(0, qa0, ka0)
    xm0, zm0 = x_cols(0, hd), x_cols(c_zm, hd)
    rots1, means1 = rotary_pair(1, qa1, ka1)
    q_rots, kmean_rows = rots0 + rots1, means0 + means1
    q_all = jnp.concatenate(q_rots, axis=1)
    km = kmean_scr[...]
    q_hi = q_all.astype(BF16)
    q_lo = (q_all - q_hi.astype(F32)).astype(BF16)
    k_hi = km.astype(BF16)
    k_lo = (km - k_hi.astype(F32)).astype(BF16)
    xz = {0: (xm0, zm0), 1: (x_cols(hd, hd), x_cols(c_zm + hd, hd))}
    gate = _dot_nt(k_hi, q_hi) + _dot_nt(k_hi, q_lo) + _dot_nt(k_lo, q_hi)
    pending = [("xz", h) for h in range(2, ml_heads)] + [("va", None), ("za", None)]
    qkv = []
    va = za = None
    for h in range(ml_heads):
        xc_b, xm_b = conv_gate_head(h, *xz[h])
        if pending:
            kind, arg = pending.pop(0)
            if kind == "xz":
                xz[arg] = (x_cols(arg * hd, hd), x_cols(c_zm + arg * hd, hd))
            elif kind == "va":
                va = x_cols(c_va, moba_width)
            else:
                za = x_cols(c_za, moba_width)
        blockdiag_head(h, xc_b, xm_b)
        qkv.append((xc_b, xm_b))
    if va is None:
        va = x_cols(c_va, moba_width)
    if za is None:
        za = x_cols(c_za, moba_width)
    g = jnp.zeros((tm, V7X_LANES), F32) + bg_ref[...]
    for h in range(ml_heads):
        g = g + gate_terms(h, *qkv[h])
    sza_ref[...] = _silu(za).astype(BF16)
    qc = x_cols(c_qc, mem_width)
    zc = x_cols(c_zc, mem_width)

    nrow = grow_ref.shape[0]
    gt = jnp.transpose(g)[0:V7X_SUBLANES, :]
    sub = lax.broadcasted_iota(jnp.int32, gt.shape, 0)
    tok_i = lax.broadcasted_iota(jnp.int32, gt.shape, 1)
    b = _lane_scan(jnp.where(sub >= ml_heads, _log_sigmoid(gt), 0.0), jnp.add, 0.0, tok_i)
    a = _lane_scan(jnp.where(sub >= ml_heads, pltpu.roll(gt, ml_heads, 0) - b, -jnp.inf),
                   jnp.maximum, -jnp.inf, tok_i)
    rows = jnp.concatenate([jnp.where(sub < ml_heads, gt, b), pltpu.roll(a, ml_heads, 0)], axis=0)
    grow_ref[...] = rows
    gcol_ref[...] = jnp.transpose(
        jnp.concatenate([rows, jnp.zeros((V7X_LANES - nrow, tm), F32)], axis=0))

    km_row = lax.broadcasted_iota(jnp.int32, kmean_scr.shape, 0)
    km_head = lax.broadcasted_iota(jnp.int32, kmean_scr.shape, 1) // moba_hd
    kmean_scr[...] = jnp.where(km_row == km_head * nb_lanes + t,
                               jnp.concatenate(kmean_rows, axis=1), kmean_scr[...])
    select_blocks(gate)
    vat_ref[...] = jnp.transpose(va).astype(BF16)
    qc_ref[...] = (qc * (mem_hd ** -0.5)).astype(BF16)
    szc_ref[...] = _silu(zc).astype(BF16)


def _mlstm_stage(qm_ref, kmt_ref, vm_ref, gcol_ref, grow_ref, add_ref, mul_ref, out_ref,
                 cn_scr, m_scr, *, heads):
    L = qm_ref.shape[0]
    hd = qm_ref.shape[1] // heads
    row_i = lax.broadcasted_iota(jnp.int32, (L, L), 0)
    col_i = lax.broadcasted_iota(jnp.int32, (L, L), 1)
    causal = row_i >= col_i
    gcol = gcol_ref[...]
    grow = grow_ref[...]
    ones = jnp.ones((L, V7X_LANES), BF16)
    head_slices = [slice(h * hd, (h + 1) * hd) for h in range(heads)]

    qk, inter, qn, mm_l, r_l, w_inter_l, v_aug, cn_prevs = [], [], [], [], [], [], [], []
    for h, hs in enumerate(head_slices):
        q = qm_ref[:, hs]
        cn_prevs.append(cn_scr[h])
        qk.append(_dot(q, kmt_ref[hs, :]))
        q_cn = _dot(q, cn_prevs[h].astype(BF16))
        inter.append(q_cn[:, :hd])
        qn.append(q_cn[:, hd:])
    for h, hs in enumerate(head_slices):
        kt = kmt_ref[hs, :]
        v_aug.append(jnp.concatenate([vm_ref[:, hs], ones], axis=1))
        i_row = grow[h:h + 1, :]
        b_row = grow[heads + h:heads + h + 1, :]
        a_rep = jnp.broadcast_to(gcol[:, 2 * heads + h:2 * heads + h + 1], (L, V7X_LANES))
        m_prev = m_scr[h][0:1, 0:1]
        cn_prev = cn_prevs[h]
        r_row = i_row - b_row
        mm = jnp.maximum(m_prev, a_rep)
        r_l.append(r_row)
        mm_l.append(mm)
        w_inter_l.append(jnp.exp(m_prev - mm))
        b_end = b_row[:, L - 1:L]
        log_w = b_end + r_row
        m_new = jnp.maximum(b_end + m_prev, jnp.max(log_w, axis=1, keepdims=True))
        decay = jnp.exp(b_end + m_prev - m_new)
        kw = (kt.astype(F32) * jnp.exp(log_w - m_new)).astype(BF16)
        cn_scr[h] = decay * cn_prev + _dot(kw, v_aug[h])
        m_scr[h] = jnp.broadcast_to(m_new, m_scr.shape[1:])

    wide = lambda rep, width: jnp.concatenate([rep] * (width // V7X_LANES), axis=1)
    s_qk = [(qk[h] * jnp.exp(jnp.where(causal, r_l[h] - wide(mm_l[h], L), -jnp.inf))).astype(BF16)
            for h in range(heads)]
    pv_rs = [_dot(s_qk[h], v_aug[h]) for h in range(heads)]

    for h, hs in enumerate(head_slices):
        b_rep = jnp.broadcast_to(gcol[:, heads + h:heads + h + 1], (L, V7X_LANES))
        num = wide(w_inter_l[h], hd) * inter[h] + pv_rs[h][:, :hd]
        den = w_inter_l[h] * qn[h] + pv_rs[h][:, hd:]
        rec = 1.0 / jnp.maximum(jnp.abs(den), jnp.exp(-(b_rep + mm_l[h])))
        hh = num * wide(rec, hd)
        mu = jnp.mean(hh, axis=1, keepdims=True)
        hc = hh - mu
        var = jnp.mean(hc * hc, axis=1, keepdims=True)
        hn = hc * lax.rsqrt(var + LN_EPS)
        out_ref[:, hs] = (hn * mul_ref[:, hs].astype(F32) + add_ref[:, hs].astype(F32)).astype(BF16)


def _moba_stage(q_ref, k_ref, vt_ref, sz_ref, out_ref, s_scr, m_scr, *, j):
    tq = q_ref.shape[0]
    bs = MOBA_BLOCK
    nblk = k_ref.shape[0] // bs
    aw = q_ref.shape[1] // MOBA_HEADS
    hd = aw // 2
    ones_rows = 2 * V7X_SUBLANES
    heads = [slice(h * aw, (h + 1) * aw) for h in range(MOBA_HEADS)]
    kv_heads = [slice(h * hd, (h + 1) * hd) for h in range(MOBA_HEADS)]
    tag_lane = lax.broadcasted_iota(jnp.int32, (bs, hd), 1)
    ones_blk = jnp.ones((ones_rows, bs), BF16)

    n_slots = s_scr.shape[0]

    def attend(own):
        causal_t = (lax.broadcasted_iota(jnp.int32, (bs, tq), 0)
                    <= lax.broadcasted_iota(jnp.int32, (bs, tq), 1))

        def score_pass(hi):
            slot = hi % n_slots
            mx = None
            for n in range(own + 1):
                k_aug = jnp.concatenate([k_ref[n * bs:(n + 1) * bs, kv_heads[hi]],
                                         (tag_lane == n).astype(BF16)], axis=1)
                s = _dot_nt(k_aug, q_ref[:, heads[hi]])
                if n == own:
                    s = jnp.where(causal_t, s, MASK_VALUE)
                s_scr[slot, n] = s
                sm = jnp.max(s, axis=0, keepdims=True)
                mx = sm if mx is None else jnp.maximum(mx, sm)
            m_scr[slot] = jnp.broadcast_to(mx, m_scr.shape[1:])

        def value_pass(hi):
            slot = hi % n_slots
            m = m_scr[slot][0:1, :]
            acc = None
            for n in range(own + 1):
                p = jnp.exp(s_scr[slot, n] - m).astype(BF16)
                vt_aug = jnp.concatenate([vt_ref[n, kv_heads[hi], :], ones_blk], axis=0)
                pv = _dot(vt_aug, p)
                acc = pv if acc is None else acc + pv
            o_t = acc[0:hd, :] * (1.0 / acc[hd:hd + 1, :])
            out_ref[:, hi * hd:(hi + 1) * hd] = (
                jnp.transpose(o_t) * sz_ref[:, hi * hd:(hi + 1) * hd].astype(F32)).astype(BF16)

        ahead = min(n_slots, MOBA_HEADS)
        for hi in range(ahead):
            score_pass(hi)
        for hi in range(MOBA_HEADS):
            value_pass(hi)
            if hi + ahead < MOBA_HEADS:
                score_pass(hi + ahead)

    for own in range(nblk):
        pl.when(j == own)(functools.partial(attend, own))


def _mix_stage(x_ref, om_ref, oa_ref, qc_ref, szc_ref, kv_scr, wo_ref, lng_ref, lnb_ref, y_ref):
    width = qc_ref.shape[1]
    hd = width // MEM_HEADS
    w_m = om_ref.shape[1]
    w_a = oa_ref.shape[1]
    head_slices = [slice(h * hd, (h + 1) * hd) for h in range(MEM_HEADS)]
    scores = [_dot_nt(qc_ref[:, hs], kv_scr[:, hs]) for hs in head_slices]
    mixed = _dot(om_ref[...], wo_ref[0:w_m, :]) + _dot(oa_ref[...], wo_ref[w_m:w_m + w_a, :])
    probs = [jnp.exp(s - jnp.max(s, axis=1, keepdims=True)) for s in scores]
    outs = [_dot(p.astype(BF16), kv_scr[:, width + h * hd:width + (h + 1) * hd])
            for h, p in enumerate(probs)]
    oc = jnp.concatenate(
        [(outs[h] * (1.0 / jnp.sum(probs[h], axis=1, keepdims=True))
          * szc_ref[:, hs].astype(F32)).astype(BF16) for h, hs in enumerate(head_slices)], axis=1)
    mixed = mixed + _dot(oc, wo_ref[w_m + w_a:, :])
    y = DEEPNORM_ALPHA * x_ref[...] + mixed
    mu = jnp.mean(y, axis=1, keepdims=True)
    yc = y - mu
    var = jnp.mean(yc * yc, axis=1, keepdims=True)
    y_ref[...] = yc * lax.rsqrt(var + LN_EPS) * lng_ref[...] + lnb_ref[...]


def _stage_weights_bf16(pairs, stage_s, sem):
    cr, cc = stage_s.shape[1:]
    chunks = [(src, dst, r0, c0) for src, dst in pairs
              for r0 in range(0, src.shape[0], cr) for c0 in range(0, src.shape[1], cc)]

    def chunk_copy(i):
        src, _, r0, c0 = chunks[i]
        return pltpu.make_async_copy(src.at[pl.ds(r0, cr), pl.ds(c0, cc)], stage_s.at[i % 2], sem.at[i % 2])

    chunk_copy(0).start()
    for i, (_, dst, r0, c0) in enumerate(chunks):
        if i + 1 < len(chunks):
            chunk_copy(i + 1).start()
        chunk_copy(i).wait()
        dst[r0:r0 + cr, c0:c0 + cc] = stage_s[i % 2].astype(BF16)


def _layer_kernel(x_ref, pos_ref, mem_ref, win_hbm, convw_ref, convb_ref, wqkv_ref,
                  wg_ref, bg_ref, normg_ref, skip_ref, invf_ref, wkv_hbm, wo_hbm, lng_ref, lnb_ref,
                  y_ref,
                  qm_s, kmt_s, vm_s, add_s, mul_s, gcol_s, grow_s, qa_s, sza_s, qc_s, szc_s,
                  ka_s, vat_s, om_s, oa_s, xpad_s, kmean_s, cn_s, m_s, score_s, rowmax_s, kv_s, wfold_s,
                  win_ref, wkv_ref, wo_ref, stage_s, stage_sem,
                  *, ml_width, ml_heads, moba_width, mem_width, n_sel, gate_slots):
    t = pl.program_id(1)
    tm = x_ref.shape[0]
    pad = xpad_s.shape[0] - tm
    hd = ml_width // ml_heads

    @pl.when((pl.program_id(0) == 0) & (t == 0))
    def _():
        _stage_weights_bf16([(win_hbm, win_ref), (wkv_hbm, wkv_ref), (wo_hbm, wo_ref)], stage_s, stage_sem)
        for h in range(ml_heads):
            rq, rk, rv = [slice(i * ml_width + h * hd, i * ml_width + (h + 1) * hd) for i in range(3)]
            wq_h, wk_h, wv_h = [wqkv_ref[i * ml_heads + h] for i in range(3)]
            wfold_s[0, rq, :] = (_dot(wq_h, wg_ref[rq, :]) + _dot(wk_h, wg_ref[rk, :])).astype(BF16)
            wfold_s[1, rq, :] = _dot(wv_h, wg_ref[rv, :]).astype(BF16)

    @pl.when(t == 0)
    def _():
        xpad_s[0:pad, :] = jnp.zeros((pad, ml_width), F32)
        kmean_s[...] = jnp.zeros(kmean_s.shape, F32)
        cn_s[...] = jnp.zeros(cn_s.shape, F32)
        m_s[...] = jnp.zeros(m_s.shape, F32)
        kv_s[...] = _dot(mem_ref[...].astype(BF16), wkv_ref[...]).astype(BF16)

    @pl.when(t > 0)
    def _():
        xpad_s[0:pad, :] = xpad_s[tm:tm + pad, :]

    rows = pl.ds(pl.multiple_of(t * tm, tm), tm)
    _proj_stage(x_ref, pos_ref, win_ref, convw_ref, convb_ref, wqkv_ref,
                wfold_s, bg_ref, normg_ref, skip_ref, invf_ref,
                qm_s, kmt_s, vm_s, add_s, mul_s, gcol_s, grow_s,
                qa_s, ka_s.at[rows], vat_s.at[t], sza_s, qc_s, szc_s, xpad_s, kmean_s,
                t=t, ml_width=ml_width, ml_heads=ml_heads, moba_width=moba_width,
                mem_width=mem_width, n_sel=n_sel, gate_slots=gate_slots)
    _mlstm_stage(qm_s, kmt_s, vm_s, gcol_s, grow_s, add_s, mul_s, om_s, cn_s, m_s, heads=ml_heads)
    _moba_stage(qa_s, ka_s, vat_s, sza_s, oa_s, score_s, rowmax_s, j=t)
    _mix_stage(x_ref, om_s, oa_s, qc_s, szc_s, kv_s, wo_ref, lng_ref, lnb_ref, y_ref)


def _layer_call(x, pos, mem, win, convw, convb, wqkv, wg, bg, normg, skip, invf, wkv, wo, lng, lnb,
                *, ml_width, ml_heads, moba_width, mem_width):
    bsz, s, d = x.shape
    m_tok = mem.shape[1]
    tm = TOKEN_TILE
    nt = s // tm
    n_sel = min(MOBA_TOP_K, nt - 1)
    gate_slots = max(V7X_SUBLANES, pl.next_power_of_2(nt))
    assert MOBA_HEADS * gate_slots <= V7X_LANES and s % tm == 0 and ml_heads >= 2
    hd = ml_width // ml_heads
    tok = lambda w: pl.BlockSpec((None, tm, w), lambda b, t: (b, t, 0))

    def resident(a):
        nd = a.ndim
        return pl.BlockSpec(a.shape, lambda b, t, _nd=nd: (0,) * _nd, pipeline_mode=pl.Buffered(1))

    weights = (win, convw, convb, wqkv, wg, bg, normg, skip, invf, wkv, wo, lng, lnb)
    staged = (win, wkv, wo)
    assert all(a.shape[0] % STAGE_ROWS == 0 and a.shape[1] % STAGE_COLS == 0 for a in staged)
    in_specs = ([tok(d),
                 pl.BlockSpec((None, None, 1, tm), lambda b, t: (b, t, 0, 0)),
                 pl.BlockSpec((None, m_tok, d), lambda b, t: (b, 0, 0))]
                + [pl.BlockSpec(memory_space=pl.ANY) if any(a is w for w in staged) else resident(a)
                   for a in weights])
    vmem = pltpu.VMEM
    score_heads = MOBA_HEADS
    scratch = [
        vmem((tm, ml_width), BF16),
        vmem((ml_width, tm), BF16),
        vmem((tm, ml_width), BF16),
        vmem((tm, ml_width), BF16),
        vmem((tm, ml_width), BF16),
        vmem((tm, V7X_LANES), F32),
        vmem((GATE_ROWS, tm), F32),
        vmem((tm, 2 * moba_width), BF16),
        vmem((tm, moba_width), BF16),
        vmem((tm, mem_width), BF16),
        vmem((tm, mem_width), BF16),
        vmem((s, moba_width), BF16),
        vmem((nt, moba_width, tm), BF16),
        vmem((tm, ml_width), BF16),
        vmem((tm, moba_width), BF16),
        vmem((tm + V7X_SUBLANES, ml_width), F32),
        vmem((V7X_LANES, moba_width), F32),
        vmem((ml_heads, hd, hd + V7X_LANES), F32),
        vmem((ml_heads, V7X_SUBLANES, V7X_LANES), F32),
        vmem((score_heads, nt, tm, MOBA_BLOCK), F32),
        vmem((score_heads, V7X_SUBLANES, tm), F32),
        vmem((m_tok, 2 * mem_width), BF16),
        vmem((2, ml_width, V7X_LANES), BF16),
        vmem(win.shape, BF16),
        vmem(wkv.shape, BF16),
        vmem(wo.shape, BF16),
        vmem((2, STAGE_ROWS, STAGE_COLS), F32),
        pltpu.SemaphoreType.DMA((2,)),
    ]
    kern = functools.partial(_layer_kernel, ml_width=ml_width, ml_heads=ml_heads, moba_width=moba_width,
                             mem_width=mem_width, n_sel=n_sel, gate_slots=gate_slots)
    return pl.pallas_call(
        kern,
        grid=(bsz, nt),
        in_specs=in_specs,
        out_specs=tok(d),
        out_shape=jax.ShapeDtypeStruct((bsz, s, d), x.dtype),
        scratch_shapes=scratch,
        compiler_params=pltpu.CompilerParams(dimension_semantics=("arbitrary", "arbitrary"),
                                             vmem_limit_bytes=V7X_VMEM_LIMIT_BYTES),
        name="layer",
    )(x, pos.reshape(bsz, nt, 1, tm), mem, *weights)


def _diag_tiles(w, heads):
    groups, blk, _ = w.shape
    hd = groups // heads * blk
    rows = w.reshape(heads, hd, blk)
    idx = jnp.arange(hd)
    same_group = (idx[:, None] // blk) == (idx[None, :] // blk)
    return jnp.where(same_group, jnp.tile(rows, (1, 1, hd // blk)), 0.0)


def kernel(x, mem, positions, w_in, mlstm_conv_w, mlstm_conv_b, mlstm_wq, mlstm_wk, mlstm_wv, mlstm_w_gates, mlstm_b_gates, mlstm_norm_g, mlstm_skip, w_mem_kv, w_out, ln_g, ln_b):
    bsz, s, d = x.shape
    ml_width = mlstm_conv_w.shape[1]
    ml_heads = mlstm_b_gates.shape[0] // 2
    mem_width = w_mem_kv.shape[1] // 2
    moba_width = (w_in.shape[1] - 2 * ml_width - 2 * mem_width) // 4
    moba_hd = moba_width // MOBA_HEADS
    assert moba_hd == V7X_LANES and mem_width // MEM_HEADS == V7X_LANES and MOBA_HEADS == 4
    assert s % TOKEN_TILE == 0

    row = lambda a: a.reshape(1, -1).astype(F32)
    wqkv = _diag_tiles(jnp.concatenate([mlstm_wq, mlstm_wk, mlstm_wv]), 3 * ml_heads).astype(BF16)
    n_gates = mlstm_w_gates.shape[1]
    wg = jnp.pad(mlstm_w_gates, ((0, 0), (0, V7X_LANES - n_gates))).astype(BF16)
    bg = jnp.pad(mlstm_b_gates, (0, V7X_LANES - n_gates)).reshape(1, -1).astype(F32)
    half = moba_hd // 2
    inv_freq = ROPE_THETA ** (-jnp.arange(half, dtype=F32) * 2.0 / moba_hd)
    invf = inv_freq.reshape(-1, 1)
    pos = positions

    return _layer_call(
        x, pos, mem, w_in.astype(F32), mlstm_conv_w.astype(F32), row(mlstm_conv_b), wqkv, wg, bg,
        row(mlstm_norm_g), row(mlstm_skip), invf, w_mem_kv.astype(F32), w_out.astype(F32),
        row(ln_g), row(ln_b),
        ml_width=ml_width, ml_heads=ml_heads, moba_width=moba_width, mem_width=mem_width)
```
